```python
import functools
import jax
import jax.numpy as jnp
from jax import lax
import numpy as np

D_MODEL = 1024
BATCH = 2
SEQ = 8192
DEPTH = 2
DEC_BATCH = 128
DEC_SEQ = 4
PAST_LEN = 2048
PAGE_SIZE = 128

D_A = D_MODEL // 4
LRU_BLOCKS = 4
LRU_BW = D_A // LRU_BLOCKS
LRU_CONV = 4
LRU_C = 8.0
HEAD_DIM = 64
N_HEADS = (D_MODEL // 2) // HEAD_DIM
N_KV = 2
GROUP = N_HEADS // N_KV
D_B = N_HEADS * HEAD_DIM
KV_W = 2 * N_KV * HEAD_DIM
CMP_BLOCK = 32
SEL_BLOCK = 64
SEL_RATIO = SEL_BLOCK // CMP_BLOCK
TOP_N = 16
WINDOW = 512
ROPE_THETA = 10000.0
D_C = D_MODEL // 4
SC_CONV = 3
MIX_WIDTH = D_A + D_B + D_C
D_IN = 2 * D_A + D_B + 3 * KV_W + 3 * N_HEADS + 3 * D_C
N_GROUPS = 4
EXP_PER_GROUP = 8
N_EXPERTS = N_GROUPS * EXP_PER_GROUP
TOP_K = 2
D_EXPERT = 256
Q_BLOCK = 128
MOE_CHUNK = 512
RMS_EPS = 1e-6
NEG_INF = -1e30
TINY = 1e-30
FORCE = 1e6

kernel_name = 'hymba_hawk_nsa_shortconv_hmoe_step'


def rmsnorm(x, g):
    xf = x.astype(jnp.float32)
    y = xf * lax.rsqrt(jnp.mean(xf * xf, axis=-1, keepdims=True) + RMS_EPS) * g.astype(jnp.float32)
    return y.astype(x.dtype)


def rope(x, pos):
    half = HEAD_DIM // 2
    inv = ROPE_THETA ** (-jnp.arange(half, dtype=jnp.float32) / half)
    ang = pos.astype(jnp.float32)[:, None] * inv[None, :]
    cos = jnp.cos(ang)[:, None, :]
    sin = jnp.sin(ang)[:, None, :]
    x1 = x[..., :half].astype(jnp.float32)
    x2 = x[..., half:].astype(jnp.float32)
    return jnp.concatenate([x1 * cos - x2 * sin, x2 * cos + x1 * sin], axis=-1).astype(x.dtype)


def masked_softmax(s, mask):
    s = jnp.where(mask, s, NEG_INF)
    m = jnp.max(s, axis=-1, keepdims=True)
    p = jnp.exp(s - m) * mask
    return p / jnp.maximum(jnp.sum(p, axis=-1, keepdims=True), TINY)


def split_proj(proj):
    sizes = (D_A, D_A, D_B, KV_W, KV_W, KV_W, 3 * N_HEADS, D_C, D_C, D_C)
    parts = []
    off = 0
    for s in sizes:
        parts.append(proj[..., off:off + s])
        off += s
    return parts


def causal_dwconv(x, buf, w):
    width = w.shape[0]
    T = x.shape[1]
    xp = jnp.concatenate([buf.astype(x.dtype), x], axis=1)
    y = w[0] * xp[:, :T]
    for j in range(1, width):
        y = y + w[j] * xp[:, j:j + T]
    return y, xp[:, T:]


def rg_lru(xc, h0, wa, ba, wx, bx, lam):
    B, T, _ = xc.shape
    xb = xc.reshape(B, T, LRU_BLOCKS, LRU_BW)
    r = jax.nn.sigmoid(jnp.einsum('btnd,nde->btne', xb, wa).reshape(B, T, D_A) + ba).astype(jnp.float32)
    i = jax.nn.sigmoid(jnp.einsum('btnd,nde->btne', xb, wx).reshape(B, T, D_A) + bx)
    log_a = -LRU_C * r * jax.nn.softplus(-lam.astype(jnp.float32))
    a = jnp.exp(log_a)
    u = jnp.sqrt(-jnp.expm1(2.0 * log_a)) * (i * xc).astype(jnp.float32)

    def step(h, au):
        a_t, u_t = au
        h = a_t * h + u_t
        return h, h

    h_last, hs = lax.scan(step, h0.astype(jnp.float32), (jnp.swapaxes(a, 0, 1), jnp.swapaxes(u, 0, 1)))
    return jnp.swapaxes(hs, 0, 1).astype(xc.dtype), h_last.astype(h0.dtype)


def compress_blocks(kv, wk, wv):
    B, T = kv.shape[:2]
    nc = T // CMP_BLOCK
    rows = kv[:, :nc * CMP_BLOCK].reshape(B, nc, CMP_BLOCK, 2, N_KV, HEAD_DIM)
    kc = jnp.einsum('bnlkd,l->bnkd', rows[:, :, :, 0], wk)
    vc = jnp.einsum('bnlkd,l->bnkd', rows[:, :, :, 1], wv)
    return kc, vc


def selection_blocks(kv):
    B, T = kv.shape[:2]
    ns = -(-T // SEL_BLOCK)
    kvp = jnp.pad(kv, ((0, 0), (0, ns * SEL_BLOCK - T), (0, 0), (0, 0), (0, 0)))
    kvp = kvp.reshape(B, ns, SEL_BLOCK, 2, N_KV, HEAD_DIM)
    ks = jnp.transpose(kvp[:, :, :, 0], (0, 3, 1, 2, 4))
    vs = jnp.transpose(kvp[:, :, :, 1], (0, 3, 1, 2, 4))
    return ks, vs


def nsa_block(q, pos_q, kc, vc, ks_bl, vs_bl, kw, vw, pos_w):
    B, Q = q.shape[:2]
    nc = kc.shape[1]
    ns = ks_bl.shape[2]
    qf = q.astype(jnp.float32) * (HEAD_DIM ** -0.5)
    s_c = jnp.einsum('bqkgd,bnkd->bqkgn', qf, kc)
    blk_end = (jnp.arange(nc, dtype=jnp.int32) + 1) * CMP_BLOCK - 1
    m_c = blk_end[None, :] <= pos_q[:, None]
    p_c = masked_softmax(s_c, m_c[None, :, None, None, :])
    o_c = jnp.einsum('bqkgn,bnkd->bqkgd', p_c, vc)
    imp = jnp.sum(p_c, axis=3)
    imp = jnp.pad(imp, ((0, 0), (0, 0), (0, 0), (0, ns * SEL_RATIO - nc)))
    imp = imp.reshape(B, Q, N_KV, ns, SEL_RATIO).sum(-1)
    blk = jnp.arange(ns, dtype=jnp.int32)[None, :]
    cur = (pos_q // SEL_BLOCK)[:, None]
    valid = blk <= cur
    forced = (blk == 0) | (blk == cur) | (blk == cur - 1)
    imp = jnp.where(forced[None, :, None, :], FORCE, imp)
    imp = jnp.where(valid[None, :, None, :], imp, NEG_INF)
    n_top = min(TOP_N, ns)
    _, idx = lax.top_k(imp, n_top)
    idx = jnp.transpose(idx, (0, 2, 1, 3))
    bi = jnp.arange(B)[:, None, None, None]
    hi = jnp.arange(N_KV)[None, :, None, None]
    ks = ks_bl[bi, hi, idx]
    vs = vs_bl[bi, hi, idx]
    s_s = jnp.einsum('bqkgd,bkqnld->bqkgnl', qf, ks).reshape(B, Q, N_KV, GROUP, n_top * SEL_BLOCK)
    pos_s = idx[..., None] * SEL_BLOCK + jnp.arange(SEL_BLOCK, dtype=jnp.int32)
    m_s = pos_s <= pos_q[None, None, :, None, None]
    m_s = jnp.transpose(m_s, (0, 2, 1, 3, 4)).reshape(B, Q, N_KV, 1, n_top * SEL_BLOCK)
    p_s = masked_softmax(s_s, m_s).reshape(B, Q, N_KV, GROUP, n_top, SEL_BLOCK)
    o_s = jnp.einsum('bqkgnl,bkqnld->bqkgd', p_s, vs)
    s_w = jnp.einsum('bqkgd,btkd->bqkgt', qf, kw)
    dlt = pos_q[:, None] - pos_w[None, :]
    m_w = (dlt >= 0) & (dlt <= WINDOW) & (pos_w[None, :] >= 0)
    p_w = masked_softmax(s_w, m_w[None, :, None, None, :])
    o_w = jnp.einsum('bqkgt,btkd->bqkgd', p_w, vw)
    return o_c, o_s, o_w


def nsa_prompt(q, kvc, kvs, kvw, cmp_wk, cmp_wv):
    B, S = q.shape[:2]
    kc, vc = compress_blocks(kvc, cmp_wk, cmp_wv)
    ks_bl, vs_bl = selection_blocks(kvs)
    kw_pad = jnp.pad(kvw, ((0, 0), (WINDOW, 0), (0, 0), (0, 0), (0, 0)))
    n_blk = S // Q_BLOCK
    qb = jnp.moveaxis(q.reshape(B, n_blk, Q_BLOCK, N_KV, GROUP, HEAD_DIM), 1, 0)
    starts = jnp.arange(n_blk, dtype=jnp.int32) * Q_BLOCK

    def one_block(args):
        q_blk, s0 = args
        pos_q = s0 + jnp.arange(Q_BLOCK, dtype=jnp.int32)
        band = lax.dynamic_slice_in_dim(kw_pad, s0, WINDOW + Q_BLOCK, axis=1)
        pos_w = s0 - WINDOW + jnp.arange(WINDOW + Q_BLOCK, dtype=jnp.int32)
        return nsa_block(q_blk, pos_q, kc, vc, ks_bl, vs_bl, band[:, :, 0], band[:, :, 1], pos_w)

    o_c, o_s, o_w = lax.map(one_block, (qb, starts))
    o_c = jnp.moveaxis(o_c, 0, 1).reshape(B, S, N_HEADS, HEAD_DIM)
    o_s = jnp.moveaxis(o_s, 0, 1).reshape(B, S, N_HEADS, HEAD_DIM)
    o_w = jnp.moveaxis(o_w, 0, 1).reshape(B, S, N_HEADS, HEAD_DIM)
    return o_c, o_s, o_w, kvw[:, S - min(WINDOW, S):]


def nsa_sample(q, kvc, kvs, kvw, pool_c, pool_s, win_buf, page_table, cmp_wk, cmp_wv):
    B, T = q.shape[:2]
    past = page_table.shape[1] * pool_c.shape[1]
    rows_c = pool_c[page_table].reshape(B, past, 2, N_KV, HEAD_DIM)
    rows_s = pool_s[page_table].reshape(B, past, 2, N_KV, HEAD_DIM)
    kc, vc = compress_blocks(jnp.concatenate([rows_c, kvc.astype(rows_c.dtype)], axis=1), cmp_wk, cmp_wv)
    ks_bl, vs_bl = selection_blocks(jnp.concatenate([rows_s, kvs.astype(rows_s.dtype)], axis=1))
    n_buf = win_buf.shape[1]
    band = jnp.concatenate([win_buf, kvw.astype(win_buf.dtype)], axis=1)
    pos_w = past - n_buf + jnp.arange(n_buf + T, dtype=jnp.int32)
    pos_q = past + jnp.arange(T, dtype=jnp.int32)
    o_c, o_s, o_w = nsa_block(q.reshape(B, T, N_KV, GROUP, HEAD_DIM), pos_q, kc, vc, ks_bl, vs_bl,
                              band[:, :, 0], band[:, :, 1], pos_w)
    o_c = o_c.reshape(B, T, N_HEADS, HEAD_DIM)
    o_s = o_s.reshape(B, T, N_HEADS, HEAD_DIM)
    o_w = o_w.reshape(B, T, N_HEADS, HEAD_DIM)
    return o_c, o_s, o_w, band[:, T:]


def hier_moe(x, rg_w, rg_b, re_w, re_b, e_gate, e_up, e_down):
    N = x.shape[0]
    p_grp = jax.nn.softmax((x @ rg_w + rg_b).astype(jnp.float32), axis=-1)
    g_sel = jnp.argmax(p_grp, axis=-1)
    g_wt = jnp.take_along_axis(p_grp, g_sel[:, None], axis=-1)
    le = (x @ re_w + re_b).astype(jnp.float32).reshape(N, N_GROUPS, EXP_PER_GROUP)
    le = jnp.take_along_axis(le, g_sel[:, None, None], axis=1)[:, 0]
    top_p, top_i = lax.top_k(jax.nn.softmax(le, axis=-1), TOP_K)
    wts = g_wt * top_p / jnp.sum(top_p, axis=-1, keepdims=True)
    eid = g_sel[:, None] * EXP_PER_GROUP + top_i
    combine = jnp.sum(jax.nn.one_hot(eid, N_EXPERTS, dtype=jnp.float32) * wts[..., None], axis=1)
    hg = jnp.einsum('nd,edf->nef', x, e_gate)
    hu = jnp.einsum('nd,edf->nef', x, e_up)
    act = jax.nn.silu(hg) * hu * combine[..., None].astype(x.dtype)
    return jnp.einsum('nef,efd->nd', act, e_down)


def channel_mixer(h, rg_w, rg_b, re_w, re_b, e_gate, e_up, e_down):
    B, T, D = h.shape
    n = B * T
    chunk = MOE_CHUNK if n % MOE_CHUNK == 0 else n
    hc = h.reshape(n // chunk, chunk, D)
    y = lax.map(lambda t: hier_moe(t, rg_w, rg_b, re_w, re_b, e_gate, e_up, e_down), hc)
    return y.reshape(B, T, D)


def layer(x, pos, lru_h0, lru_buf, sc_buf, attend, norm1_g, w_in, lru_conv_w, lru_conv_b, lru_wa, lru_ba,
          lru_wx, lru_bx, lru_lambda, sc_conv_w, w_out, norm2_g, rg_w, rg_b, re_w, re_b, e_gate, e_up, e_down):
    B, T, _ = x.shape
    h = rmsnorm(x, norm1_g)
    proj = h @ w_in
    xa, ga, q, kvc, kvs, kvw, gate_logits, c_in, c_out, cx = split_proj(proj)
    xc, new_lru_buf = causal_dwconv(xa, lru_buf, lru_conv_w)
    hs, new_h = rg_lru(xc + lru_conv_b, lru_h0, lru_wa, lru_ba, lru_wx, lru_bx, lru_lambda)
    out_a = hs * jax.nn.gelu(ga)
    q = rope(q.reshape(B, T, N_HEADS, HEAD_DIM), pos)

    def rope_keys(kv):
        kv = kv.reshape(B, T, 2, N_KV, HEAD_DIM)
        return jnp.stack([rope(kv[:, :, 0], pos), kv[:, :, 1]], axis=2)

    kvc, kvs, kvw = rope_keys(kvc), rope_keys(kvs), rope_keys(kvw)
    o_c, o_s, o_w, new_win = attend(q, kvc, kvs, kvw)
    g = jax.nn.sigmoid(gate_logits.astype(jnp.float32)).reshape(B, T, N_HEADS, 3)
    out_b = (g[..., 0:1] * o_c + g[..., 1:2] * o_s + g[..., 2:3] * o_w).reshape(B, T, D_B).astype(x.dtype)
    uc, new_sc_buf = causal_dwconv(c_in * cx, sc_buf, sc_conv_w)
    out_c = c_out * uc
    x = x + jnp.concatenate([out_a, out_b, out_c], axis=-1) @ w_out
    x = x + channel_mixer(rmsnorm(x, norm2_g), rg_w, rg_b, re_w, re_b, e_gate, e_up, e_down)
    return x, (kvc, kvs, new_win, new_h, new_lru_buf, new_sc_buf)


def setup_inputs(seed: int = 0) -> dict:
    key = jax.random.key(seed)
    keys = jax.random.split(key, 32)
    f32 = jnp.float32

    def nrm(i, shape, scale):
        return scale * jax.random.normal(keys[i], shape, f32)

    n_pages = PAST_LEN // PAGE_SIZE
    n_used = DEC_BATCH * n_pages
    n_phys = n_used + max(1, n_used // 4)
    w_buf = min(WINDOW, PAST_LEN)
    page_table = jax.random.permutation(keys[0], n_phys)[:n_used].reshape(DEC_BATCH, n_pages).astype(jnp.int32)
    u = jax.random.uniform(keys[1], (DEPTH, D_A), f32, 0.9, 0.999)
    return {
        'x_prompt': nrm(2, (BATCH, SEQ, D_MODEL), 1.0),
        'x_sample': nrm(3, (DEC_BATCH, DEC_SEQ, D_MODEL), 1.0),
        'cache_kv_cmp': nrm(4, (DEPTH, n_phys, PAGE_SIZE, 2, N_KV, HEAD_DIM), 1.0),
        'cache_kv_sel': nrm(5, (DEPTH, n_phys, PAGE_SIZE, 2, N_KV, HEAD_DIM), 1.0),
        'cache_kv_win': nrm(6, (DEPTH, DEC_BATCH, w_buf, 2, N_KV, HEAD_DIM), 1.0),
        'state_lru_h': nrm(7, (DEPTH, DEC_BATCH, D_A), 0.5),
        'state_lru_conv': nrm(8, (DEPTH, DEC_BATCH, LRU_CONV - 1, D_A), 1.0),
        'state_sconv': nrm(9, (DEPTH, DEC_BATCH, SC_CONV - 1, D_C), 1.0),
        'page_table': page_table,
        'norm1_g': 1.0 + nrm(10, (DEPTH, D_MODEL), 0.05),
        'w_in': nrm(11, (DEPTH, D_MODEL, D_IN), D_MODEL ** -0.5),
        'lru_conv_w': nrm(12, (DEPTH, LRU_CONV, D_A), 0.5),
        'lru_conv_b': nrm(13, (DEPTH, D_A), 0.02),
        'lru_wa': nrm(14, (DEPTH, LRU_BLOCKS, LRU_BW, LRU_BW), LRU_BW ** -0.5),
        'lru_ba': nrm(15, (DEPTH, D_A), 0.1),
        'lru_wx': nrm(16, (DEPTH, LRU_BLOCKS, LRU_BW, LRU_BW), LRU_BW ** -0.5),
        'lru_bx': nrm(17, (DEPTH, D_A), 0.1),
        'lru_lambda': jnp.log(u) - jnp.log1p(-u),
        'nsa_cmp_wk': (1.0 + nrm(18, (DEPTH, CMP_BLOCK), 0.1)) * CMP_BLOCK ** -0.5,
        'nsa_cmp_wv': (1.0 + nrm(19, (DEPTH, CMP_BLOCK), 0.1)) * CMP_BLOCK ** -0.5,
        'sc_conv_w': nrm(20, (DEPTH, SC_CONV, D_C), SC_CONV ** -0.5),
        'w_out': nrm(21, (DEPTH, MIX_WIDTH, D_MODEL), MIX_WIDTH ** -0.5),
        'norm2_g': 1.0 + nrm(22, (DEPTH, D_MODEL), 0.05),
        'router_group_w': nrm(23, (DEPTH, D_MODEL, N_GROUPS), D_MODEL ** -0.5),
        'router_group_b': nrm(24, (DEPTH, N_GROUPS), 0.01),
        'router_exp_w': nrm(25, (DEPTH, D_MODEL, N_EXPERTS), D_MODEL ** -0.5),
        'router_exp_b': nrm(26, (DEPTH, N_EXPERTS), 0.01),
        'exp_w_gate': nrm(27, (DEPTH, N_EXPERTS, D_MODEL, D_EXPERT), D_MODEL ** -0.5),
        'exp_w_up': nrm(28, (DEPTH, N_EXPERTS, D_MODEL, D_EXPERT), D_MODEL ** -0.5),
        'exp_w_down': nrm(29, (DEPTH, N_EXPERTS, D_EXPERT, D_MODEL), D_EXPERT ** -0.5),
        'norm_f_g': 1.0 + nrm(30, (D_MODEL,), 0.05),
    }


def reference(x_prompt, x_sample, cache_kv_cmp, cache_kv_sel, cache_kv_win, state_lru_h, state_lru_conv,
              state_sconv, page_table, norm1_g, w_in, lru_conv_w, lru_conv_b, lru_wa, lru_ba, lru_wx, lru_bx,
              lru_lambda, nsa_cmp_wk, nsa_cmp_wv, sc_conv_w, w_out, norm2_g, router_group_w, router_group_b,
              router_exp_w, router_exp_b, exp_w_gate, exp_w_up, exp_w_down, norm_f_g):
    B, S, _ = x_prompt.shape
    T = x_sample.shape[1]
    past = page_table.shape[1] * cache_kv_cmp.shape[2]
    pos_p = jnp.arange(S, dtype=jnp.int32)
    pos_s = past + jnp.arange(T, dtype=jnp.int32)
    h0_p = jnp.zeros((B, D_A), x_prompt.dtype)
    lbuf_p = jnp.zeros((B, LRU_CONV - 1, D_A), x_prompt.dtype)
    sbuf_p = jnp.zeros((B, SC_CONV - 1, D_C), x_prompt.dtype)
    xp, xs = x_prompt, x_sample
    st_p, st_s = [], []
    for l in range(DEPTH):
        wts = (norm1_g[l], w_in[l], lru_conv_w[l], lru_conv_b[l], lru_wa[l], lru_ba[l], lru_wx[l], lru_bx[l],
               lru_lambda[l], sc_conv_w[l], w_out[l], norm2_g[l], router_group_w[l], router_group_b[l],
               router_exp_w[l], router_exp_b[l], exp_w_gate[l], exp_w_up[l], exp_w_down[l])
        attend_p = functools.partial(nsa_prompt, cmp_wk=nsa_cmp_wk[l], cmp_wv=nsa_cmp_wv[l])
        attend_s = functools.partial(nsa_sample, pool_c=cache_kv_cmp[l], pool_s=cache_kv_sel[l],
                                     win_buf=cache_kv_win[l], page_table=page_table,
                                     cmp_wk=nsa_cmp_wk[l], cmp_wv=nsa_cmp_wv[l])
        xp, sp = layer(xp, pos_p, h0_p, lbuf_p, sbuf_p, attend_p, *wts)
        xs, ss = layer(xs, pos_s, state_lru_h[l], state_lru_conv[l], state_sconv[l], attend_s, *wts)
        st_p.append(sp)
        st_s.append(ss)
    y_prompt = rmsnorm(xp, norm_f_g)
    y_sample = rmsnorm(xs, norm_f_g)
    new_kv_cmp_p = jnp.stack([s[0] for s in st_p])
    new_kv_cmp_s = jnp.stack([s[0] for s in st_s])
    new_kv_sel_p = jnp.stack([s[1] for s in st_p])
    new_kv_sel_s = jnp.stack([s[1] for s in st_s])
    new_kv_win_p = jnp.stack([s[2] for s in st_p])
    new_kv_win_s = jnp.stack([s[2] for s in st_s])
    new_lru_h_p = jnp.stack([s[3] for s in st_p])
    new_lru_h_s = jnp.stack([s[3] for s in st_s])
    new_lru_conv_p = jnp.stack([s[4] for s in st_p])
    new_lru_conv_s = jnp.stack([s[4] for s in st_s])
    new_sconv_p = jnp.stack([s[5] for s in st_p])
    new_sconv_s = jnp.stack([s[5] for s in st_s])
    return (y_prompt, y_sample, new_kv_cmp_p, new_kv_cmp_s, new_kv_sel_p, new_kv_sel_s, new_kv_win_p,
            new_kv_win_s, new_lru_h_p, new_lru_h_s, new_lru_conv_p, new_lru_conv_s, new_sconv_p, new_sconv_s)
```

```python
import functools

import jax
import jax.numpy as jnp
from jax import lax
from jax.experimental import pallas as pl
from jax.experimental.pallas import tpu as pltpu

F32 = jnp.float32
MXU_DTYPE = jnp.bfloat16

HEAD_DIM = 64
N_HEADS = 8
N_KV = 2
GROUP = N_HEADS // N_KV
CMP_BLOCK = 32
SEL_BLOCK = 64
TOP_N = 16
WINDOW = 512
Q_BLOCK = 128
ROPE_THETA = 10000.0
LRU_C = 8.0
N_GROUPS = 4
EXP_PER_GROUP = 8
N_EXPERTS = N_GROUPS * EXP_PER_GROUP
RMS_EPS = 1e-6
NEG_INF = -1e30
TINY = 1e-30
FORCE = 1e6

LANES = 128
VMEM_LIMIT = 56 * 2 ** 20
TOKEN_TILE = 512
TIME_CHUNK = 256
KEY_TILE = 256
EXPERT_TILE = 256
COMBINE_TILE = 256

_NT = (((1,), (1,)), ((), ()))


def _cparams(*sem):
    return pltpu.CompilerParams(dimension_semantics=sem, vmem_limit_bytes=VMEM_LIMIT)


def _precision_for(w_dtype):
    return lax.Precision.HIGHEST if w_dtype == F32 else None


def _rmsnorm(x, g):
    return x * lax.rsqrt(jnp.mean(x * x, axis=-1, keepdims=True) + RMS_EPS) * g


def _swap_halves(x):
    w = x.shape[-1]
    lane = lax.broadcasted_iota(jnp.int32, x.shape, x.ndim - 1)
    first = (lane % HEAD_DIM) < HEAD_DIM // 2
    return jnp.where(first, pltpu.roll(x, w - HEAD_DIM // 2, axis=1), pltpu.roll(x, HEAD_DIM // 2, axis=1))


def _masked_softmax(s, mask):
    s = jnp.where(mask, s, NEG_INF)
    m = jnp.max(s, axis=-1, keepdims=True)
    p = jnp.where(mask, jnp.exp(s - m), 0.0)
    return p / jnp.maximum(jnp.sum(p, axis=-1, keepdims=True), TINY)


def _top_n_mask(w, n_top):
    idxf = lax.broadcasted_iota(jnp.int32, w.shape, 1).astype(F32)

    def body(_, carry):
        w, sel = carry
        m = jnp.max(w, axis=-1, keepdims=True)
        first = jnp.min(jnp.where(w == m, idxf, 1e9), axis=-1, keepdims=True)
        pick = idxf == first
        return jnp.where(pick, -jnp.inf, w), jnp.where(pick, 1.0, sel)

    _, sel = lax.fori_loop(0, n_top, body, (w, jnp.zeros_like(w)))
    return sel


def _block_importance_mask(imp, pos_q, n_sel):
    blk = lax.broadcasted_iota(jnp.int32, imp.shape, 1)
    cur = pos_q // SEL_BLOCK
    forced = (blk == 0) | (blk == cur) | (blk == cur - 1)
    valid = (blk <= cur) & (blk < n_sel)
    w = jnp.where(forced, FORCE, imp)
    w = jnp.where(valid, w, NEG_INF)
    w = jnp.where(blk < n_sel, w, -jnp.inf)
    return _top_n_mask(w, min(TOP_N, n_sel))


def _in_proj_kernel(x_ref, g_ref, w_ref, cos_ref, sin_ref,
                    lru_ref, sc_ref, gate_ref, q_ref, kvc_ref, kvs_ref, kvw_ref, kvsb_ref, kvwb_ref,
                    *, d_a, d_b, kv_w, d_c):
    hb = _rmsnorm(x_ref[...], g_ref[...]).astype(w_ref.dtype)
    prec = _precision_for(w_ref.dtype)

    def mm(a, b):
        return jnp.dot(hb, w_ref[:, a:b], preferred_element_type=F32, precision=prec)

    cos = cos_ref[...]
    sin = sin_ref[...]
    off = 2 * d_a
    lru_ref[...] = mm(0, off)
    q = mm(off, off + d_b)
    reps = d_b // LANES
    cos_q = jnp.concatenate([cos] * reps, axis=1)
    sin_q = jnp.concatenate([sin] * reps, axis=1)
    q_ref[...] = (q * cos_q + _swap_halves(q) * sin_q) * (HEAD_DIM ** -0.5)
    off += d_b
    for ref, bref in ((kvc_ref, None), (kvs_ref, kvsb_ref), (kvw_ref, kvwb_ref)):
        kv = mm(off, off + kv_w)
        k = kv[:, :kv_w // 2]
        kv = jnp.concatenate([k * cos + _swap_halves(k) * sin, kv[:, kv_w // 2:]], axis=1)
        ref[...] = kv
        if bref is not None:
            bref[...] = kv.astype(MXU_DTYPE)
        off += kv_w
    sc_ref[...] = mm(off, off + 3 * d_c)
    off += 3 * d_c
    gate_ref[...] = jax.nn.sigmoid(mm(off, off + LANES))


def _in_proj(x, g1, w_r, cos_t, sin_t, n_prompt, seq, *, d_a, d_b, kv_w, d_c):
    n, d = x.shape
    tm = TOKEN_TILE
    n_ptiles = n_prompt // tm
    n_stiles = seq // tm

    def tab_map(i):
        return (jnp.where(i < n_ptiles, i % n_stiles, n_stiles), 0)

    row = lambda w: pl.BlockSpec((tm, w), lambda i: (i, 0))
    widths = (2 * d_a, 3 * d_c, LANES, d_b, kv_w, kv_w, kv_w, kv_w, kv_w)
    dtypes = (F32, F32, F32, F32, F32, F32, F32, MXU_DTYPE, MXU_DTYPE)
    return pl.pallas_call(
        functools.partial(_in_proj_kernel, d_a=d_a, d_b=d_b, kv_w=kv_w, d_c=d_c),
        grid=(n // tm,),
        in_specs=[row(d),
                  pl.BlockSpec((1, d), lambda i: (0, 0)),
                  pl.BlockSpec(w_r.shape, lambda i: (0, 0)),
                  pl.BlockSpec((tm, LANES), tab_map),
                  pl.BlockSpec((tm, LANES), tab_map)],
        out_specs=[row(w) for w in widths],
        out_shape=[jax.ShapeDtypeStruct((n, w), dt) for w, dt in zip(widths, dtypes)],
        compiler_params=_cparams("parallel"),
        name="in_proj",
    )(x, g1, w_r, cos_t, sin_t)


def _scan_rows(a, u):
    t = a.shape[0]
    row = lax.broadcasted_iota(jnp.int32, a.shape, 0)
    d = 1
    while d < t:
        keep = row >= d
        a_sh = jnp.where(keep, pltpu.roll(a, d, axis=0), 1.0)
        u_sh = jnp.where(keep, pltpu.roll(u, d, axis=0), 0.0)
        u = a * u_sh + u
        a = a * a_sh
        d *= 2
    return a, u


def _lru_gates(xc, wa_ref, ba_ref, wx_ref, bx_ref, lam_ref):
    xb = xc.astype(MXU_DTYPE)
    r = jax.nn.sigmoid(jnp.dot(xb, wa_ref[...], preferred_element_type=F32) + ba_ref[...])
    i = jax.nn.sigmoid(jnp.dot(xb, wx_ref[...], preferred_element_type=F32) + bx_ref[...])
    log_a = -LRU_C * r * jax.nn.softplus(-lam_ref[...])
    a = jnp.exp(log_a)
    th = jnp.tanh(log_a)
    u = jnp.sqrt(-2.0 * th / (1.0 - th)) * (i * xc)
    return a, u


def _seq_mix_kernel(lru_ref, sc_ref, h0_ref, lb0_ref, sb0_ref, cw_ref, cb_ref, wa_ref, ba_ref, wx_ref, bx_ref,
                    lam_ref, scw_ref, oa_ref, oc_ref, hn_ref, lbn_ref, sbn_ref, xpad, vpad, hcar, *, d_a, d_c):
    c = pl.program_id(1)
    tc = lru_ref.shape[0]
    nlb = lb0_ref.shape[1]
    nsb = sb0_ref.shape[1]

    @pl.when(c == 0)
    def _():
        xpad[0:8, :] = jnp.zeros((8, d_a), F32)
        xpad[8 - nlb:8, :] = lb0_ref[0]
        vpad[0:8, :] = jnp.zeros((8, d_c), F32)
        vpad[8 - nsb:8, :] = sb0_ref[0]
        hcar[...] = h0_ref[0]

    xa = lru_ref[:, 0:d_a]
    ga = lru_ref[:, d_a:2 * d_a]
    xpad[8:8 + tc, :] = xa
    xc = cw_ref[0:1, :] * xpad[8 - nlb:8 - nlb + tc, :]
    for j in range(1, nlb):
        xc = xc + cw_ref[j:j + 1, :] * xpad[8 - nlb + j:8 - nlb + j + tc, :]
    xc = xc + cw_ref[nlb:nlb + 1, :] * xa + cb_ref[...]
    a, u = _lru_gates(xc, wa_ref, ba_ref, wx_ref, bx_ref, lam_ref)
    a_cum, hs = _scan_rows(a, u)
    hs = hs + a_cum * hcar[...]
    hcar[...] = hs[tc - 1:tc, :]
    oa_ref[...] = hs * jax.nn.gelu(ga)

    v = sc_ref[:, 0:d_c] * sc_ref[:, 2 * d_c:3 * d_c]
    vpad[8:8 + tc, :] = v
    uc = scw_ref[0:1, :] * vpad[8 - nsb:8 - nsb + tc, :]
    for j in range(1, nsb):
        uc = uc + scw_ref[j:j + 1, :] * vpad[8 - nsb + j:8 - nsb + j + tc, :]
    uc = uc + scw_ref[nsb:nsb + 1, :] * v
    oc_ref[...] = sc_ref[:, d_c:2 * d_c] * uc

    hn_ref[0] = hs[tc - 1:tc, :]
    lbn_ref[0] = xpad[8 + tc - nlb:8 + tc, :]
    sbn_ref[0] = vpad[8 + tc - nsb:8 + tc, :]
    xpad[0:8, :] = xpad[tc:tc + 8, :]
    vpad[0:8, :] = vpad[tc:tc + 8, :]


def _seq_mix_prompt(lru_in, sc_in, h0, lb0, sb0, wts, bsz, seq, *, d_a, d_c):
    n = bsz * seq
    tc = TIME_CHUNK
    nch = seq // tc
    row = lambda w: pl.BlockSpec((tc, w), lambda b, c: (b * nch + c, 0))
    full = lambda a: pl.BlockSpec(a.shape, lambda b, c: (0,) * a.ndim)
    state = lambda r, w: pl.BlockSpec((1, r, w), lambda b, c: (b, 0, 0))
    nlb, nsb = lb0.shape[1], sb0.shape[1]
    return pl.pallas_call(
        functools.partial(_seq_mix_kernel, d_a=d_a, d_c=d_c),
        grid=(bsz, nch),
        in_specs=[row(2 * d_a), row(3 * d_c), state(1, d_a), state(nlb, d_a), state(nsb, d_c)]
                 + [full(w) for w in wts],
        out_specs=[row(d_a), row(d_c), state(1, d_a), state(nlb, d_a), state(nsb, d_c)],
        out_shape=[jax.ShapeDtypeStruct((n, d_a), F32), jax.ShapeDtypeStruct((n, d_c), F32),
                   jax.ShapeDtypeStruct((bsz, 1, d_a), F32), jax.ShapeDtypeStruct((bsz, nlb, d_a), F32),
                   jax.ShapeDtypeStruct((bsz, nsb, d_c), F32)],
        scratch_shapes=[pltpu.VMEM((tc + 8, d_a), F32), pltpu.VMEM((tc + 8, d_c), F32), pltpu.VMEM((1, d_a), F32)],
        compiler_params=_cparams("arbitrary", "arbitrary"),
        name="seq_mix_prompt",
    )(lru_in, sc_in, h0, lb0, sb0, *wts)


def _seq_mix_sample_kernel(lru_ref, sc_ref, h0_ref, lb0_ref, sb0_ref, cw_ref, cb_ref, wa_ref, ba_ref, wx_ref,
                           bx_ref, lam_ref, scw_ref, oa_ref, oc_ref, hn_ref, lbn_ref, sbn_ref, *, d_a, d_c):
    t_len = lru_ref.shape[0]
    nlb = lb0_ref.shape[0]
    nsb = sb0_ref.shape[0]
    xs = [lb0_ref[j] for j in range(nlb)] + [lru_ref[t][:, 0:d_a] for t in range(t_len)]
    vs = [sb0_ref[j] for j in range(nsb)] + [sc_ref[t][:, 0:d_c] * sc_ref[t][:, 2 * d_c:3 * d_c] for t in range(t_len)]
    h = h0_ref[...]
    for t in range(t_len):
        xc = cw_ref[0:1, :] * xs[t]
        for j in range(1, nlb + 1):
            xc = xc + cw_ref[j:j + 1, :] * xs[t + j]
        xc = xc + cb_ref[...]
        a, u = _lru_gates(xc, wa_ref, ba_ref, wx_ref, bx_ref, lam_ref)
        h = a * h + u
        oa_ref[t] = h * jax.nn.gelu(lru_ref[t][:, d_a:2 * d_a])
        uc = scw_ref[0:1, :] * vs[t]
        for j in range(1, nsb + 1):
            uc = uc + scw_ref[j:j + 1, :] * vs[t + j]
        oc_ref[t] = sc_ref[t][:, d_c:2 * d_c] * uc
    hn_ref[...] = h
    for j in range(nlb):
        lbn_ref[j] = xs[t_len + j]
    for j in range(nsb):
        sbn_ref[j] = vs[t_len + j]


def _seq_mix_sample(lru_t, sc_t, h0, lb0_t, sb0_t, wts, *, d_a, d_c):
    t_len, bsz, _ = lru_t.shape
    outs = [jax.ShapeDtypeStruct((t_len, bsz, d_a), F32), jax.ShapeDtypeStruct((t_len, bsz, d_c), F32),
            jax.ShapeDtypeStruct(h0.shape, F32), jax.ShapeDtypeStruct(lb0_t.shape, F32),
            jax.ShapeDtypeStruct(sb0_t.shape, F32)]
    return pl.pallas_call(
        functools.partial(_seq_mix_sample_kernel, d_a=d_a, d_c=d_c),
        out_shape=outs,
        compiler_params=pltpu.CompilerParams(vmem_limit_bytes=VMEM_LIMIT),
        name="seq_mix_sample",
    )(lru_t, sc_t, h0, lb0_t, sb0_t, *wts)


def _compress_kernel(kvc_ref, w_ref, out_ref):
    half = out_ref.shape[0] // 2
    x = kvc_ref[...].reshape(half, 2 * CMP_BLOCK, out_ref.shape[1])
    w = w_ref[...][None]
    out_ref[0:half, :] = jnp.sum(x[:, 0:CMP_BLOCK, :] * w, axis=1)
    out_ref[half:2 * half, :] = jnp.sum(x[:, CMP_BLOCK:2 * CMP_BLOCK, :] * w, axis=1)


def _compress_prompt(kvc, w_cmp, bsz, seq):
    kv_w = kvc.shape[1]
    n_cmp = seq // CMP_BLOCK
    return pl.pallas_call(
        _compress_kernel,
        grid=(bsz,),
        in_specs=[pl.BlockSpec((seq, kv_w), lambda b: (b, 0)), pl.BlockSpec(w_cmp.shape, lambda b: (0, 0))],
        out_specs=pl.BlockSpec((n_cmp, kv_w), lambda b: (b, 0)),
        out_shape=jax.ShapeDtypeStruct((bsz * n_cmp, kv_w), F32),
        compiler_params=_cparams("parallel"),
        name="nsa_compress",
    )(kvc, w_cmp)


def _pad_heads(q, lhs_ref, rows):
    lane = lax.broadcasted_iota(jnp.int32, (rows, LANES), 1)
    for h in range(N_HEADS):
        k = h // GROUP
        slab = q[:, (h // 2) * LANES:(h // 2 + 1) * LANES]
        if h % 2 != k:
            slab = pltpu.roll(slab, HEAD_DIM, axis=1)
        keep = (lane >= k * HEAD_DIM) & (lane < (k + 1) * HEAD_DIM)
        lhs_ref[h * rows:(h + 1) * rows, 0:LANES] = jnp.where(keep, slab, 0.0).astype(MXU_DTYPE)


def _gated_output(gate, o_c, o_s, o_w, rows):
    lane = lax.broadcasted_iota(jnp.int32, (rows, LANES), 1)
    slabs = []
    for m in range(N_HEADS // 2):
        parts = []
        for h in (2 * m, 2 * m + 1):
            k = h // GROUP
            sl = slice(h * rows, (h + 1) * rows)
            r = (gate[:, 3 * h:3 * h + 1] * o_c[sl] + gate[:, 3 * h + 1:3 * h + 2] * o_s[sl]
                 + gate[:, 3 * h + 2:3 * h + 3] * o_w[sl])
            if h % 2 != k:
                r = pltpu.roll(r, HEAD_DIM, axis=1)
            parts.append(r)
        slabs.append(jnp.where(lane < HEAD_DIM, parts[0], parts[1]))
    return slabs


def _compressed_branch(qpad, kcv, pos_q, rows, n_sel):
    n_cmp = kcv.shape[0]
    half = n_cmp // 2
    kc = kcv[:, 0:LANES].astype(MXU_DTYPE)
    vc = kcv[:, LANES:2 * LANES].astype(MXU_DTYPE)
    s_c = lax.dot_general(qpad, kc, _NT, preferred_element_type=F32)
    col = lax.broadcasted_iota(jnp.int32, (rows, n_cmp), 1)
    blk = jnp.where(col < half, 2 * col, 2 * (col - half) + 1)
    m_c = (blk + 1) * CMP_BLOCK - 1 <= pos_q
    ps = []
    imp = [jnp.zeros((rows, n_cmp), F32) for _ in range(N_KV)]
    for h in range(N_HEADS):
        p = _masked_softmax(s_c[h * rows:(h + 1) * rows], m_c)
        imp[h // GROUP] = imp[h // GROUP] + p
        ps.append(p.astype(MXU_DTYPE))
    o_c = jnp.dot(jnp.concatenate(ps, axis=0), vc, preferred_element_type=F32)
    sels = []
    for k in range(N_KV):
        imp_s = imp[k][:, 0:half] + imp[k][:, half:n_cmp]
        if half < LANES:
            imp_s = jnp.concatenate([imp_s, jnp.zeros((rows, LANES - half), F32)], axis=1)
        sels.append(_block_importance_mask(imp_s, pos_q, n_sel))
    return o_c, sels


def _store_selection(lhs_ref, sels, rows):
    for k in range(N_KV):
        neg = jnp.where(sels[k] > 0.0, 0.0, NEG_INF).astype(MXU_DTYPE)
        for g in range(GROUP):
            h = k * GROUP + g
            lhs_ref[h * rows:(h + 1) * rows, LANES:2 * LANES] = neg


def _block_onehot(n_rows):
    row = lax.broadcasted_iota(jnp.int32, (n_rows, LANES), 0)
    lane = lax.broadcasted_iota(jnp.int32, (n_rows, LANES), 1)
    return jnp.where(row // SEL_BLOCK == lane, 1.0, 0.0).astype(MXU_DTYPE)


def _nsa_prompt_kernel(q_ref, gate_ref, kcv_ref, kvs_ref, kvw_ref, out_ref, kaug, lhs, m_sc, l_sc, acc_sc, *, seq):
    qb = pl.program_id(1)
    rows = Q_BLOCK
    tk = KEY_TILE
    s0 = qb * rows
    n_sel = seq // SEL_BLOCK

    @pl.when(qb == 0)
    def _():
        kaug[:, 0:LANES] = kvs_ref[:, 0:LANES]
        kaug[:, LANES:2 * LANES] = _block_onehot(seq)

    _pad_heads(q_ref[...], lhs, rows)
    qpad = lhs[:, 0:LANES]
    pos_q = s0 + lax.broadcasted_iota(jnp.int32, (rows, 1), 0)

    o_c, sels = _compressed_branch(qpad, kcv_ref[...], pos_q, rows, n_sel)
    _store_selection(lhs, sels, rows)

    m_sc[...] = jnp.full(m_sc.shape, NEG_INF, F32)
    l_sc[...] = jnp.zeros(l_sc.shape, F32)
    acc_sc[...] = jnp.zeros(acc_sc.shape, F32)
    lhs_v = lhs[...]

    def tile(kt, causal):
        k0 = pl.multiple_of(kt * tk, tk)
        s = lax.dot_general(lhs_v, kaug[pl.ds(k0, tk), :], _NT, preferred_element_type=F32)
        v = kvs_ref[pl.ds(k0, tk), LANES:2 * LANES]
        if causal:
            cm = k0 + lax.broadcasted_iota(jnp.int32, (rows, tk), 1) <= pos_q
        ps = []
        for h in range(N_HEADS):
            sl = slice(h * rows, (h + 1) * rows)
            sh = s[sl]
            if causal:
                sh = jnp.where(cm, sh, NEG_INF)
            m_old = m_sc[sl]
            m_new = jnp.maximum(m_old, jnp.max(sh, axis=-1, keepdims=True))
            alpha = jnp.exp(m_old - m_new)
            p = jnp.exp(sh - m_new)
            l_sc[sl] = alpha * l_sc[sl] + jnp.sum(p, axis=-1, keepdims=True)
            m_sc[sl] = m_new
            acc_sc[sl] = alpha * acc_sc[sl]
            ps.append(p.astype(MXU_DTYPE))
        acc_sc[...] += jnp.dot(jnp.concatenate(ps, axis=0), v, preferred_element_type=F32)

    n_full = s0 // tk

    def loop_body(kt, carry):
        tile(kt, False)
        return carry

    lax.fori_loop(0, n_full, loop_body, 0)
    tile(n_full, True)

    o_s = acc_sc[...] / jnp.maximum(l_sc[...], TINY)

    band = WINDOW + rows
    start = pl.multiple_of(jnp.maximum(s0 - WINDOW, 0), rows)
    kw = kvw_ref[pl.ds(start, band), 0:LANES]
    vw = kvw_ref[pl.ds(start, band), LANES:2 * LANES]
    s_w = lax.dot_general(qpad, kw, _NT, preferred_element_type=F32)
    dlt = pos_q - (start + lax.broadcasted_iota(jnp.int32, (rows, band), 1))
    m_w = (dlt >= 0) & (dlt <= WINDOW)
    pw = [_masked_softmax(s_w[h * rows:(h + 1) * rows], m_w).astype(MXU_DTYPE) for h in range(N_HEADS)]
    o_w = jnp.dot(jnp.concatenate(pw, axis=0), vw, preferred_element_type=F32)

    for m, slab in enumerate(_gated_output(gate_ref[...], o_c, o_s, o_w, rows)):
        out_ref[:, m * LANES:(m + 1) * LANES] = slab


def _nsa_prompt(q, gate, kcv, kvs_b, kvw_b, bsz, seq):
    n, d_b = bsz * seq, q.shape[1]
    kv_w = kvs_b.shape[1]
    nq = seq // Q_BLOCK
    n_cmp = seq // CMP_BLOCK
    rows = N_HEADS * Q_BLOCK
    return pl.pallas_call(
        functools.partial(_nsa_prompt_kernel, seq=seq),
        grid=(bsz, nq),
        in_specs=[pl.BlockSpec((Q_BLOCK, d_b), lambda b, i: (b * nq + i, 0)),
                  pl.BlockSpec((Q_BLOCK, LANES), lambda b, i: (b * nq + i, 0)),
                  pl.BlockSpec((n_cmp, kv_w), lambda b, i: (b, 0)),
                  pl.BlockSpec((seq, kv_w), lambda b, i: (b, 0)),
                  pl.BlockSpec((seq, kv_w), lambda b, i: (b, 0))],
        out_specs=pl.BlockSpec((Q_BLOCK, d_b), lambda b, i: (b * nq + i, 0)),
        out_shape=jax.ShapeDtypeStruct((n, d_b), F32),
        scratch_shapes=[pltpu.VMEM((seq, 2 * LANES), MXU_DTYPE), pltpu.VMEM((rows, 2 * LANES), MXU_DTYPE),
                        pltpu.VMEM((rows, 1), F32), pltpu.VMEM((rows, 1), F32), pltpu.VMEM((rows, LANES), F32)],
        compiler_params=_cparams("arbitrary", "arbitrary"),
        name="nsa_prompt",
    )(q, gate, kcv, kvs_b, kvw_b)


def _nsa_sample_kernel(pt_ref, q_ref, gate_ref, ksn_ref, kwn_ref, wcmp_ref, win_ref, *rest,
                       n_pages, page, past, t_len):
    cmp_pages = rest[:n_pages]
    sel_pages = rest[n_pages:2 * n_pages]
    out_ref, nwin_ref, kaug, vsel, kcv, lhs = rest[2 * n_pages:]
    rows = 8
    n_cmp = past // CMP_BLOCK
    n_sel = pl.cdiv(past + t_len, SEL_BLOCK)
    per_page = page // CMP_BLOCK

    @pl.when(pl.program_id(0) == 0)
    def _():
        kaug[:, LANES:2 * LANES] = _block_onehot(past)

    q = jnp.concatenate([q_ref[0], jnp.zeros((rows - t_len, q_ref.shape[2]), F32)], axis=0)
    _pad_heads(q, lhs, rows)
    qpad = lhs[:, 0:LANES]
    tq = lax.broadcasted_iota(jnp.int32, (rows, 1), 0)
    pos_q = past + tq

    wrep = jnp.concatenate([wcmp_ref[...]] * per_page, axis=0)
    for p in range(n_pages):
        x = cmp_pages[p][...] * wrep
        for j in range(per_page):
            blk = p * per_page + j
            dst = (blk % 2) * (n_cmp // 2) + blk // 2
            kcv[dst:dst + 1, :] = jnp.sum(x[j * CMP_BLOCK:(j + 1) * CMP_BLOCK], axis=0, keepdims=True)
        kaug[p * page:(p + 1) * page, 0:LANES] = sel_pages[p][:, 0:LANES].astype(MXU_DTYPE)
        vsel[p * page:(p + 1) * page, :] = sel_pages[p][:, LANES:2 * LANES].astype(MXU_DTYPE)

    o_c, sels = _compressed_branch(qpad, kcv[...], pos_q, rows, n_sel)
    _store_selection(lhs, sels, rows)

    def new_rows(ref):
        kv = jnp.concatenate([ref[0], jnp.zeros((LANES - t_len, ref.shape[2]), F32)], axis=0)
        return kv[:, 0:LANES].astype(MXU_DTYPE), kv[:, LANES:2 * LANES].astype(MXU_DTYPE)

    tk_new = lax.broadcasted_iota(jnp.int32, (rows, LANES), 1)
    m_new = (tk_new < t_len) & (tk_new <= tq)

    def joint_attention(s_past, mask_past, v_past, s_new, v_new):
        outs = []
        p_past, p_new = [], []
        for h in range(N_HEADS):
            sl = slice(h * rows, (h + 1) * rows)
            sp = s_past[sl] if mask_past is None else jnp.where(mask_past, s_past[sl], NEG_INF)
            sn = jnp.where(m_new, s_new[sl], NEG_INF)
            m = jnp.maximum(jnp.max(sp, axis=-1, keepdims=True), jnp.max(sn, axis=-1, keepdims=True))
            pp = jnp.exp(sp - m) if mask_past is None else jnp.where(mask_past, jnp.exp(sp - m), 0.0)
            pn = jnp.where(m_new, jnp.exp(sn - m), 0.0)
            den = jnp.maximum(jnp.sum(pp, axis=-1, keepdims=True) + jnp.sum(pn, axis=-1, keepdims=True), TINY)
            p_past.append((pp / den).astype(MXU_DTYPE))
            p_new.append((pn / den).astype(MXU_DTYPE))
        return (jnp.dot(jnp.concatenate(p_past, axis=0), v_past, preferred_element_type=F32)
                + jnp.dot(jnp.concatenate(p_new, axis=0), v_new, preferred_element_type=F32))

    kn, vn = new_rows(ksn_ref)
    s_past = lax.dot_general(lhs[...], kaug[...], _NT, preferred_element_type=F32)
    s_new = lax.dot_general(qpad, kn, _NT, preferred_element_type=F32)
    o_s = joint_attention(s_past, None, vsel[...], s_new, vn)

    n_buf = win_ref.shape[0]
    kwn, vwn = new_rows(kwn_ref)
    kwb = win_ref[:, 0:LANES].astype(MXU_DTYPE)
    vwb = win_ref[:, LANES:2 * LANES].astype(MXU_DTYPE)
    s_wb = lax.dot_general(qpad, kwb, _NT, preferred_element_type=F32)
    s_wn = lax.dot_general(qpad, kwn, _NT, preferred_element_type=F32)
    pos_w = past - n_buf + lax.broadcasted_iota(jnp.int32, (rows, n_buf), 1)
    dlt = pos_q - pos_w
    m_wb = (dlt >= 0) & (dlt <= WINDOW) & (pos_w >= 0)
    o_w = joint_attention(s_wb, m_wb, vwb, s_wn, vwn)

    for m, slab in enumerate(_gated_output(
            jnp.concatenate([gate_ref[0], jnp.zeros((rows - t_len, LANES), F32)], axis=0), o_c, o_s, o_w, rows)):
        out_ref[0, :, m * LANES:(m + 1) * LANES] = slab[0:t_len]

    nwin_ref[0:n_buf - t_len, :] = win_ref[t_len:n_buf, :]
    nwin_ref[n_buf - t_len:n_buf, :] = kwn_ref[0]


def _nsa_sample(page_table, q, gate, kvs_new, kvw_new, w_cmp, pool_c, pool_s, win, layer):
    dbs, t_len, d_b = q.shape
    n_pages = page_table.shape[1]
    page, kv_w = pool_c.shape[2], pool_c.shape[3]
    n_buf = win.shape[2]
    past = n_pages * page
    tok = lambda w: pl.BlockSpec((1, t_len, w), lambda b, pt: (b, 0, 0))

    def page_spec(p):
        return pl.BlockSpec((None, None, page, kv_w), lambda b, pt, p=p: (layer, pt[b * n_pages + p], 0, 0))

    grid_spec = pltpu.PrefetchScalarGridSpec(
        num_scalar_prefetch=1,
        grid=(dbs,),
        in_specs=[tok(d_b), tok(LANES), tok(kv_w), tok(kv_w),
                  pl.BlockSpec(w_cmp.shape, lambda b, pt: (0, 0)),
                  pl.BlockSpec((None, None, n_buf, kv_w), lambda b, pt: (layer, b, 0, 0))]
                 + [page_spec(p) for p in range(n_pages)] * 2,
        out_specs=[tok(d_b), pl.BlockSpec((None, n_buf, kv_w), lambda b, pt: (b, 0, 0))],
        scratch_shapes=[pltpu.VMEM((past, 2 * LANES), MXU_DTYPE), pltpu.VMEM((past, LANES), MXU_DTYPE),
                        pltpu.VMEM((past // CMP_BLOCK, kv_w), F32), pltpu.VMEM((N_HEADS * 8, 2 * LANES), MXU_DTYPE)],
    )
    return pl.pallas_call(
        functools.partial(_nsa_sample_kernel, n_pages=n_pages, page=page, past=past, t_len=t_len),
        grid_spec=grid_spec,
        out_shape=[jax.ShapeDtypeStruct((dbs, t_len, d_b), F32), jax.ShapeDtypeStruct((dbs, n_buf, kv_w), F32)],
        compiler_params=_cparams("arbitrary"),
        name="nsa_sample",
    )(page_table.reshape(-1), q, gate, kvs_new, kvw_new, w_cmp, win, *([pool_c] * n_pages), *([pool_s] * n_pages))


def _out_proj_router_kernel(x_ref, oa_ref, ob_ref, oc_ref, oas_ref, obs_ref, ocs_ref, wo_ref, g2_ref, wr_ref, br_ref,
                            xn_ref, h2_ref, ri_ref, rw_ref, cnt_ref, run_ref, *, n_ptiles):
    tm = x_ref.shape[0]

    @pl.when(pl.program_id(0) == 0)
    def _():
        run_ref[...] = jnp.zeros(run_ref.shape, F32)

    is_sample = pl.program_id(0) >= n_ptiles
    mix = jnp.concatenate([jnp.where(is_sample, oas_ref[...], oa_ref[...]),
                           jnp.where(is_sample, obs_ref[...], ob_ref[...]),
                           jnp.where(is_sample, ocs_ref[...], oc_ref[...])], axis=1).astype(wo_ref.dtype)
    xn = x_ref[...] + jnp.dot(mix, wo_ref[...], preferred_element_type=F32, precision=_precision_for(wo_ref.dtype))
    xn_ref[...] = xn
    h2 = _rmsnorm(xn, g2_ref[...])
    h2_ref[...] = h2
    logits = jnp.dot(h2, wr_ref[...], preferred_element_type=F32, precision=lax.Precision.HIGHEST) + br_ref[...]

    lane = lax.broadcasted_iota(jnp.int32, (tm, LANES), 1)
    lanef = lane.astype(F32)

    def softmax_over(mask):
        m = jnp.max(jnp.where(mask, logits, -jnp.inf), axis=-1, keepdims=True)
        e = jnp.where(mask, jnp.exp(logits - m), 0.0)
        return e / jnp.sum(e, axis=-1, keepdims=True)

    def first_max(p, mask):
        pm = jnp.max(jnp.where(mask, p, -1.0), axis=-1, keepdims=True)
        idx = jnp.min(jnp.where(mask & (p == pm), lanef, 1e9), axis=-1, keepdims=True)
        return pm, idx

    is_g = lane < N_GROUPS
    g_wt, g_sel = first_max(softmax_over(is_g), is_g)
    lo = N_GROUPS + EXP_PER_GROUP * g_sel
    in_e = (lanef >= lo) & (lanef < lo + EXP_PER_GROUP)
    p_e = softmax_over(in_e)
    p0, i0 = first_max(p_e, in_e)
    p1, i1 = first_max(p_e, in_e & (lanef != i0))
    den = p0 + p1
    e0 = i0 - N_GROUPS
    e1 = i1 - N_GROUPS

    hit0 = lanef == e0
    hit1 = lanef == e1
    onehot = jnp.where(hit0 | hit1, 1.0, 0.0)
    r_i = lax.broadcasted_iota(jnp.int32, (tm, tm), 0)
    c_i = lax.broadcasted_iota(jnp.int32, (tm, tm), 1)
    ltri = jnp.where(c_i < r_i, 1.0, 0.0).astype(jnp.bfloat16)
    before = jnp.dot(ltri, onehot.astype(jnp.bfloat16), preferred_element_type=F32) + run_ref[0:1, :]
    r0 = jnp.sum(jnp.where(hit0, before, 0.0), axis=-1, keepdims=True)
    r1 = jnp.sum(jnp.where(hit1, before, 0.0), axis=-1, keepdims=True)
    run_ref[...] = run_ref[...] + jnp.sum(onehot, axis=0, keepdims=True)
    cnt_ref[...] = run_ref[...]

    ri = jnp.where(lane == 0, e0, jnp.where(lane == 1, e1, jnp.where(lane == 2, r0, jnp.where(lane == 3, r1, 0.0))))
    ri_ref[...] = ri.astype(jnp.int32)
    rw_ref[...] = jnp.where(lane == 0, g_wt * p0 / den, jnp.where(lane == 1, g_wt * p1 / den, 0.0))


def _out_proj_router(x, mix_p, mix_s, w_out, g2, w_route, b_route):
    n, d = x.shape
    tm = TOKEN_TILE
    n_ptiles = mix_p[0].shape[0] // tm
    assert all(a.shape[0] == tm for a in mix_s) and n == (n_ptiles + 1) * tm
    row = lambda w: pl.BlockSpec((tm, w), lambda i: (i, 0))
    prow = lambda a: pl.BlockSpec((tm, a.shape[1]), lambda i: (jnp.minimum(i, n_ptiles - 1), 0))
    full = lambda a: pl.BlockSpec(a.shape, lambda i: (0,) * a.ndim)
    return pl.pallas_call(
        functools.partial(_out_proj_router_kernel, n_ptiles=n_ptiles),
        grid=(n // tm,),
        in_specs=[row(d)] + [prow(a) for a in mix_p] + [full(a) for a in mix_s]
                 + [full(w_out), full(g2), full(w_route), full(b_route)],
        out_specs=[row(d), row(d), row(LANES), row(LANES), pl.BlockSpec((8, LANES), lambda i: (0, 0))],
        out_shape=[jax.ShapeDtypeStruct((n, d), F32), jax.ShapeDtypeStruct((n, d), F32),
                   jax.ShapeDtypeStruct((n, LANES), jnp.int32), jax.ShapeDtypeStruct((n, LANES), F32),
                   jax.ShapeDtypeStruct((8, LANES), F32)],
        scratch_shapes=[pltpu.VMEM((8, LANES), F32)],
        compiler_params=_cparams("arbitrary"),
        name="out_proj_router",
    )(x, *mix_p, *mix_s, w_out, g2, w_route, b_route)


def _row_copy(src_hbm, row, dst, slot, r, sem):
    return pltpu.make_async_copy(src_hbm.at[pl.ds(row, 1)], dst.at[slot, pl.ds(r, 1)], sem.at[slot])


def _expert_kernel(te_ref, nv_ref, src_ref, h_hbm, wg_ref, wu_ref, wd_ref, ys_ref, xbuf, sem):
    t = pl.program_id(0)
    nv = nv_ref[0]
    te = xbuf.shape[1]

    def issue(tile, slot):
        def body(r, carry):
            _row_copy(h_hbm, src_ref[tile * te + r], xbuf, slot, r, sem).start()
            return carry
        lax.fori_loop(0, te, body, 0, unroll=8)

    def wait_all(slot):
        def body(r, carry):
            _row_copy(h_hbm, 0, xbuf, slot, r, sem).wait()
            return carry
        lax.fori_loop(0, te, body, 0, unroll=8)

    @pl.when(t == 0)
    def _():
        issue(0, 0)

    @pl.when(t + 1 < nv)
    def _():
        issue(t + 1, (t + 1) % 2)

    @pl.when(t < nv)
    def _():
        slot = t % 2
        wait_all(slot)
        x = xbuf[slot].astype(MXU_DTYPE)
        hg = jnp.dot(x, wg_ref[...].astype(MXU_DTYPE), preferred_element_type=F32)
        hu = jnp.dot(x, wu_ref[...].astype(MXU_DTYPE), preferred_element_type=F32)
        act = (jax.nn.silu(hg) * hu).astype(MXU_DTYPE)
        ys_ref[...] = jnp.dot(act, wd_ref[...].astype(MXU_DTYPE), preferred_element_type=F32)

    @pl.when(t >= nv)
    def _():
        ys_ref[...] = jnp.zeros(ys_ref.shape, F32)


def _experts(tile_expert, n_valid, src_tok, h2, w_gate, w_up, w_down, layer):
    n_tiles = tile_expert.shape[0]
    te = EXPERT_TILE
    d = h2.shape[1]
    d_e = w_gate.shape[3]
    wspec = lambda r, c: pl.BlockSpec((None, None, r, c), lambda t, te_r, nv_r, src_r: (layer, te_r[t], 0, 0))
    grid_spec = pltpu.PrefetchScalarGridSpec(
        num_scalar_prefetch=3,
        grid=(n_tiles,),
        in_specs=[pl.BlockSpec(memory_space=pl.ANY), wspec(d, d_e), wspec(d, d_e), wspec(d_e, d)],
        out_specs=pl.BlockSpec((te, d), lambda t, te_r, nv_r, src_r: (t, 0)),
        scratch_shapes=[pltpu.VMEM((2, te, d), F32), pltpu.SemaphoreType.DMA((2,))],
    )
    return pl.pallas_call(
        _expert_kernel,
        grid_spec=grid_spec,
        out_shape=jax.ShapeDtypeStruct((n_tiles * te, d), F32),
        compiler_params=_cparams("arbitrary"),
        name="moe_experts",
    )(tile_expert, n_valid, src_tok, h2, w_gate, w_up, w_down)


def _combine_kernel(d0_ref, d1_ref, x_ref, rw_ref, gf_ref, ys_hbm, *rest, final):
    if final:
        xo_ref, y_ref, buf, sem = rest
    else:
        xo_ref, buf, sem = rest
    t = pl.program_id(0)
    nt = pl.num_programs(0)
    tm = x_ref.shape[0]

    def issue(tile, slot):
        def body(r, carry):
            _row_copy(ys_hbm, d0_ref[tile * tm + r], buf, slot, r, sem).start()
            _row_copy(ys_hbm, d1_ref[tile * tm + r], buf, slot, tm + r, sem).start()
            return carry
        lax.fori_loop(0, tm, body, 0, unroll=8)

    def wait_all(slot):
        def body(r, carry):
            _row_copy(ys_hbm, 0, buf, slot, r, sem).wait()
            return carry
        lax.fori_loop(0, 2 * tm, body, 0, unroll=8)

    @pl.when(t == 0)
    def _():
        issue(0, 0)

    @pl.when(t + 1 < nt)
    def _():
        issue(t + 1, (t + 1) % 2)

    slot = t % 2
    wait_all(slot)
    rw = rw_ref[...]
    xo = x_ref[...] + rw[:, 0:1] * buf[slot, 0:tm, :] + rw[:, 1:2] * buf[slot, tm:2 * tm, :]
    xo_ref[...] = xo
    if final:
        y_ref[...] = _rmsnorm(xo, gf_ref[...])


def _combine(d0, d1, x, rw, gf, ys, final):
    n, d = x.shape
    tm = COMBINE_TILE
    row = lambda w: pl.BlockSpec((tm, w), lambda t, a, b: (t, 0))
    n_out = 2 if final else 1
    grid_spec = pltpu.PrefetchScalarGridSpec(
        num_scalar_prefetch=2,
        grid=(n // tm,),
        in_specs=[row(d), row(LANES), pl.BlockSpec((1, d), lambda t, a, b: (0, 0)), pl.BlockSpec(memory_space=pl.ANY)],
        out_specs=[row(d)] * n_out,
        scratch_shapes=[pltpu.VMEM((2, 2 * tm, d), F32), pltpu.SemaphoreType.DMA((2,))],
    )
    return pl.pallas_call(
        functools.partial(_combine_kernel, final=final),
        grid_spec=grid_spec,
        out_shape=[jax.ShapeDtypeStruct((n, d), F32)] * n_out,
        compiler_params=_cparams("arbitrary"),
        name="moe_combine_final" if final else "moe_combine",
    )(d0, d1, x, rw, gf, ys)


def _route_plan(ri, cnt, n_tiles):
    te = EXPERT_TILE
    n = ri.shape[0]
    counts = cnt[0, :N_EXPERTS].astype(jnp.int32)
    padded = ((counts + te - 1) // te) * te
    ends = jnp.cumsum(padded)
    offs = ends - padded
    d0 = offs[ri[:, 0]] + ri[:, 2]
    d1 = offs[ri[:, 1]] + ri[:, 3]
    tok = jnp.arange(n, dtype=jnp.int32)
    src = jnp.zeros((n_tiles * te,), jnp.int32).at[d0].set(tok).at[d1].set(tok)
    tile_expert = jnp.minimum(jnp.searchsorted(ends, jnp.arange(n_tiles, dtype=jnp.int32) * te, side="right"),
                              N_EXPERTS - 1).astype(jnp.int32)
    n_valid = (ends[-1:] // te).astype(jnp.int32)
    return d0, d1, src, tile_expert, n_valid


def _rope_tables(pos):
    half = HEAD_DIM // 2
    inv = ROPE_THETA ** (-jnp.arange(half, dtype=F32) / half)
    ang = pos.astype(F32)[:, None] * inv[None, :]
    cos = jnp.cos(ang)
    sin = jnp.sin(ang)
    cos_t = jnp.concatenate([cos, cos] * (LANES // HEAD_DIM), axis=1)
    sin_t = jnp.concatenate([-sin, sin] * (LANES // HEAD_DIM), axis=1)
    return cos_t, sin_t


def _block_diag(w):
    nb, bw, _ = w.shape
    out = jnp.zeros((nb * bw, nb * bw), w.dtype)
    for i in range(nb):
        out = out.at[i * bw:(i + 1) * bw, i * bw:(i + 1) * bw].set(w[i])
    return out


def kernel(x_prompt, x_sample, cache_kv_cmp, cache_kv_sel, cache_kv_win, state_lru_h, state_lru_conv, state_sconv, page_table, norm1_g, w_in, lru_conv_w, lru_conv_b, lru_wa, lru_ba, lru_wx, lru_bx, lru_lambda, nsa_cmp_wk, nsa_cmp_wv, sc_conv_w, w_out, norm2_g, router_group_w, router_group_b, router_exp_w, router_exp_b, exp_w_gate, exp_w_up, exp_w_down, norm_f_g):
    bsz, seq, d = x_prompt.shape
    dbs, t_len, _ = x_sample.shape
    depth = w_in.shape[0]
    d_a = lru_conv_w.shape[2]
    d_c = sc_conv_w.shape[2]
    kv_w = 2 * N_KV * HEAD_DIM
    d_b = N_HEADS * HEAD_DIM
    n_gate = 3 * N_HEADS
    n_p = bsz * seq
    n_s = dbs * t_len
    n = n_p + n_s
    assert n_s == TOKEN_TILE and n_p % TOKEN_TILE == 0 and seq % TOKEN_TILE == 0
    page = cache_kv_cmp.shape[2]
    past = page_table.shape[1] * page
    n_buf = cache_kv_win.shape[2]

    pos = jnp.concatenate([jnp.arange(seq, dtype=jnp.int32),
                           jnp.tile(past + jnp.arange(t_len, dtype=jnp.int32), dbs)])
    cos_t, sin_t = _rope_tables(pos)

    pool_c = cache_kv_cmp.reshape(depth, -1, page, kv_w)
    pool_s = cache_kv_sel.reshape(depth, -1, page, kv_w)
    win = cache_kv_win.reshape(depth, dbs, n_buf, kv_w)

    g_off = 2 * d_a + d_b + 3 * kv_w
    x = jnp.concatenate([x_prompt.reshape(n_p, d), x_sample.reshape(n_s, d)], axis=0)
    h0_p = jnp.zeros((bsz, 1, d_a), F32)
    lb0_p = jnp.zeros((bsz, lru_conv_w.shape[1] - 1, d_a), F32)
    sb0_p = jnp.zeros((bsz, sc_conv_w.shape[1] - 1, d_c), F32)

    n_tiles = (2 * n) // EXPERT_TILE + N_EXPERTS
    proj_dtype = lambda l: F32 if l < depth - 1 else MXU_DTYPE
    states_p, states_s = [], []
    y = None
    for l in range(depth):
        w_l = w_in[l]
        w_r = jnp.concatenate([w_l[:, :g_off], w_l[:, g_off + n_gate:], w_l[:, g_off:g_off + n_gate],
                               jnp.zeros((d, LANES - n_gate), F32)], axis=1).astype(proj_dtype(l))
        lru_in, sc_in, gate, q, kvc, kvs, kvw, kvs_b, kvw_b = _in_proj(
            x, norm1_g[l][None], w_r, cos_t, sin_t, n_p, seq, d_a=d_a, d_b=d_b, kv_w=kv_w, d_c=d_c)

        seq_w = (lru_conv_w[l], lru_conv_b[l][None], _block_diag(lru_wa[l]).astype(MXU_DTYPE), lru_ba[l][None],
                 _block_diag(lru_wx[l]).astype(MXU_DTYPE), lru_bx[l][None], lru_lambda[l][None], sc_conv_w[l])
        out_a, out_c, hn_p, lbn_p, sbn_p = _seq_mix_prompt(lru_in, sc_in, h0_p, lb0_p, sb0_p, seq_w, bsz, seq,
                                                           d_a=d_a, d_c=d_c)
        tmaj = lambda a: jnp.swapaxes(a.reshape(dbs, t_len, -1), 0, 1)
        oa_s, oc_s, hn_s, lbn_s, sbn_s = _seq_mix_sample(
            tmaj(lru_in[n_p:]), tmaj(sc_in[n_p:]), state_lru_h[l], jnp.swapaxes(state_lru_conv[l], 0, 1),
            jnp.swapaxes(state_sconv[l], 0, 1), seq_w, d_a=d_a, d_c=d_c)
        oa_s = jnp.swapaxes(oa_s, 0, 1).reshape(n_s, d_a)
        oc_s = jnp.swapaxes(oc_s, 0, 1).reshape(n_s, d_c)

        w_cmp = jnp.concatenate([jnp.broadcast_to(nsa_cmp_wk[l][:, None], (CMP_BLOCK, kv_w // 2)),
                                 jnp.broadcast_to(nsa_cmp_wv[l][:, None], (CMP_BLOCK, kv_w // 2))], axis=1)
        kcv = _compress_prompt(kvc, w_cmp, bsz, seq)
        out_b = _nsa_prompt(q, gate, kcv, kvs_b, kvw_b, bsz, seq)
        s3 = lambda a: a[n_p:].reshape(dbs, t_len, -1)
        ob_s, nwin_s = _nsa_sample(page_table, s3(q), s3(gate), s3(kvs), s3(kvw), w_cmp, pool_c, pool_s, win, l)
        ob_s = ob_s.reshape(n_s, d_b)

        w_route = jnp.concatenate([router_group_w[l], router_exp_w[l],
                                   jnp.zeros((d, LANES - N_GROUPS - N_EXPERTS), F32)], axis=1)
        b_route = jnp.concatenate([router_group_b[l], router_exp_b[l],
                                   jnp.zeros((LANES - N_GROUPS - N_EXPERTS,), F32)])[None]
        xn, h2, ri, rw, cnt = _out_proj_router(x, (out_a, out_b, out_c), (oa_s, ob_s, oc_s),
                                               w_out[l].astype(proj_dtype(l)), norm2_g[l][None], w_route, b_route)
        d0, d1, src, tile_expert, n_valid = _route_plan(ri, cnt, n_tiles)
        ys = _experts(tile_expert, n_valid, src, h2, exp_w_gate, exp_w_up, exp_w_down, l)
        final = l == depth - 1
        outs = _combine(d0, d1, xn, rw, norm_f_g[None], ys, final)
        x = outs[0]
        if final:
            y = outs[1]

        kv6 = lambda a, lead: a.reshape(lead + (2, N_KV, HEAD_DIM))
        states_p.append((kv6(kvc[:n_p], (bsz, seq)), kv6(kvs[:n_p], (bsz, seq)),
                         kv6(kvw[:n_p].reshape(bsz, seq, kv_w)[:, seq - min(WINDOW, seq):], (bsz, min(WINDOW, seq))),
                         hn_p[:, 0], lbn_p, sbn_p))
        states_s.append((kv6(kvc[n_p:], (dbs, t_len)), kv6(kvs[n_p:], (dbs, t_len)), kv6(nwin_s, (dbs, n_buf)),
                         hn_s, jnp.swapaxes(lbn_s, 0, 1), jnp.swapaxes(sbn_s, 0, 1)))

    stack = lambda sts, i: jnp.stack([s[i] for s in sts])
    res = [y[:n_p].reshape(bsz, seq, d), y[n_p:].reshape(dbs, t_len, d)]
    for i in range(6):
        res += [stack(states_p, i), stack(states_s, i)]
    return tuple(res)
```

```python
import functools

import jax
import jax.numpy as jnp
from jax import lax
from jax.experimental import pallas as pl
from jax.experimental.pallas import tpu as pltpu

F32 = jnp.float32
MXU_DTYPE = jnp.bfloat16

HEAD_DIM = 64
N_HEADS = 8
N_KV = 2
GROUP = N_HEADS // N_KV
CMP_BLOCK = 32
SEL_BLOCK = 64
TOP_N = 16
WINDOW = 512
Q_BLOCK = 128
ROPE_THETA = 10000.0
LRU_C = 8.0
N_GROUPS = 4
EXP_PER_GROUP = 8
N_EXPERTS = N_GROUPS * EXP_PER_GROUP
RMS_EPS = 1e-6
NEG_INF = -1e30
TINY = 1e-30
FORCE = 1e6

LANES = 128
VMEM_LIMIT = 56 * 2 ** 20
TOKEN_TILE = 512
TIME_CHUNK = 256
KEY_TILE = 512
EXPERT_TILE = 256
COMBINE_TILE = 256

_NT = (((1,), (1,)), ((), ()))


def _cparams(*sem):
    return pltpu.CompilerParams(dimension_semantics=sem, vmem_limit_bytes=VMEM_LIMIT)


def _precision_for(w_dtype):
    return lax.Precision.HIGHEST if w_dtype == F32 else None


def _rmsnorm(x, g):
    return x * lax.rsqrt(jnp.mean(x * x, axis=-1, keepdims=True) + RMS_EPS) * g


def _swap_halves(x):
    w = x.shape[-1]
    lane = lax.broadcasted_iota(jnp.int32, x.shape, x.ndim - 1)
    first = (lane % HEAD_DIM) < HEAD_DIM // 2
    return jnp.where(first, pltpu.roll(x, w - HEAD_DIM // 2, axis=1), pltpu.roll(x, HEAD_DIM // 2, axis=1))


def _masked_softmax(s, mask):
    s = jnp.where(mask, s, NEG_INF)
    m = jnp.max(s, axis=-1, keepdims=True)
    p = jnp.where(mask, jnp.exp(s - m), 0.0)
    return p / jnp.maximum(jnp.sum(p, axis=-1, keepdims=True), TINY)


def _top_n_mask(w, n_top):
    idxf = lax.broadcasted_iota(jnp.int32, w.shape, 1).astype(F32)

    def body(_, carry):
        w, sel = carry
        m = jnp.max(w, axis=-1, keepdims=True)
        first = jnp.min(jnp.where(w == m, idxf, 1e9), axis=-1, keepdims=True)
        pick = idxf == first
        return jnp.where(pick, -jnp.inf, w), jnp.where(pick, 1.0, sel)

    _, sel = lax.fori_loop(0, n_top, body, (w, jnp.zeros_like(w)))
    return sel


def _block_importance_mask(imp, pos_q, n_sel):
    blk = lax.broadcasted_iota(jnp.int32, imp.shape, 1)
    cur = pos_q // SEL_BLOCK
    forced = (blk == 0) | (blk == cur) | (blk == cur - 1)
    valid = (blk <= cur) & (blk < n_sel)
    w = jnp.where(forced, FORCE, imp)
    w = jnp.where(valid, w, NEG_INF)
    w = jnp.where(blk < n_sel, w, -jnp.inf)
    return _top_n_mask(w, min(TOP_N, n_sel))


def _in_proj_kernel(x_ref, g_ref, w_ref, cos_ref, sin_ref,
                    lru_ref, sc_ref, gate_ref, q_ref, kvc_ref, kvs_ref, kvw_ref, kvsb_ref, kvwb_ref,
                    *, d_a, d_b, kv_w, d_c):
    hb = _rmsnorm(x_ref[...], g_ref[...]).astype(w_ref.dtype)
    prec = _precision_for(w_ref.dtype)

    def mm(a, b):
        return jnp.dot(hb, w_ref[:, a:b], preferred_element_type=F32, precision=prec)

    cos = cos_ref[...]
    sin = sin_ref[...]
    off = 2 * d_a
    lru_ref[...] = mm(0, off)
    q = mm(off, off + d_b)
    reps = d_b // LANES
    cos_q = jnp.concatenate([cos] * reps, axis=1)
    sin_q = jnp.concatenate([sin] * reps, axis=1)
    q_ref[...] = (q * cos_q + _swap_halves(q) * sin_q) * (HEAD_DIM ** -0.5)
    off += d_b
    for ref, bref in ((kvc_ref, None), (kvs_ref, kvsb_ref), (kvw_ref, kvwb_ref)):
        kv = mm(off, off + kv_w)
        k = kv[:, :kv_w // 2]
        kv = jnp.concatenate([k * cos + _swap_halves(k) * sin, kv[:, kv_w // 2:]], axis=1)
        ref[...] = kv
        if bref is not None:
            bref[...] = kv.astype(MXU_DTYPE)
        off += kv_w
    sc_ref[...] = mm(off, off + 3 * d_c)
    off += 3 * d_c
    gate_ref[...] = jax.nn.sigmoid(mm(off, off + LANES))


def _in_proj(x, g1, w_r, cos_t, sin_t, n_prompt, seq, *, d_a, d_b, kv_w, d_c):
    n, d = x.shape
    tm = TOKEN_TILE
    n_ptiles = n_prompt // tm
    n_stiles = seq // tm

    def tab_map(i):
        return (jnp.where(i < n_ptiles, i % n_stiles, n_stiles), 0)

    row = lambda w: pl.BlockSpec((tm, w), lambda i: (i, 0))
    widths = (2 * d_a, 3 * d_c, LANES, d_b, kv_w, kv_w, kv_w, kv_w, kv_w)
    dtypes = (F32, F32, F32, F32, F32, F32, F32, MXU_DTYPE, MXU_DTYPE)
    return pl.pallas_call(
        functools.partial(_in_proj_kernel, d_a=d_a, d_b=d_b, kv_w=kv_w, d_c=d_c),
        grid=(n // tm,),
        in_specs=[row(d),
                  pl.BlockSpec((1, d), lambda i: (0, 0)),
                  pl.BlockSpec(w_r.shape, lambda i: (0, 0)),
                  pl.BlockSpec((tm, LANES), tab_map),
                  pl.BlockSpec((tm, LANES), tab_map)],
        out_specs=[row(w) for w in widths],
        out_shape=[jax.ShapeDtypeStruct((n, w), dt) for w, dt in zip(widths, dtypes)],
        compiler_params=_cparams("parallel"),
        name="in_proj",
    )(x, g1, w_r, cos_t, sin_t)


def _scan_rows(a, u):
    t = a.shape[0]
    row = lax.broadcasted_iota(jnp.int32, a.shape, 0)
    d = 1
    while d < t:
        keep = row >= d
        a_sh = jnp.where(keep, pltpu.roll(a, d, axis=0), 1.0)
        u_sh = jnp.where(keep, pltpu.roll(u, d, axis=0), 0.0)
        u = a * u_sh + u
        a = a * a_sh
        d *= 2
    return a, u


def _lru_gates(xc, wa_ref, ba_ref, wx_ref, bx_ref, lam_ref):
    xb = xc.astype(MXU_DTYPE)
    r = jax.nn.sigmoid(jnp.dot(xb, wa_ref[...], preferred_element_type=F32) + ba_ref[...])
    i = jax.nn.sigmoid(jnp.dot(xb, wx_ref[...], preferred_element_type=F32) + bx_ref[...])
    log_a = -LRU_C * r * jax.nn.softplus(-lam_ref[...])
    a = jnp.exp(log_a)
    th = jnp.tanh(log_a)
    u = jnp.sqrt(-2.0 * th / (1.0 - th)) * (i * xc)
    return a, u


def _seq_mix_kernel(lru_ref, sc_ref, h0_ref, lb0_ref, sb0_ref, cw_ref, cb_ref, wa_ref, ba_ref, wx_ref, bx_ref,
                    lam_ref, scw_ref, oa_ref, oc_ref, hn_ref, lbn_ref, sbn_ref, xpad, vpad, hcar, *, d_a, d_c):
    c = pl.program_id(1)
    tc = lru_ref.shape[0]
    nlb = lb0_ref.shape[1]
    nsb = sb0_ref.shape[1]

    @pl.when(c == 0)
    def _():
        xpad[0:8, :] = jnp.zeros((8, d_a), F32)
        xpad[8 - nlb:8, :] = lb0_ref[0]
        vpad[0:8, :] = jnp.zeros((8, d_c), F32)
        vpad[8 - nsb:8, :] = sb0_ref[0]
        hcar[...] = h0_ref[0]

    xa = lru_ref[:, 0:d_a]
    ga = lru_ref[:, d_a:2 * d_a]
    xpad[8:8 + tc, :] = xa
    xc = cw_ref[0:1, :] * xpad[8 - nlb:8 - nlb + tc, :]
    for j in range(1, nlb):
        xc = xc + cw_ref[j:j + 1, :] * xpad[8 - nlb + j:8 - nlb + j + tc, :]
    xc = xc + cw_ref[nlb:nlb + 1, :] * xa + cb_ref[...]
    a, u = _lru_gates(xc, wa_ref, ba_ref, wx_ref, bx_ref, lam_ref)
    a_cum, hs = _scan_rows(a, u)
    hs = hs + a_cum * hcar[...]
    hcar[...] = hs[tc - 1:tc, :]
    oa_ref[...] = hs * jax.nn.gelu(ga)

    v = sc_ref[:, 0:d_c] * sc_ref[:, 2 * d_c:3 * d_c]
    vpad[8:8 + tc, :] = v
    uc = scw_ref[0:1, :] * vpad[8 - nsb:8 - nsb + tc, :]
    for j in range(1, nsb):
        uc = uc + scw_ref[j:j + 1, :] * vpad[8 - nsb + j:8 - nsb + j + tc, :]
    uc = uc + scw_ref[nsb:nsb + 1, :] * v
    oc_ref[...] = sc_ref[:, d_c:2 * d_c] * uc

    hn_ref[0] = hs[tc - 1:tc, :]
    lbn_ref[0] = xpad[8 + tc - nlb:8 + tc, :]
    sbn_ref[0] = vpad[8 + tc - nsb:8 + tc, :]
    xpad[0:8, :] = xpad[tc:tc + 8, :]
    vpad[0:8, :] = vpad[tc:tc + 8, :]


def _seq_mix_prompt(lru_in, sc_in, h0, lb0, sb0, wts, bsz, seq, *, d_a, d_c):
    n = bsz * seq
    tc = TIME_CHUNK
    nch = seq // tc
    row = lambda w: pl.BlockSpec((tc, w), lambda b, c: (b * nch + c, 0))
    full = lambda a: pl.BlockSpec(a.shape, lambda b, c: (0,) * a.ndim)
    state = lambda r, w: pl.BlockSpec((1, r, w), lambda b, c: (b, 0, 0))
    nlb, nsb = lb0.shape[1], sb0.shape[1]
    return pl.pallas_call(
        functools.partial(_seq_mix_kernel, d_a=d_a, d_c=d_c),
        grid=(bsz, nch),
        in_specs=[row(2 * d_a), row(3 * d_c), state(1, d_a), state(nlb, d_a), state(nsb, d_c)]
                 + [full(w) for w in wts],
        out_specs=[row(d_a), row(d_c), state(1, d_a), state(nlb, d_a), state(nsb, d_c)],
        out_shape=[jax.ShapeDtypeStruct((n, d_a), F32), jax.ShapeDtypeStruct((n, d_c), F32),
                   jax.ShapeDtypeStruct((bsz, 1, d_a), F32), jax.ShapeDtypeStruct((bsz, nlb, d_a), F32),
                   jax.ShapeDtypeStruct((bsz, nsb, d_c), F32)],
        scratch_shapes=[pltpu.VMEM((tc + 8, d_a), F32), pltpu.VMEM((tc + 8, d_c), F32), pltpu.VMEM((1, d_a), F32)],
        compiler_params=_cparams("arbitrary", "arbitrary"),
        name="seq_mix_prompt",
    )(lru_in, sc_in, h0, lb0, sb0, *wts)


def _seq_mix_sample_kernel(lru_ref, sc_ref, h0_ref, lb0_ref, sb0_ref, cw_ref, cb_ref, wa_ref, ba_ref, wx_ref,
                           bx_ref, lam_ref, scw_ref, oa_ref, oc_ref, hn_ref, lbn_ref, sbn_ref, *, d_a, d_c):
    t_len = lru_ref.shape[0]
    nlb = lb0_ref.shape[0]
    nsb = sb0_ref.shape[0]
    xs = [lb0_ref[j] for j in range(nlb)] + [lru_ref[t][:, 0:d_a] for t in range(t_len)]
    vs = [sb0_ref[j] for j in range(nsb)] + [sc_ref[t][:, 0:d_c] * sc_ref[t][:, 2 * d_c:3 * d_c] for t in range(t_len)]
    h = h0_ref[...]
    for t in range(t_len):
        xc = cw_ref[0:1, :] * xs[t]
        for j in range(1, nlb + 1):
            xc = xc + cw_ref[j:j + 1, :] * xs[t + j]
        xc = xc + cb_ref[...]
        a, u = _lru_gates(xc, wa_ref, ba_ref, wx_ref, bx_ref, lam_ref)
        h = a * h + u
        oa_ref[t] = h * jax.nn.gelu(lru_ref[t][:, d_a:2 * d_a])
        uc = scw_ref[0:1, :] * vs[t]
        for j in range(1, nsb + 1):
            uc = uc + scw_ref[j:j + 1, :] * vs[t + j]
        oc_ref[t] = sc_ref[t][:, d_c:2 * d_c] * uc
    hn_ref[...] = h
    for j in range(nlb):
        lbn_ref[j] = xs[t_len + j]
    for j in range(nsb):
        sbn_ref[j] = vs[t_len + j]


def _seq_mix_sample(lru_t, sc_t, h0, lb0_t, sb0_t, wts, *, d_a, d_c):
    t_len, bsz, _ = lru_t.shape
    outs = [jax.ShapeDtypeStruct((t_len, bsz, d_a), F32), jax.ShapeDtypeStruct((t_len, bsz, d_c), F32),
            jax.ShapeDtypeStruct(h0.shape, F32), jax.ShapeDtypeStruct(lb0_t.shape, F32),
            jax.ShapeDtypeStruct(sb0_t.shape, F32)]
    return pl.pallas_call(
        functools.partial(_seq_mix_sample_kernel, d_a=d_a, d_c=d_c),
        out_shape=outs,
        compiler_params=pltpu.CompilerParams(vmem_limit_bytes=VMEM_LIMIT),
        name="seq_mix_sample",
    )(lru_t, sc_t, h0, lb0_t, sb0_t, *wts)


def _compress_kernel(kvc_ref, w_ref, out_ref):
    half = out_ref.shape[0] // 2
    x = kvc_ref[...].reshape(half, 2 * CMP_BLOCK, out_ref.shape[1])
    w = w_ref[...][None]
    out_ref[0:half, :] = jnp.sum(x[:, 0:CMP_BLOCK, :] * w, axis=1)
    out_ref[half:2 * half, :] = jnp.sum(x[:, CMP_BLOCK:2 * CMP_BLOCK, :] * w, axis=1)


def _compress_prompt(kvc, w_cmp, bsz, seq):
    kv_w = kvc.shape[1]
    n_cmp = seq // CMP_BLOCK
    return pl.pallas_call(
        _compress_kernel,
        grid=(bsz,),
        in_specs=[pl.BlockSpec((seq, kv_w), lambda b: (b, 0)), pl.BlockSpec(w_cmp.shape, lambda b: (0, 0))],
        out_specs=pl.BlockSpec((n_cmp, kv_w), lambda b: (b, 0)),
        out_shape=jax.ShapeDtypeStruct((bsz * n_cmp, kv_w), F32),
        compiler_params=_cparams("parallel"),
        name="nsa_compress",
    )(kvc, w_cmp)


def _pad_heads(q, lhs_ref, rows):
    lane = lax.broadcasted_iota(jnp.int32, (rows, LANES), 1)
    for h in range(N_HEADS):
        k = h // GROUP
        slab = q[:, (h // 2) * LANES:(h // 2 + 1) * LANES]
        if h % 2 != k:
            slab = pltpu.roll(slab, HEAD_DIM, axis=1)
        keep = (lane >= k * HEAD_DIM) & (lane < (k + 1) * HEAD_DIM)
        lhs_ref[h * rows:(h + 1) * rows, 0:LANES] = jnp.where(keep, slab, 0.0).astype(MXU_DTYPE)


def _gated_output(gate, o_c, o_s, o_w, rows):
    lane = lax.broadcasted_iota(jnp.int32, (rows, LANES), 1)
    slabs = []
    for m in range(N_HEADS // 2):
        parts = []
        for h in (2 * m, 2 * m + 1):
            k = h // GROUP
            sl = slice(h * rows, (h + 1) * rows)
            r = (gate[:, 3 * h:3 * h + 1] * o_c[sl] + gate[:, 3 * h + 1:3 * h + 2] * o_s[sl]
                 + gate[:, 3 * h + 2:3 * h + 3] * o_w[sl])
            if h % 2 != k:
                r = pltpu.roll(r, HEAD_DIM, axis=1)
            parts.append(r)
        slabs.append(jnp.where(lane < HEAD_DIM, parts[0], parts[1]))
    return slabs


def _compressed_branch(qpad, kcv, pos_q, rows, n_sel):
    n_cmp = kcv.shape[0]
    half = n_cmp // 2
    kc = kcv[:, 0:LANES].astype(MXU_DTYPE)
    vc = kcv[:, LANES:2 * LANES].astype(MXU_DTYPE)
    s_c = lax.dot_general(qpad, kc, _NT, preferred_element_type=F32)
    col = lax.broadcasted_iota(jnp.int32, (rows, n_cmp), 1)
    blk = jnp.where(col < half, 2 * col, 2 * (col - half) + 1)
    m_c = (blk + 1) * CMP_BLOCK - 1 <= pos_q
    ps = []
    imp = [jnp.zeros((rows, n_cmp), F32) for _ in range(N_KV)]
    for h in range(N_HEADS):
        p = _masked_softmax(s_c[h * rows:(h + 1) * rows], m_c)
        imp[h // GROUP] = imp[h // GROUP] + p
        ps.append(p.astype(MXU_DTYPE))
    o_c = jnp.dot(jnp.concatenate(ps, axis=0), vc, preferred_element_type=F32)
    sels = []
    for k in range(N_KV):
        imp_s = imp[k][:, 0:half] + imp[k][:, half:n_cmp]
        if half < LANES:
            imp_s = jnp.concatenate([imp_s, jnp.zeros((rows, LANES - half), F32)], axis=1)
        sels.append(_block_importance_mask(imp_s, pos_q, n_sel))
    return o_c, sels


def _store_selection(lhs_ref, sels, rows):
    for k in range(N_KV):
        neg = jnp.where(sels[k] > 0.0, 0.0, NEG_INF).astype(MXU_DTYPE)
        for g in range(GROUP):
            h = k * GROUP + g
            lhs_ref[h * rows:(h + 1) * rows, LANES:2 * LANES] = neg


def _block_onehot(n_rows):
    row = lax.broadcasted_iota(jnp.int32, (n_rows, LANES), 0)
    lane = lax.broadcasted_iota(jnp.int32, (n_rows, LANES), 1)
    return jnp.where(row // SEL_BLOCK == lane, 1.0, 0.0).astype(MXU_DTYPE)


def _masked_softmax_keys(s, mask):
    s = jnp.where(mask, s, NEG_INF)
    m = jnp.max(s, axis=0, keepdims=True)
    p = jnp.where(mask, jnp.exp(s - m), 0.0)
    return p / jnp.maximum(jnp.sum(p, axis=0, keepdims=True), TINY)


def _select_blocks_keys(imp, pos_q, n_sel):
    blk = lax.broadcasted_iota(jnp.int32, imp.shape, 0)
    cur = pos_q // SEL_BLOCK
    forced = (blk == 0) | (blk == cur) | (blk == cur - 1)
    w = jnp.where(forced, FORCE, imp)
    w = jnp.where(blk <= cur, w, NEG_INF)
    blkf = blk.astype(F32)

    def body(_, carry):
        w, sel = carry
        m = jnp.max(w, axis=0, keepdims=True)
        first = jnp.min(jnp.where(w == m, blkf, 1e9), axis=0, keepdims=True)
        pick = blkf == first
        return jnp.where(pick, -jnp.inf, w), jnp.where(pick, 1.0, sel)

    _, sel = lax.fori_loop(0, min(TOP_N, n_sel), body, (w, jnp.zeros_like(w)))
    return sel


def _nsa_prompt_kernel(q_ref, gate_ref, kcv_ref, kvs_ref, kvw_ref, out_ref,
                       kaug, vst, vwt, vct, lhs, m_sc, l_sc, acc_sc, *, seq):
    qb = pl.program_id(1)
    rows = Q_BLOCK
    tk = KEY_TILE
    s0 = qb * rows
    n_sel = seq // SEL_BLOCK
    n_cmp = seq // CMP_BLOCK
    cols = N_HEADS * rows

    @pl.when(qb == 0)
    def _():
        kaug[:, 0:LANES] = kvs_ref[:, 0:LANES]
        kaug[:, LANES:2 * LANES] = _block_onehot(seq)
        vct[...] = kcv_ref[:, LANES:2 * LANES].T.astype(MXU_DTYPE)

        def tr_sel(i, c):
            r0 = pl.multiple_of(i * tk, tk)
            vst[i] = kvs_ref[pl.ds(r0, tk), LANES:2 * LANES].astype(F32).T.astype(MXU_DTYPE)
            return c

        def tr_win(i, c):
            r0 = pl.multiple_of(i * rows, rows)
            vwt[i] = kvw_ref[pl.ds(r0, rows), LANES:2 * LANES].astype(F32).T.astype(MXU_DTYPE)
            return c

        lax.fori_loop(0, seq // tk, tr_sel, 0)
        lax.fori_loop(0, seq // rows, tr_win, 0)

    _pad_heads(q_ref[...], lhs, rows)
    qpad = lhs[:, 0:LANES]
    lane = lax.broadcasted_iota(jnp.int32, (1, cols), 1)
    pos_q = s0 + lane % rows

    half = n_cmp // 2
    s_c = lax.dot_general(kcv_ref[:, 0:LANES].astype(MXU_DTYPE), qpad, _NT, preferred_element_type=F32)
    r_c = lax.broadcasted_iota(jnp.int32, (n_cmp, cols), 0)
    blk_c = jnp.where(r_c < half, 2 * r_c, 2 * (r_c - half) + 1)
    p_c = _masked_softmax_keys(s_c, (blk_c + 1) * CMP_BLOCK - 1 <= pos_q)
    o_c = jnp.dot(vct[...], p_c.astype(MXU_DTYPE), preferred_element_type=F32)

    imp = []
    for k in range(N_KV):
        acc = p_c[:, k * GROUP * rows:(k * GROUP + 1) * rows]
        for g in range(1, GROUP):
            acc = acc + p_c[:, (k * GROUP + g) * rows:(k * GROUP + g + 1) * rows]
        imp.append(acc[0:half] + acc[half:n_cmp])
    sel = _select_blocks_keys(jnp.concatenate(imp, axis=1), pos_q[:, 0:N_KV * rows], n_sel)
    for k in range(N_KV):
        sel_k = sel[:, k * rows:(k + 1) * rows]
        if n_sel < LANES:
            sel_k = jnp.concatenate([sel_k, jnp.zeros((LANES - n_sel, rows), F32)], axis=0)
        neg = jnp.where(sel_k.T > 0.0, 0.0, NEG_INF).astype(MXU_DTYPE)
        for g in range(GROUP):
            h = k * GROUP + g
            lhs[h * rows:(h + 1) * rows, LANES:2 * LANES] = neg

    m_sc[...] = jnp.full(m_sc.shape, NEG_INF, F32)
    l_sc[...] = jnp.zeros(l_sc.shape, F32)
    acc_sc[...] = jnp.zeros(acc_sc.shape, F32)

    def tile(kt, causal):
        k0 = pl.multiple_of(kt * tk, tk)
        s = lax.dot_general(kaug[pl.ds(k0, tk), :], lhs[...], _NT, preferred_element_type=F32)
        if causal:
            s = jnp.where(k0 + lax.broadcasted_iota(jnp.int32, (tk, cols), 0) <= pos_q, s, NEG_INF)
        m_old = m_sc[...]
        m_new = jnp.maximum(m_old, jnp.max(s, axis=0, keepdims=True))
        alpha = jnp.exp(m_old - m_new)
        p = jnp.exp(s - m_new)
        l_sc[...] = alpha * l_sc[...] + jnp.sum(p, axis=0, keepdims=True)
        m_sc[...] = m_new
        acc_sc[...] = alpha * acc_sc[...] + jnp.dot(vst[kt], p.astype(MXU_DTYPE), preferred_element_type=F32)

    n_full = s0 // tk

    def loop_body(kt, carry):
        tile(kt, False)
        return carry

    lax.fori_loop(0, n_full, loop_body, 0)
    tile(n_full, True)
    o_s = acc_sc[...] / jnp.maximum(l_sc[...], TINY)

    band = WINDOW + rows
    start = pl.multiple_of(jnp.maximum(s0 - WINDOW, 0), rows)
    s_w = lax.dot_general(kvw_ref[pl.ds(start, band), 0:LANES], qpad, _NT, preferred_element_type=F32)
    dlt = pos_q - (start + lax.broadcasted_iota(jnp.int32, (band, cols), 0))
    p_w = _masked_softmax_keys(s_w, (dlt >= 0) & (dlt <= WINDOW))
    t0 = start // rows
    vw = jnp.concatenate([vwt[t0 + j] for j in range(band // rows)], axis=1)
    o_w = jnp.dot(vw, p_w.astype(MXU_DTYPE), preferred_element_type=F32)

    g_t = gate_ref[...].T
    for m in range(N_HEADS // 2):
        parts = []
        for h in (2 * m, 2 * m + 1):
            k = h // GROUP
            rs = slice(k * HEAD_DIM, (k + 1) * HEAD_DIM)
            ls = slice(h * rows, (h + 1) * rows)
            parts.append(g_t[3 * h:3 * h + 1] * o_c[rs, ls] + g_t[3 * h + 1:3 * h + 2] * o_s[rs, ls]
                         + g_t[3 * h + 2:3 * h + 3] * o_w[rs, ls])
        out_ref[:, m * LANES:(m + 1) * LANES] = jnp.concatenate(parts, axis=0).T


def _nsa_prompt(q, gate, kcv, kvs_b, kvw_b, bsz, seq):
    n, d_b = bsz * seq, q.shape[1]
    kv_w = kvs_b.shape[1]
    nq = seq // Q_BLOCK
    n_cmp = seq // CMP_BLOCK
    rows = N_HEADS * Q_BLOCK
    return pl.pallas_call(
        functools.partial(_nsa_prompt_kernel, seq=seq),
        grid=(bsz, nq),
        in_specs=[pl.BlockSpec((Q_BLOCK, d_b), lambda b, i: (b * nq + i, 0)),
                  pl.BlockSpec((Q_BLOCK, LANES), lambda b, i: (b * nq + i, 0)),
                  pl.BlockSpec((n_cmp, kv_w), lambda b, i: (b, 0)),
                  pl.BlockSpec((seq, kv_w), lambda b, i: (b, 0)),
                  pl.BlockSpec((seq, kv_w), lambda b, i: (b, 0))],
        out_specs=pl.BlockSpec((Q_BLOCK, d_b), lambda b, i: (b * nq + i, 0)),
        out_shape=jax.ShapeDtypeStruct((n, d_b), F32),
        scratch_shapes=[pltpu.VMEM((seq, 2 * LANES), MXU_DTYPE),
                        pltpu.VMEM((seq // KEY_TILE, LANES, KEY_TILE), MXU_DTYPE),
                        pltpu.VMEM((seq // Q_BLOCK, LANES, Q_BLOCK), MXU_DTYPE),
                        pltpu.VMEM((LANES, n_cmp), MXU_DTYPE),
                        pltpu.VMEM((rows, 2 * LANES), MXU_DTYPE),
                        pltpu.VMEM((1, rows), F32), pltpu.VMEM((1, rows), F32), pltpu.VMEM((LANES, rows), F32)],
        compiler_params=_cparams("arbitrary", "arbitrary"),
        name="nsa_prompt",
    )(q, gate, kcv, kvs_b, kvw_b)


def _nsa_sample_kernel(pt_ref, q_ref, gate_ref, ksn_ref, kwn_ref, wcmp_ref, win_ref, *rest,
                       n_pages, page, past, t_len):
    cmp_pages = rest[:n_pages]
    sel_pages = rest[n_pages:2 * n_pages]
    out_ref, nwin_ref, kaug, vsel, kcv, lhs = rest[2 * n_pages:]
    rows = 8
    n_cmp = past // CMP_BLOCK
    n_sel = pl.cdiv(past + t_len, SEL_BLOCK)
    per_page = page // CMP_BLOCK

    @pl.when(pl.program_id(0) == 0)
    def _():
        kaug[:, LANES:2 * LANES] = _block_onehot(past)

    q = jnp.concatenate([q_ref[0], jnp.zeros((rows - t_len, q_ref.shape[2]), F32)], axis=0)
    _pad_heads(q, lhs, rows)
    qpad = lhs[:, 0:LANES]
    tq = lax.broadcasted_iota(jnp.int32, (rows, 1), 0)
    pos_q = past + tq

    wrep = jnp.concatenate([wcmp_ref[...]] * per_page, axis=0)
    for p in range(n_pages):
        x = cmp_pages[p][...] * wrep
        for j in range(per_page):
            blk = p * per_page + j
            dst = (blk % 2) * (n_cmp // 2) + blk // 2
            kcv[dst:dst + 1, :] = jnp.sum(x[j * CMP_BLOCK:(j + 1) * CMP_BLOCK], axis=0, keepdims=True)
        kaug[p * page:(p + 1) * page, 0:LANES] = sel_pages[p][:, 0:LANES].astype(MXU_DTYPE)
        vsel[p * page:(p + 1) * page, :] = sel_pages[p][:, LANES:2 * LANES].astype(MXU_DTYPE)

    o_c, sels = _compressed_branch(qpad, kcv[...], pos_q, rows, n_sel)
    _store_selection(lhs, sels, rows)

    def new_rows(ref):
        kv = jnp.concatenate([ref[0], jnp.zeros((LANES - t_len, ref.shape[2]), F32)], axis=0)
        return kv[:, 0:LANES].astype(MXU_DTYPE), kv[:, LANES:2 * LANES].astype(MXU_DTYPE)

    tk_new = lax.broadcasted_iota(jnp.int32, (rows, LANES), 1)
    m_new = (tk_new < t_len) & (tk_new <= tq)

    def joint_attention(s_past, mask_past, v_past, s_new, v_new):
        outs = []
        p_past, p_new = [], []
        for h in range(N_HEADS):
            sl = slice(h * rows, (h + 1) * rows)
            sp = s_past[sl] if mask_past is None else jnp.where(mask_past, s_past[sl], NEG_INF)
            sn = jnp.where(m_new, s_new[sl], NEG_INF)
            m = jnp.maximum(jnp.max(sp, axis=-1, keepdims=True), jnp.max(sn, axis=-1, keepdims=True))
            pp = jnp.exp(sp - m) if mask_past is None else jnp.where(mask_past, jnp.exp(sp - m), 0.0)
            pn = jnp.where(m_new, jnp.exp(sn - m), 0.0)
            den = jnp.maximum(jnp.sum(pp, axis=-1, keepdims=True) + jnp.sum(pn, axis=-1, keepdims=True), TINY)
            p_past.append((pp / den).astype(MXU_DTYPE))
            p_new.append((pn / den).astype(MXU_DTYPE))
        return (jnp.dot(jnp.concatenate(p_past, axis=0), v_past, preferred_element_type=F32)
                + jnp.dot(jnp.concatenate(p_new, axis=0), v_new, preferred_element_type=F32))

    kn, vn = new_rows(ksn_ref)
    s_past = lax.dot_general(lhs[...], kaug[...], _NT, preferred_element_type=F32)
    s_new = lax.dot_general(qpad, kn, _NT, preferred_element_type=F32)
    o_s = joint_attention(s_past, None, vsel[...], s_new, vn)

    n_buf = win_ref.shape[0]
    kwn, vwn = new_rows(kwn_ref)
    kwb = win_ref[:, 0:LANES].astype(MXU_DTYPE)
    vwb = win_ref[:, LANES:2 * LANES].astype(MXU_DTYPE)
    s_wb = lax.dot_general(qpad, kwb, _NT, preferred_element_type=F32)
    s_wn = lax.dot_general(qpad, kwn, _NT, preferred_element_type=F32)
    pos_w = past - n_buf + lax.broadcasted_iota(jnp.int32, (rows, n_buf), 1)
    dlt = pos_q - pos_w
    m_wb = (dlt >= 0) & (dlt <= WINDOW) & (pos_w >= 0)
    o_w = joint_attention(s_wb, m_wb, vwb, s_wn, vwn)

    for m, slab in enumerate(_gated_output(
            jnp.concatenate([gate_ref[0], jnp.zeros((rows - t_len, LANES), F32)], axis=0), o_c, o_s, o_w, rows)):
        out_ref[0, :, m * LANES:(m + 1) * LANES] = slab[0:t_len]

    nwin_ref[0:n_buf - t_len, :] = win_ref[t_len:n_buf, :]
    nwin_ref[n_buf - t_len:n_buf, :] = kwn_ref[0]


def _nsa_sample(page_table, q, gate, kvs_new, kvw_new, w_cmp, pool_c, pool_s, win, layer):
    dbs, t_len, d_b = q.shape
    n_pages = page_table.shape[1]
    page, kv_w = pool_c.shape[2], pool_c.shape[3]
    n_buf = win.shape[2]
    past = n_pages * page
    tok = lambda w: pl.BlockSpec((1, t_len, w), lambda b, pt: (b, 0, 0))

    def page_spec(p):
        return pl.BlockSpec((None, None, page, kv_w), lambda b, pt, p=p: (layer, pt[b * n_pages + p], 0, 0))

    grid_spec = pltpu.PrefetchScalarGridSpec(
        num_scalar_prefetch=1,
        grid=(dbs,),
        in_specs=[tok(d_b), tok(LANES), tok(kv_w), tok(kv_w),
                  pl.BlockSpec(w_cmp.shape, lambda b, pt: (0, 0)),
                  pl.BlockSpec((None, None, n_buf, kv_w), lambda b, pt: (layer, b, 0, 0))]
                 + [page_spec(p) for p in range(n_pages)] * 2,
        out_specs=[tok(d_b), pl.BlockSpec((None, n_buf, kv_w), lambda b, pt: (b, 0, 0))],
        scratch_shapes=[pltpu.VMEM((past, 2 * LANES), MXU_DTYPE), pltpu.VMEM((past, LANES), MXU_DTYPE),
                        pltpu.VMEM((past // CMP_BLOCK, kv_w), F32), pltpu.VMEM((N_HEADS * 8, 2 * LANES), MXU_DTYPE)],
    )
    return pl.pallas_call(
        functools.partial(_nsa_sample_kernel, n_pages=n_pages, page=page, past=past, t_len=t_len),
        grid_spec=grid_spec,
        out_shape=[jax.ShapeDtypeStruct((dbs, t_len, d_b), F32), jax.ShapeDtypeStruct((dbs, n_buf, kv_w), F32)],
        compiler_params=_cparams("arbitrary"),
        name="nsa_sample",
    )(page_table.reshape(-1), q, gate, kvs_new, kvw_new, w_cmp, win, *([pool_c] * n_pages), *([pool_s] * n_pages))


def _out_proj_router_kernel(x_ref, oa_ref, ob_ref, oc_ref, oas_ref, obs_ref, ocs_ref, wo_ref, g2_ref, wr_ref, br_ref,
                            xn_ref, h2_ref, ri_ref, rw_ref, cnt_ref, run_ref, *, n_ptiles):
    tm = x_ref.shape[0]

    @pl.when(pl.program_id(0) == 0)
    def _():
        run_ref[...] = jnp.zeros(run_ref.shape, F32)

    is_sample = pl.program_id(0) >= n_ptiles
    mix = jnp.concatenate([jnp.where(is_sample, oas_ref[...], oa_ref[...]),
                           jnp.where(is_sample, obs_ref[...], ob_ref[...]),
                           jnp.where(is_sample, ocs_ref[...], oc_ref[...])], axis=1).astype(wo_ref.dtype)
    xn = x_ref[...] + jnp.dot(mix, wo_ref[...], preferred_element_type=F32, precision=_precision_for(wo_ref.dtype))
    xn_ref[...] = xn
    h2 = _rmsnorm(xn, g2_ref[...])
    h2_ref[...] = h2
    logits = jnp.dot(h2, wr_ref[...], preferred_element_type=F32, precision=lax.Precision.HIGHEST) + br_ref[...]

    lane = lax.broadcasted_iota(jnp.int32, (tm, LANES), 1)
    lanef = lane.astype(F32)

    def softmax_over(mask):
        m = jnp.max(jnp.where(mask, logits, -jnp.inf), axis=-1, keepdims=True)
        e = jnp.where(mask, jnp.exp(logits - m), 0.0)
        return e / jnp.sum(e, axis=-1, keepdims=True)

    def first_max(p, mask):
        pm = jnp.max(jnp.where(mask, p, -1.0), axis=-1, keepdims=True)
        idx = jnp.min(jnp.where(mask & (p == pm), lanef, 1e9), axis=-1, keepdims=True)
        return pm, idx

    is_g = lane < N_GROUPS
    g_wt, g_sel = first_max(softmax_over(is_g), is_g)
    lo = N_GROUPS + EXP_PER_GROUP * g_sel
    in_e = (lanef >= lo) & (lanef < lo + EXP_PER_GROUP)
    p_e = softmax_over(in_e)
    p0, i0 = first_max(p_e, in_e)
    p1, i1 = first_max(p_e, in_e & (lanef != i0))
    den = p0 + p1
    e0 = i0 - N_GROUPS
    e1 = i1 - N_GROUPS

    hit0 = lanef == e0
    hit1 = lanef == e1
    onehot = jnp.where(hit0 | hit1, 1.0, 0.0)
    r_i = lax.broadcasted_iota(jnp.int32, (tm, tm), 0)
    c_i = lax.broadcasted_iota(jnp.int32, (tm, tm), 1)
    ltri = jnp.where(c_i < r_i, 1.0, 0.0).astype(jnp.bfloat16)
    before = jnp.dot(ltri, onehot.astype(jnp.bfloat16), preferred_element_type=F32) + run_ref[0:1, :]
    r0 = jnp.sum(jnp.where(hit0, before, 0.0), axis=-1, keepdims=True)
    r1 = jnp.sum(jnp.where(hit1, before, 0.0), axis=-1, keepdims=True)
    run_ref[...] = run_ref[...] + jnp.sum(onehot, axis=0, keepdims=True)
    cnt_ref[...] = run_ref[...]

    ri = jnp.where(lane == 0, e0, jnp.where(lane == 1, e1, jnp.where(lane == 2, r0, jnp.where(lane == 3, r1, 0.0))))
    ri_ref[...] = ri.astype(jnp.int32)
    rw_ref[...] = jnp.where(lane == 0, g_wt * p0 / den, jnp.where(lane == 1, g_wt * p1 / den, 0.0))


def _out_proj_router(x, mix_p, mix_s, w_out, g2, w_route, b_route):
    n, d = x.shape
    tm = TOKEN_TILE
    n_ptiles = mix_p[0].shape[0] // tm
    assert all(a.shape[0] == tm for a in mix_s) and n == (n_ptiles + 1) * tm
    row = lambda w: pl.BlockSpec((tm, w), lambda i: (i, 0))
    prow = lambda a: pl.BlockSpec((tm, a.shape[1]), lambda i: (jnp.minimum(i, n_ptiles - 1), 0))
    full = lambda a: pl.BlockSpec(a.shape, lambda i: (0,) * a.ndim)
    return pl.pallas_call(
        functools.partial(_out_proj_router_kernel, n_ptiles=n_ptiles),
        grid=(n // tm,),
        in_specs=[row(d)] + [prow(a) for a in mix_p] + [full(a) for a in mix_s]
                 + [full(w_out), full(g2), full(w_route), full(b_route)],
        out_specs=[row(d), row(d), row(LANES), row(LANES), pl.BlockSpec((8, LANES), lambda i: (0, 0))],
        out_shape=[jax.ShapeDtypeStruct((n, d), F32), jax.ShapeDtypeStruct((n, d), F32),
                   jax.ShapeDtypeStruct((n, LANES), jnp.int32), jax.ShapeDtypeStruct((n, LANES), F32),
                   jax.ShapeDtypeStruct((8, LANES), F32)],
        scratch_shapes=[pltpu.VMEM((8, LANES), F32)],
        compiler_params=_cparams("arbitrary"),
        name="out_proj_router",
    )(x, *mix_p, *mix_s, w_out, g2, w_route, b_route)


def _row_copy(src_hbm, row, dst, slot, r, sem):
    return pltpu.make_async_copy(src_hbm.at[pl.ds(row, 1)], dst.at[slot, pl.ds(r, 1)], sem.at[slot])


def _expert_kernel(te_ref, nv_ref, src_ref, h_hbm, wg_ref, wu_ref, wd_ref, ys_ref, xbuf, sem):
    t = pl.program_id(0)
    nv = nv_ref[0]
    te = xbuf.shape[1]

    def issue(tile, slot):
        def body(r, carry):
            _row_copy(h_hbm, src_ref[tile * te + r], xbuf, slot, r, sem).start()
            return carry
        lax.fori_loop(0, te, body, 0, unroll=8)

    def wait_all(slot):
        def body(r, carry):
            _row_copy(h_hbm, 0, xbuf, slot, r, sem).wait()
            return carry
        lax.fori_loop(0, te, body, 0, unroll=8)

    @pl.when(t == 0)
    def _():
        issue(0, 0)

    @pl.when(t + 1 < nv)
    def _():
        issue(t + 1, (t + 1) % 2)

    @pl.when(t < nv)
    def _():
        slot = t % 2
        wait_all(slot)
        x = xbuf[slot].astype(MXU_DTYPE)
        hg = jnp.dot(x, wg_ref[...].astype(MXU_DTYPE), preferred_element_type=F32)
        hu = jnp.dot(x, wu_ref[...].astype(MXU_DTYPE), preferred_element_type=F32)
        act = (jax.nn.silu(hg) * hu).astype(MXU_DTYPE)
        ys_ref[...] = jnp.dot(act, wd_ref[...].astype(MXU_DTYPE), preferred_element_type=F32)

    @pl.when(t >= nv)
    def _():
        ys_ref[...] = jnp.zeros(ys_ref.shape, F32)


def _experts(tile_expert, n_valid, src_tok, h2, w_gate, w_up, w_down, layer):
    n_tiles = tile_expert.shape[0]
    te = EXPERT_TILE
    d = h2.shape[1]
    d_e = w_gate.shape[3]
    wspec = lambda r, c: pl.BlockSpec((None, None, r, c), lambda t, te_r, nv_r, src_r: (layer, te_r[t], 0, 0))
    grid_spec = pltpu.PrefetchScalarGridSpec(
        num_scalar_prefetch=3,
        grid=(n_tiles,),
        in_specs=[pl.BlockSpec(memory_space=pl.ANY), wspec(d, d_e), wspec(d, d_e), wspec(d_e, d)],
        out_specs=pl.BlockSpec((te, d), lambda t, te_r, nv_r, src_r: (t, 0)),
        scratch_shapes=[pltpu.VMEM((2, te, d), F32), pltpu.SemaphoreType.DMA((2,))],
    )
    return pl.pallas_call(
        _expert_kernel,
        grid_spec=grid_spec,
        out_shape=jax.ShapeDtypeStruct((n_tiles * te, d), F32),
        compiler_params=_cparams("arbitrary"),
        name="moe_experts",
    )(tile_expert, n_valid, src_tok, h2, w_gate, w_up, w_down)


def _combine_kernel(d0_ref, d1_ref, x_ref, rw_ref, gf_ref, ys_hbm, *rest, final):
    if final:
        xo_ref, y_ref, buf, sem = rest
    else:
        xo_ref, buf, sem = rest
    t = pl.program_id(0)
    nt = pl.num_programs(0)
    tm = x_ref.shape[0]

    def issue(tile, slot):
        def body(r, carry):
            _row_copy(ys_hbm, d0_ref[tile * tm + r], buf, slot, r, sem).start()
            _row_copy(ys_hbm, d1_ref[tile * tm + r], buf, slot, tm + r, sem).start()
            return carry
        lax.fori_loop(0, tm, body, 0, unroll=8)

    def wait_all(slot):
        def body(r, carry):
            _row_copy(ys_hbm, 0, buf, slot, r, sem).wait()
            return carry
        lax.fori_loop(0, 2 * tm, body, 0, unroll=8)

    @pl.when(t == 0)
    def _():
        issue(0, 0)

    @pl.when(t + 1 < nt)
    def _():
        issue(t + 1, (t + 1) % 2)

    slot = t % 2
    wait_all(slot)
    rw = rw_ref[...]
    xo = x_ref[...] + rw[:, 0:1] * buf[slot, 0:tm, :] + rw[:, 1:2] * buf[slot, tm:2 * tm, :]
    xo_ref[...] = xo
    if final:
        y_ref[...] = _rmsnorm(xo, gf_ref[...])


def _combine(d0, d1, x, rw, gf, ys, final):
    n, d = x.shape
    tm = COMBINE_TILE
    row = lambda w: pl.BlockSpec((tm, w), lambda t, a, b: (t, 0))
    n_out = 2 if final else 1
    grid_spec = pltpu.PrefetchScalarGridSpec(
        num_scalar_prefetch=2,
        grid=(n // tm,),
        in_specs=[row(d), row(LANES), pl.BlockSpec((1, d), lambda t, a, b: (0, 0)), pl.BlockSpec(memory_space=pl.ANY)],
        out_specs=[row(d)] * n_out,
        scratch_shapes=[pltpu.VMEM((2, 2 * tm, d), F32), pltpu.SemaphoreType.DMA((2,))],
    )
    return pl.pallas_call(
        functools.partial(_combine_kernel, final=final),
        grid_spec=grid_spec,
        out_shape=[jax.ShapeDtypeStruct((n, d), F32)] * n_out,
        compiler_params=_cparams("arbitrary"),
        name="moe_combine_final" if final else "moe_combine",
    )(d0, d1, x, rw, gf, ys)


def _route_plan(ri, cnt, n_tiles):
    te = EXPERT_TILE
    n = ri.shape[0]
    counts = cnt[0, :N_EXPERTS].astype(jnp.int32)
    padded = ((counts + te - 1) // te) * te
    ends = jnp.cumsum(padded)
    offs = ends - padded
    d0 = offs[ri[:, 0]] + ri[:, 2]
    d1 = offs[ri[:, 1]] + ri[:, 3]
    tok = jnp.arange(n, dtype=jnp.int32)
    src = jnp.zeros((n_tiles * te,), jnp.int32).at[d0].set(tok).at[d1].set(tok)
    tile_start = jnp.arange(n_tiles, dtype=jnp.int32) * te
    tile_expert = jnp.minimum(jnp.sum((ends[None, :] <= tile_start[:, None]).astype(jnp.int32), axis=1),
                              N_EXPERTS - 1)
    n_valid = (ends[-1:] // te).astype(jnp.int32)
    return d0, d1, src, tile_expert, n_valid


def _rope_tables(pos):
    half = HEAD_DIM // 2
    inv = ROPE_THETA ** (-jnp.arange(half, dtype=F32) / half)
    ang = pos.astype(F32)[:, None] * inv[None, :]
    cos = jnp.cos(ang)
    sin = jnp.sin(ang)
    cos_t = jnp.concatenate([cos, cos] * (LANES // HEAD_DIM), axis=1)
    sin_t = jnp.concatenate([-sin, sin] * (LANES // HEAD_DIM), axis=1)
    return cos_t, sin_t


def _block_diag(w):
    nb, bw, _ = w.shape
    out = jnp.zeros((nb * bw, nb * bw), w.dtype)
    for i in range(nb):
        out = out.at[i * bw:(i + 1) * bw, i * bw:(i + 1) * bw].set(w[i])
    return out


def kernel(x_prompt, x_sample, cache_kv_cmp, cache_kv_sel, cache_kv_win, state_lru_h, state_lru_conv, state_sconv, page_table, norm1_g, w_in, lru_conv_w, lru_conv_b, lru_wa, lru_ba, lru_wx, lru_bx, lru_lambda, nsa_cmp_wk, nsa_cmp_wv, sc_conv_w, w_out, norm2_g, router_group_w, router_group_b, router_exp_w, router_exp_b, exp_w_gate, exp_w_up, exp_w_down, norm_f_g):
    bsz, seq, d = x_prompt.shape
    dbs, t_len, _ = x_sample.shape
    depth = w_in.shape[0]
    d_a = lru_conv_w.shape[2]
    d_c = sc_conv_w.shape[2]
    kv_w = 2 * N_KV * HEAD_DIM
    d_b = N_HEADS * HEAD_DIM
    n_gate = 3 * N_HEADS
    n_p = bsz * seq
    n_s = dbs * t_len
    n = n_p + n_s
    assert n_s == TOKEN_TILE and n_p % TOKEN_TILE == 0 and seq % TOKEN_TILE == 0
    page = cache_kv_cmp.shape[2]
    past = page_table.shape[1] * page
    n_buf = cache_kv_win.shape[2]

    pos = jnp.concatenate([jnp.arange(seq, dtype=jnp.int32),
                           jnp.tile(past + jnp.arange(t_len, dtype=jnp.int32), dbs)])
    cos_t, sin_t = _rope_tables(pos)

    pool_c = cache_kv_cmp.reshape(depth, -1, page, kv_w)
    pool_s = cache_kv_sel.reshape(depth, -1, page, kv_w)
    win = cache_kv_win.reshape(depth, dbs, n_buf, kv_w)

    g_off = 2 * d_a + d_b + 3 * kv_w
    x = jnp.concatenate([x_prompt.reshape(n_p, d), x_sample.reshape(n_s, d)], axis=0)
    h0_p = jnp.zeros((bsz, 1, d_a), F32)
    lb0_p = jnp.zeros((bsz, lru_conv_w.shape[1] - 1, d_a), F32)
    sb0_p = jnp.zeros((bsz, sc_conv_w.shape[1] - 1, d_c), F32)

    n_tiles = (2 * n) // EXPERT_TILE + N_EXPERTS
    proj_dtype = lambda l: F32 if l < depth - 1 else MXU_DTYPE
    states_p, states_s = [], []
    y = None
    for l in range(depth):
        w_l = w_in[l]
        w_r = jnp.concatenate([w_l[:, :g_off], w_l[:, g_off + n_gate:], w_l[:, g_off:g_off + n_gate],
                               jnp.zeros((d, LANES - n_gate), F32)], axis=1).astype(proj_dtype(l))
        lru_in, sc_in, gate, q, kvc, kvs, kvw, kvs_b, kvw_b = _in_proj(
            x, norm1_g[l][None], w_r, cos_t, sin_t, n_p, seq, d_a=d_a, d_b=d_b, kv_w=kv_w, d_c=d_c)

        seq_w = (lru_conv_w[l], lru_conv_b[l][None], _block_diag(lru_wa[l]).astype(MXU_DTYPE), lru_ba[l][None],
                 _block_diag(lru_wx[l]).astype(MXU_DTYPE), lru_bx[l][None], lru_lambda[l][None], sc_conv_w[l])
        out_a, out_c, hn_p, lbn_p, sbn_p = _seq_mix_prompt(lru_in, sc_in, h0_p, lb0_p, sb0_p, seq_w, bsz, seq,
                                                           d_a=d_a, d_c=d_c)
        tmaj = lambda a: jnp.swapaxes(a.reshape(dbs, t_len, -1), 0, 1)
        oa_s, oc_s, hn_s, lbn_s, sbn_s = _seq_mix_sample(
            tmaj(lru_in[n_p:]), tmaj(sc_in[n_p:]), state_lru_h[l], jnp.swapaxes(state_lru_conv[l], 0, 1),
            jnp.swapaxes(state_sconv[l], 0, 1), seq_w, d_a=d_a, d_c=d_c)
        oa_s = jnp.swapaxes(oa_s, 0, 1).reshape(n_s, d_a)
        oc_s = jnp.swapaxes(oc_s, 0, 1).reshape(n_s, d_c)

        w_cmp = jnp.concatenate([jnp.broadcast_to(nsa_cmp_wk[l][:, None], (CMP_BLOCK, kv_w // 2)),
                                 jnp.broadcast_to(nsa_cmp_wv[l][:, None], (CMP_BLOCK, kv_w // 2))], axis=1)
        kcv = _compress_prompt(kvc, w_cmp, bsz, seq)
        out_b = _nsa_prompt(q, gate, kcv, kvs_b, kvw_b, bsz, seq)
        s3 = lambda a: a[n_p:].reshape(dbs, t_len, -1)
        ob_s, nwin_s = _nsa_sample(page_table, s3(q), s3(gate), s3(kvs), s3(kvw), w_cmp, pool_c, pool_s, win, l)
        ob_s = ob_s.reshape(n_s, d_b)

        w_route = jnp.concatenate([router_group_w[l], router_exp_w[l],
                                   jnp.zeros((d, LANES - N_GROUPS - N_EXPERTS), F32)], axis=1)
        b_route = jnp.concatenate([router_group_b[l], router_exp_b[l],
                                   jnp.zeros((LANES - N_GROUPS - N_EXPERTS,), F32)])[None]
        xn, h2, ri, rw, cnt = _out_proj_router(x, (out_a, out_b, out_c), (oa_s, ob_s, oc_s),
                                               w_out[l].astype(proj_dtype(l)), norm2_g[l][None], w_route, b_route)
        d0, d1, src, tile_expert, n_valid = _route_plan(ri, cnt, n_tiles)
        ys = _experts(tile_expert, n_valid, src, h2, exp_w_gate, exp_w_up, exp_w_down, l)
        final = l == depth - 1
        outs = _combine(d0, d1, xn, rw, norm_f_g[None], ys, final)
        x = outs[0]
        if final:
            y = outs[1]

        kv6 = lambda a, lead: a.reshape(lead + (2, N_KV, HEAD_DIM))
        states_p.append((kv6(kvc[:n_p], (bsz, seq)), kv6(kvs[:n_p], (bsz, seq)),
                         kv6(kvw[:n_p].reshape(bsz, seq, kv_w)[:, seq - min(WINDOW, seq):], (bsz, min(WINDOW, seq))),
                         hn_p[:, 0], lbn_p, sbn_p))
        states_s.append((kv6(kvc[n_p:], (dbs, t_len)), kv6(kvs[n_p:], (dbs, t_len)), kv6(nwin_s, (dbs, n_buf)),
                         hn_s, jnp.swapaxes(lbn_s, 0, 1), jnp.swapaxes(sbn_s, 0, 1)))

    stack = lambda sts, i: jnp.stack([s[i] for s in sts])
    res = [y[:n_p].reshape(bsz, seq, d), y[n_p:].reshape(dbs, t_len, d)]
    for i in range(6):
        res += [stack(states_p, i), stack(states_s, i)]
    return tuple(res)
```

```python
import functools

import jax
import jax.numpy as jnp
from jax import lax
from jax.experimental import pallas as pl
from jax.experimental.pallas import tpu as pltpu

F32 = jnp.float32
MXU_DTYPE = jnp.bfloat16

HEAD_DIM = 64
N_HEADS = 8
N_KV = 2
GROUP = N_HEADS // N_KV
CMP_BLOCK = 32
SEL_BLOCK = 64
TOP_N = 16
WINDOW = 512
Q_BLOCK = 128
ROPE_THETA = 10000.0
LRU_C = 8.0
N_GROUPS = 4
EXP_PER_GROUP = 8
N_EXPERTS = N_GROUPS * EXP_PER_GROUP
RMS_EPS = 1e-6
NEG_INF = -1e30
TINY = 1e-30
FORCE = 1e6

LANES = 128
VMEM_LIMIT = 56 * 2 ** 20
TOKEN_TILE = 512
TIME_CHUNK = 256
KEY_TILE = 512
EXPERT_TILE = 256
COMBINE_TILE = 256

_NT = (((1,), (1,)), ((), ()))


def _cparams(*sem):
    return pltpu.CompilerParams(dimension_semantics=sem, vmem_limit_bytes=VMEM_LIMIT)


def _split_weight(w, parts):
    hi = w.astype(MXU_DTYPE)
    if parts == 1:
        return hi[None]
    return jnp.stack([hi, (w - hi.astype(F32)).astype(MXU_DTYPE)])


def _split_act(a, parts):
    hi = a.astype(MXU_DTYPE)
    return (hi,) if parts == 1 else (hi, (a - hi.astype(F32)).astype(MXU_DTYPE))


def _split_dot(a_parts, w_ref, cols=slice(None)):
    dot = lambda a, p: jnp.dot(a, w_ref[p, :, cols], preferred_element_type=F32)
    if len(a_parts) == 1:
        return dot(a_parts[0], 0)
    return (dot(a_parts[0], 1) + dot(a_parts[1], 0)) + dot(a_parts[0], 0)


def _rmsnorm(x, g):
    return x * lax.rsqrt(jnp.mean(x * x, axis=-1, keepdims=True) + RMS_EPS) * g


def _swap_halves(x):
    w = x.shape[-1]
    lane = lax.broadcasted_iota(jnp.int32, x.shape, x.ndim - 1)
    first = (lane % HEAD_DIM) < HEAD_DIM // 2
    return jnp.where(first, pltpu.roll(x, w - HEAD_DIM // 2, axis=1), pltpu.roll(x, HEAD_DIM // 2, axis=1))


def _masked_softmax(s, mask):
    s = jnp.where(mask, s, NEG_INF)
    m = jnp.max(s, axis=-1, keepdims=True)
    p = jnp.where(mask, jnp.exp(s - m), 0.0)
    return p / jnp.maximum(jnp.sum(p, axis=-1, keepdims=True), TINY)


def _select_blocks_keys(imp, pos_q, n_sel):
    blk = lax.broadcasted_iota(jnp.int32, imp.shape, 0)
    cur = pos_q // SEL_BLOCK
    forced = (blk == 0) | (blk == cur) | (blk == cur - 1)
    w = jnp.where(forced, FORCE, imp)
    w = jnp.where(blk <= cur, w, NEG_INF)
    w = jnp.where(blk < n_sel, w, -jnp.inf)
    blkf = blk.astype(F32)

    def body(_, carry):
        w, sel = carry
        m = jnp.max(w, axis=0, keepdims=True)
        first = jnp.min(jnp.where(w == m, blkf, 1e9), axis=0, keepdims=True)
        pick = blkf == first
        return jnp.where(pick, -jnp.inf, w), jnp.where(pick, 1.0, sel)

    _, sel = lax.fori_loop(0, min(TOP_N, n_sel), body, (w, jnp.zeros_like(w)))
    return sel


def _in_proj_kernel(x_ref, g_ref, w_ref, cos_ref, sin_ref,
                    lru_ref, sc_ref, gate_ref, q_ref, kvc_ref, kvs_ref, kvw_ref, kvsb_ref, kvwb_ref,
                    *, d_a, d_b, kv_w, d_c):
    h_parts = _split_act(_rmsnorm(x_ref[...], g_ref[...]), w_ref.shape[0])

    def mm(a, b):
        return _split_dot(h_parts, w_ref, slice(a, b))

    cos = cos_ref[...]
    sin = sin_ref[...]
    off = 2 * d_a
    lru_ref[...] = mm(0, off)
    q = mm(off, off + d_b)
    reps = d_b // LANES
    cos_q = jnp.concatenate([cos] * reps, axis=1)
    sin_q = jnp.concatenate([sin] * reps, axis=1)
    q_ref[...] = (q * cos_q + _swap_halves(q) * sin_q) * (HEAD_DIM ** -0.5)
    off += d_b
    for ref, bref in ((kvc_ref, None), (kvs_ref, kvsb_ref), (kvw_ref, kvwb_ref)):
        kv = mm(off, off + kv_w)
        k = kv[:, :kv_w // 2]
        kv = jnp.concatenate([k * cos + _swap_halves(k) * sin, kv[:, kv_w // 2:]], axis=1)
        ref[...] = kv
        if bref is not None:
            bref[...] = kv.astype(MXU_DTYPE)
        off += kv_w
    sc_ref[...] = mm(off, off + 3 * d_c)
    off += 3 * d_c
    gate_ref[...] = jax.nn.sigmoid(mm(off, off + LANES))


def _in_proj(x, g1, w_r, cos_t, sin_t, n_prompt, seq, *, d_a, d_b, kv_w, d_c):
    n, d = x.shape
    tm = TOKEN_TILE
    n_ptiles = n_prompt // tm
    n_stiles = seq // tm

    def tab_map(i):
        return (jnp.where(i < n_ptiles, i % n_stiles, n_stiles), 0)

    row = lambda w: pl.BlockSpec((tm, w), lambda i: (i, 0))
    widths = (2 * d_a, 3 * d_c, LANES, d_b, kv_w, kv_w, kv_w, kv_w, kv_w)
    dtypes = (F32, F32, F32, F32, F32, F32, F32, MXU_DTYPE, MXU_DTYPE)
    return pl.pallas_call(
        functools.partial(_in_proj_kernel, d_a=d_a, d_b=d_b, kv_w=kv_w, d_c=d_c),
        grid=(n // tm,),
        in_specs=[row(d),
                  pl.BlockSpec((1, d), lambda i: (0, 0)),
                  pl.BlockSpec(w_r.shape, lambda i: (0, 0, 0)),
                  pl.BlockSpec((tm, LANES), tab_map),
                  pl.BlockSpec((tm, LANES), tab_map)],
        out_specs=[row(w) for w in widths],
        out_shape=[jax.ShapeDtypeStruct((n, w), dt) for w, dt in zip(widths, dtypes)],
        compiler_params=_cparams("parallel"),
        name="in_proj",
    )(x, g1, w_r, cos_t, sin_t)


def _scan_rows(a, u):
    t = a.shape[0]
    row = lax.broadcasted_iota(jnp.int32, a.shape, 0)
    d = 1
    while d < t:
        keep = row >= d
        a_sh = jnp.where(keep, pltpu.roll(a, d, axis=0), 1.0)
        u_sh = jnp.where(keep, pltpu.roll(u, d, axis=0), 0.0)
        u = a * u_sh + u
        a = a * a_sh
        d *= 2
    return a, u


def _lru_gates(xc, wa_ref, ba_ref, wx_ref, bx_ref, lam_ref):
    xb = xc.astype(MXU_DTYPE)
    r = jax.nn.sigmoid(jnp.dot(xb, wa_ref[...], preferred_element_type=F32) + ba_ref[...])
    i = jax.nn.sigmoid(jnp.dot(xb, wx_ref[...], preferred_element_type=F32) + bx_ref[...])
    log_a = -LRU_C * r * jax.nn.softplus(-lam_ref[...])
    a = jnp.exp(log_a)
    th = jnp.tanh(log_a)
    u = jnp.sqrt(-2.0 * th / (1.0 - th)) * (i * xc)
    return a, u


def _seq_mix_kernel(lru_ref, sc_ref, h0_ref, lb0_ref, sb0_ref, cw_ref, cb_ref, wa_ref, ba_ref, wx_ref, bx_ref,
                    lam_ref, scw_ref, oa_ref, oc_ref, hn_ref, lbn_ref, sbn_ref, xpad, vpad, hcar, *, d_a, d_c):
    c = pl.program_id(1)
    tc = lru_ref.shape[0]
    nlb = lb0_ref.shape[1]
    nsb = sb0_ref.shape[1]

    @pl.when(c == 0)
    def _():
        xpad[0:8, :] = jnp.zeros((8, d_a), F32)
        xpad[8 - nlb:8, :] = lb0_ref[0]
        vpad[0:8, :] = jnp.zeros((8, d_c), F32)
        vpad[8 - nsb:8, :] = sb0_ref[0]
        hcar[...] = h0_ref[0]

    xa = lru_ref[:, 0:d_a]
    ga = lru_ref[:, d_a:2 * d_a]
    xpad[8:8 + tc, :] = xa
    xc = cw_ref[0:1, :] * xpad[8 - nlb:8 - nlb + tc, :]
    for j in range(1, nlb):
        xc = xc + cw_ref[j:j + 1, :] * xpad[8 - nlb + j:8 - nlb + j + tc, :]
    xc = xc + cw_ref[nlb:nlb + 1, :] * xa + cb_ref[...]
    a, u = _lru_gates(xc, wa_ref, ba_ref, wx_ref, bx_ref, lam_ref)
    a_cum, hs = _scan_rows(a, u)
    hs = hs + a_cum * hcar[...]
    hcar[...] = hs[tc - 1:tc, :]
    oa_ref[...] = hs * jax.nn.gelu(ga)

    v = sc_ref[:, 0:d_c] * sc_ref[:, 2 * d_c:3 * d_c]
    vpad[8:8 + tc, :] = v
    uc = scw_ref[0:1, :] * vpad[8 - nsb:8 - nsb + tc, :]
    for j in range(1, nsb):
        uc = uc + scw_ref[j:j + 1, :] * vpad[8 - nsb + j:8 - nsb + j + tc, :]
    uc = uc + scw_ref[nsb:nsb + 1, :] * v
    oc_ref[...] = sc_ref[:, d_c:2 * d_c] * uc

    hn_ref[0] = hs[tc - 1:tc, :]
    lbn_ref[0] = xpad[8 + tc - nlb:8 + tc, :]
    sbn_ref[0] = vpad[8 + tc - nsb:8 + tc, :]
    xpad[0:8, :] = xpad[tc:tc + 8, :]
    vpad[0:8, :] = vpad[tc:tc + 8, :]


def _seq_mix_prompt(lru_in, sc_in, h0, lb0, sb0, wts, bsz, seq, *, d_a, d_c):
    n = bsz * seq
    tc = TIME_CHUNK
    nch = seq // tc
    row = lambda w: pl.BlockSpec((tc, w), lambda b, c: (b * nch + c, 0))
    full = lambda a: pl.BlockSpec(a.shape, lambda b, c: (0,) * a.ndim)
    state = lambda r, w: pl.BlockSpec((1, r, w), lambda b, c: (b, 0, 0))
    nlb, nsb = lb0.shape[1], sb0.shape[1]
    return pl.pallas_call(
        functools.partial(_seq_mix_kernel, d_a=d_a, d_c=d_c),
        grid=(bsz, nch),
        in_specs=[row(2 * d_a), row(3 * d_c), state(1, d_a), state(nlb, d_a), state(nsb, d_c)]
                 + [full(w) for w in wts],
        out_specs=[row(d_a), row(d_c), state(1, d_a), state(nlb, d_a), state(nsb, d_c)],
        out_shape=[jax.ShapeDtypeStruct((n, d_a), F32), jax.ShapeDtypeStruct((n, d_c), F32),
                   jax.ShapeDtypeStruct((bsz, 1, d_a), F32), jax.ShapeDtypeStruct((bsz, nlb, d_a), F32),
                   jax.ShapeDtypeStruct((bsz, nsb, d_c), F32)],
        scratch_shapes=[pltpu.VMEM((tc + 8, d_a), F32), pltpu.VMEM((tc + 8, d_c), F32), pltpu.VMEM((1, d_a), F32)],
        compiler_params=_cparams("arbitrary", "arbitrary"),
        name="seq_mix_prompt",
    )(lru_in, sc_in, h0, lb0, sb0, *wts)


def _seq_mix_sample_kernel(lru_ref, sc_ref, h0_ref, lb0_ref, sb0_ref, cw_ref, cb_ref, wa_ref, ba_ref, wx_ref,
                           bx_ref, lam_ref, scw_ref, oa_ref, oc_ref, hn_ref, lbn_ref, sbn_ref, *, d_a, d_c):
    t_len = lru_ref.shape[0]
    nlb = lb0_ref.shape[0]
    nsb = sb0_ref.shape[0]
    xs = [lb0_ref[j] for j in range(nlb)] + [lru_ref[t][:, 0:d_a] for t in range(t_len)]
    vs = [sb0_ref[j] for j in range(nsb)] + [sc_ref[t][:, 0:d_c] * sc_ref[t][:, 2 * d_c:3 * d_c] for t in range(t_len)]
    h = h0_ref[...]
    for t in range(t_len):
        xc = cw_ref[0:1, :] * xs[t]
        for j in range(1, nlb + 1):
            xc = xc + cw_ref[j:j + 1, :] * xs[t + j]
        xc = xc + cb_ref[...]
        a, u = _lru_gates(xc, wa_ref, ba_ref, wx_ref, bx_ref, lam_ref)
        h = a * h + u
        oa_ref[t] = h * jax.nn.gelu(lru_ref[t][:, d_a:2 * d_a])
        uc = scw_ref[0:1, :] * vs[t]
        for j in range(1, nsb + 1):
            uc = uc + scw_ref[j:j + 1, :] * vs[t + j]
        oc_ref[t] = sc_ref[t][:, d_c:2 * d_c] * uc
    hn_ref[...] = h
    for j in range(nlb):
        lbn_ref[j] = xs[t_len + j]
    for j in range(nsb):
        sbn_ref[j] = vs[t_len + j]


def _seq_mix_sample(lru_t, sc_t, h0, lb0_t, sb0_t, wts, *, d_a, d_c):
    t_len, bsz, _ = lru_t.shape
    outs = [jax.ShapeDtypeStruct((t_len, bsz, d_a), F32), jax.ShapeDtypeStruct((t_len, bsz, d_c), F32),
            jax.ShapeDtypeStruct(h0.shape, F32), jax.ShapeDtypeStruct(lb0_t.shape, F32),
            jax.ShapeDtypeStruct(sb0_t.shape, F32)]
    return pl.pallas_call(
        functools.partial(_seq_mix_sample_kernel, d_a=d_a, d_c=d_c),
        out_shape=outs,
        compiler_params=pltpu.CompilerParams(vmem_limit_bytes=VMEM_LIMIT),
        name="seq_mix_sample",
    )(lru_t, sc_t, h0, lb0_t, sb0_t, *wts)


def _compress_kernel(kvc_ref, w_ref, out_ref):
    half = out_ref.shape[0] // 2
    x = kvc_ref[...].reshape(half, 2 * CMP_BLOCK, out_ref.shape[1])
    w = w_ref[...][None]
    out_ref[0:half, :] = jnp.sum(x[:, 0:CMP_BLOCK, :] * w, axis=1)
    out_ref[half:2 * half, :] = jnp.sum(x[:, CMP_BLOCK:2 * CMP_BLOCK, :] * w, axis=1)


def _compress_prompt(kvc, w_cmp, bsz, seq):
    kv_w = kvc.shape[1]
    n_cmp = seq // CMP_BLOCK
    return pl.pallas_call(
        _compress_kernel,
        grid=(bsz,),
        in_specs=[pl.BlockSpec((seq, kv_w), lambda b: (b, 0)), pl.BlockSpec(w_cmp.shape, lambda b: (0, 0))],
        out_specs=pl.BlockSpec((n_cmp, kv_w), lambda b: (b, 0)),
        out_shape=jax.ShapeDtypeStruct((bsz * n_cmp, kv_w), F32),
        compiler_params=_cparams("parallel"),
        name="nsa_compress",
    )(kvc, w_cmp)


def _pad_heads(q, lhs_ref, rows):
    lane = lax.broadcasted_iota(jnp.int32, (rows, LANES), 1)
    for h in range(N_HEADS):
        k = h // GROUP
        slab = q[:, (h // 2) * LANES:(h // 2 + 1) * LANES]
        if h % 2 != k:
            slab = pltpu.roll(slab, HEAD_DIM, axis=1)
        keep = (lane >= k * HEAD_DIM) & (lane < (k + 1) * HEAD_DIM)
        lhs_ref[h * rows:(h + 1) * rows, 0:LANES] = jnp.where(keep, slab, 0.0).astype(MXU_DTYPE)


def _gated_output(gate, o_c, o_s, o_w, rows):
    lane = lax.broadcasted_iota(jnp.int32, (rows, LANES), 1)
    slabs = []
    for m in range(N_HEADS // 2):
        parts = []
        for h in (2 * m, 2 * m + 1):
            k = h // GROUP
            sl = slice(h * rows, (h + 1) * rows)
            r = (gate[:, 3 * h:3 * h + 1] * o_c[sl] + gate[:, 3 * h + 1:3 * h + 2] * o_s[sl]
                 + gate[:, 3 * h + 2:3 * h + 3] * o_w[sl])
            if h % 2 != k:
                r = pltpu.roll(r, HEAD_DIM, axis=1)
            parts.append(r)
        slabs.append(jnp.where(lane < HEAD_DIM, parts[0], parts[1]))
    return slabs


def _compressed_branch(qpad, kcv, pos0, rows, n_sel):
    n_cmp = kcv.shape[0]
    half = n_cmp // 2
    pos_q = pos0 + lax.broadcasted_iota(jnp.int32, (rows, 1), 0)
    kc = kcv[:, 0:LANES].astype(MXU_DTYPE)
    vc = kcv[:, LANES:2 * LANES].astype(MXU_DTYPE)
    s_c = lax.dot_general(qpad, kc, _NT, preferred_element_type=F32)
    col = lax.broadcasted_iota(jnp.int32, (rows, n_cmp), 1)
    blk = jnp.where(col < half, 2 * col, 2 * (col - half) + 1)
    m_c = (blk + 1) * CMP_BLOCK - 1 <= pos_q
    ps = []
    imp = [jnp.zeros((rows, n_cmp), F32) for _ in range(N_KV)]
    for h in range(N_HEADS):
        p = _masked_softmax(s_c[h * rows:(h + 1) * rows], m_c)
        imp[h // GROUP] = imp[h // GROUP] + p
        ps.append(p.astype(MXU_DTYPE))
    o_c = jnp.dot(jnp.concatenate(ps, axis=0), vc, preferred_element_type=F32)
    imps = []
    for k in range(N_KV):
        imp_s = imp[k][:, 0:half] + imp[k][:, half:n_cmp]
        if half < LANES:
            imp_s = jnp.concatenate([imp_s, jnp.zeros((rows, LANES - half), F32)], axis=1)
        imps.append(imp_s)
    imp_t = jnp.concatenate(imps + [jnp.zeros((LANES - N_KV * rows, LANES), F32)], axis=0).T
    n_rows = -(-n_sel // 8) * 8
    pos_l = pos0 + lax.broadcasted_iota(jnp.int32, (1, LANES), 1) % rows
    sel_t = _select_blocks_keys(imp_t[0:n_rows], pos_l, n_sel)
    sel = jnp.concatenate([sel_t, jnp.zeros((LANES - n_rows, LANES), F32)], axis=0).T
    return o_c, [sel[k * rows:(k + 1) * rows] for k in range(N_KV)]


def _store_selection(lhs_ref, sels, rows):
    for k in range(N_KV):
        neg = jnp.where(sels[k] > 0.0, 0.0, NEG_INF).astype(MXU_DTYPE)
        for g in range(GROUP):
            h = k * GROUP + g
            lhs_ref[h * rows:(h + 1) * rows, LANES:2 * LANES] = neg


def _block_onehot(n_rows):
    row = lax.broadcasted_iota(jnp.int32, (n_rows, LANES), 0)
    lane = lax.broadcasted_iota(jnp.int32, (n_rows, LANES), 1)
    return jnp.where(row // SEL_BLOCK == lane, 1.0, 0.0).astype(MXU_DTYPE)


def _masked_softmax_keys(s, mask):
    s = jnp.where(mask, s, NEG_INF)
    m = jnp.max(s, axis=0, keepdims=True)
    p = jnp.where(mask, jnp.exp(s - m), 0.0)
    return p / jnp.maximum(jnp.sum(p, axis=0, keepdims=True), TINY)


def _nsa_prompt_kernel(q_ref, gate_ref, kcv_ref, kvs_ref, kvw_ref, out_ref,
                       kaug, vst, vwt, vct, lhs, m_sc, l_sc, acc_sc, *, seq):
    qb = pl.program_id(1)
    rows = Q_BLOCK
    tk = KEY_TILE
    s0 = qb * rows
    n_sel = seq // SEL_BLOCK
    n_cmp = seq // CMP_BLOCK
    cols = N_HEADS * rows

    @pl.when(qb == 0)
    def _():
        kaug[:, 0:LANES] = kvs_ref[:, 0:LANES]
        kaug[:, LANES:2 * LANES] = _block_onehot(seq)
        vct[...] = kcv_ref[:, LANES:2 * LANES].T.astype(MXU_DTYPE)

        def tr_sel(i, c):
            r0 = pl.multiple_of(i * tk, tk)
            vst[i] = kvs_ref[pl.ds(r0, tk), LANES:2 * LANES].astype(F32).T.astype(MXU_DTYPE)
            return c

        def tr_win(i, c):
            r0 = pl.multiple_of(i * rows, rows)
            vwt[i] = kvw_ref[pl.ds(r0, rows), LANES:2 * LANES].astype(F32).T.astype(MXU_DTYPE)
            return c

        lax.fori_loop(0, seq // tk, tr_sel, 0)
        lax.fori_loop(0, seq // rows, tr_win, 0)

    _pad_heads(q_ref[...], lhs, rows)
    qpad = lhs[:, 0:LANES]
    lane = lax.broadcasted_iota(jnp.int32, (1, cols), 1)
    pos_q = s0 + lane % rows

    half = n_cmp // 2
    s_c = lax.dot_general(kcv_ref[:, 0:LANES].astype(MXU_DTYPE), qpad, _NT, preferred_element_type=F32)
    r_c = lax.broadcasted_iota(jnp.int32, (n_cmp, cols), 0)
    blk_c = jnp.where(r_c < half, 2 * r_c, 2 * (r_c - half) + 1)
    p_c = _masked_softmax_keys(s_c, (blk_c + 1) * CMP_BLOCK - 1 <= pos_q)
    o_c = jnp.dot(vct[...], p_c.astype(MXU_DTYPE), preferred_element_type=F32)

    imp = []
    for k in range(N_KV):
        acc = p_c[:, k * GROUP * rows:(k * GROUP + 1) * rows]
        for g in range(1, GROUP):
            acc = acc + p_c[:, (k * GROUP + g) * rows:(k * GROUP + g + 1) * rows]
        imp.append(acc[0:half] + acc[half:n_cmp])
    sel = _select_blocks_keys(jnp.concatenate(imp, axis=1), pos_q[:, 0:N_KV * rows], n_sel)
    for k in range(N_KV):
        sel_k = sel[:, k * rows:(k + 1) * rows]
        if n_sel < LANES:
            sel_k = jnp.concatenate([sel_k, jnp.zeros((LANES - n_sel, rows), F32)], axis=0)
        neg = jnp.where(sel_k.T > 0.0, 0.0, NEG_INF).astype(MXU_DTYPE)
        for g in range(GROUP):
            h = k * GROUP + g
            lhs[h * rows:(h + 1) * rows, LANES:2 * LANES] = neg

    m_sc[...] = jnp.full(m_sc.shape, NEG_INF, F32)
    l_sc[...] = jnp.zeros(l_sc.shape, F32)
    acc_sc[...] = jnp.zeros(acc_sc.shape, F32)

    def tile(kt, causal):
        k0 = pl.multiple_of(kt * tk, tk)
        s = lax.dot_general(kaug[pl.ds(k0, tk), :], lhs[...], _NT, preferred_element_type=F32)
        if causal:
            s = jnp.where(k0 + lax.broadcasted_iota(jnp.int32, (tk, cols), 0) <= pos_q, s, NEG_INF)
        m_old = m_sc[...]
        m_new = jnp.maximum(m_old, jnp.max(s, axis=0, keepdims=True))
        alpha = jnp.exp(m_old - m_new)
        p = jnp.exp(s - m_new)
        l_sc[...] = alpha * l_sc[...] + jnp.sum(p, axis=0, keepdims=True)
        m_sc[...] = m_new
        acc_sc[...] = alpha * acc_sc[...] + jnp.dot(vst[kt], p.astype(MXU_DTYPE), preferred_element_type=F32)

    n_full = s0 // tk

    def loop_body(kt, carry):
        tile(kt, False)
        return carry

    lax.fori_loop(0, n_full, loop_body, 0)
    tile(n_full, True)
    o_s = acc_sc[...] / jnp.maximum(l_sc[...], TINY)

    band = WINDOW + rows
    start = pl.multiple_of(jnp.maximum(s0 - WINDOW, 0), rows)
    s_w = lax.dot_general(kvw_ref[pl.ds(start, band), 0:LANES], qpad, _NT, preferred_element_type=F32)
    dlt = pos_q - (start + lax.broadcasted_iota(jnp.int32, (band, cols), 0))
    p_w = _masked_softmax_keys(s_w, (dlt >= 0) & (dlt <= WINDOW))
    t0 = start // rows
    vw = jnp.concatenate([vwt[t0 + j] for j in range(band // rows)], axis=1)
    o_w = jnp.dot(vw, p_w.astype(MXU_DTYPE), preferred_element_type=F32)

    g_t = gate_ref[...].T
    for m in range(N_HEADS // 2):
        parts = []
        for h in (2 * m, 2 * m + 1):
            k = h // GROUP
            rs = slice(k * HEAD_DIM, (k + 1) * HEAD_DIM)
            ls = slice(h * rows, (h + 1) * rows)
            parts.append(g_t[3 * h:3 * h + 1] * o_c[rs, ls] + g_t[3 * h + 1:3 * h + 2] * o_s[rs, ls]
                         + g_t[3 * h + 2:3 * h + 3] * o_w[rs, ls])
        out_ref[:, m * LANES:(m + 1) * LANES] = jnp.concatenate(parts, axis=0).T


def _nsa_prompt(q, gate, kcv, kvs_b, kvw_b, bsz, seq):
    n, d_b = bsz * seq, q.shape[1]
    kv_w = kvs_b.shape[1]
    nq = seq // Q_BLOCK
    n_cmp = seq // CMP_BLOCK
    rows = N_HEADS * Q_BLOCK
    return pl.pallas_call(
        functools.partial(_nsa_prompt_kernel, seq=seq),
        grid=(bsz, nq),
        in_specs=[pl.BlockSpec((Q_BLOCK, d_b), lambda b, i: (b * nq + i, 0)),
                  pl.BlockSpec((Q_BLOCK, LANES), lambda b, i: (b * nq + i, 0)),
                  pl.BlockSpec((n_cmp, kv_w), lambda b, i: (b, 0)),
                  pl.BlockSpec((seq, kv_w), lambda b, i: (b, 0)),
                  pl.BlockSpec((seq, kv_w), lambda b, i: (b, 0))],
        out_specs=pl.BlockSpec((Q_BLOCK, d_b), lambda b, i: (b * nq + i, 0)),
        out_shape=jax.ShapeDtypeStruct((n, d_b), F32),
        scratch_shapes=[pltpu.VMEM((seq, 2 * LANES), MXU_DTYPE),
                        pltpu.VMEM((seq // KEY_TILE, LANES, KEY_TILE), MXU_DTYPE),
                        pltpu.VMEM((seq // Q_BLOCK, LANES, Q_BLOCK), MXU_DTYPE),
                        pltpu.VMEM((LANES, n_cmp), MXU_DTYPE),
                        pltpu.VMEM((rows, 2 * LANES), MXU_DTYPE),
                        pltpu.VMEM((1, rows), F32), pltpu.VMEM((1, rows), F32), pltpu.VMEM((LANES, rows), F32)],
        compiler_params=_cparams("arbitrary", "arbitrary"),
        name="nsa_prompt",
    )(q, gate, kcv, kvs_b, kvw_b)


def _nsa_sample_kernel(pt_ref, q_ref, gate_ref, ksn_ref, kwn_ref, wcmp_ref, win_ref, *rest,
                       n_pages, page, past, t_len):
    cmp_pages = rest[:n_pages]
    sel_pages = rest[n_pages:2 * n_pages]
    out_ref, nwin_ref, kaug, vsel, kcv, lhs = rest[2 * n_pages:]
    rows = 8
    n_cmp = past // CMP_BLOCK
    n_sel = pl.cdiv(past + t_len, SEL_BLOCK)
    per_page = page // CMP_BLOCK

    @pl.when(pl.program_id(0) == 0)
    def _():
        kaug[:, LANES:2 * LANES] = _block_onehot(past)

    q = jnp.concatenate([q_ref[0], jnp.zeros((rows - t_len, q_ref.shape[2]), F32)], axis=0)
    _pad_heads(q, lhs, rows)
    qpad = lhs[:, 0:LANES]
    tq = lax.broadcasted_iota(jnp.int32, (rows, 1), 0)
    pos_q = past + tq

    wrep = jnp.concatenate([wcmp_ref[...]] * per_page, axis=0)
    for p in range(n_pages):
        x = cmp_pages[p][...] * wrep
        for j in range(per_page):
            blk = p * per_page + j
            dst = (blk % 2) * (n_cmp // 2) + blk // 2
            kcv[dst:dst + 1, :] = jnp.sum(x[j * CMP_BLOCK:(j + 1) * CMP_BLOCK], axis=0, keepdims=True)
        kaug[p * page:(p + 1) * page, 0:LANES] = sel_pages[p][:, 0:LANES].astype(MXU_DTYPE)
        vsel[p * page:(p + 1) * page, :] = sel_pages[p][:, LANES:2 * LANES].astype(MXU_DTYPE)

    o_c, sels = _compressed_branch(qpad, kcv[...], past, rows, n_sel)
    _store_selection(lhs, sels, rows)

    def new_rows(ref):
        kv = jnp.concatenate([ref[0], jnp.zeros((LANES - t_len, ref.shape[2]), F32)], axis=0)
        return kv[:, 0:LANES].astype(MXU_DTYPE), kv[:, LANES:2 * LANES].astype(MXU_DTYPE)

    tk_new = lax.broadcasted_iota(jnp.int32, (rows, LANES), 1)
    m_new = (tk_new < t_len) & (tk_new <= tq)

    def joint_attention(s_past, mask_past, v_past, s_new, v_new):
        outs = []
        p_past, p_new = [], []
        for h in range(N_HEADS):
            sl = slice(h * rows, (h + 1) * rows)
            sp = s_past[sl] if mask_past is None else jnp.where(mask_past, s_past[sl], NEG_INF)
            sn = jnp.where(m_new, s_new[sl], NEG_INF)
            m = jnp.maximum(jnp.max(sp, axis=-1, keepdims=True), jnp.max(sn, axis=-1, keepdims=True))
            pp = jnp.exp(sp - m) if mask_past is None else jnp.where(mask_past, jnp.exp(sp - m), 0.0)
            pn = jnp.where(m_new, jnp.exp(sn - m), 0.0)
            den = jnp.maximum(jnp.sum(pp, axis=-1, keepdims=True) + jnp.sum(pn, axis=-1, keepdims=True), TINY)
            p_past.append((pp / den).astype(MXU_DTYPE))
            p_new.append((pn / den).astype(MXU_DTYPE))
        return (jnp.dot(jnp.concatenate(p_past, axis=0), v_past, preferred_element_type=F32)
                + jnp.dot(jnp.concatenate(p_new, axis=0), v_new, preferred_element_type=F32))

    kn, vn = new_rows(ksn_ref)
    s_past = lax.dot_general(lhs[...], kaug[...], _NT, preferred_element_type=F32)
    s_new = lax.dot_general(qpad, kn, _NT, preferred_element_type=F32)
    o_s = joint_attention(s_past, None, vsel[...], s_new, vn)

    n_buf = win_ref.shape[0]
    kwn, vwn = new_rows(kwn_ref)
    kwb = win_ref[:, 0:LANES].astype(MXU_DTYPE)
    vwb = win_ref[:, LANES:2 * LANES].astype(MXU_DTYPE)
    s_wb = lax.dot_general(qpad, kwb, _NT, preferred_element_type=F32)
    s_wn = lax.dot_general(qpad, kwn, _NT, preferred_element_type=F32)
    pos_w = past - n_buf + lax.broadcasted_iota(jnp.int32, (rows, n_buf), 1)
    dlt = pos_q - pos_w
    m_wb = (dlt >= 0) & (dlt <= WINDOW) & (pos_w >= 0)
    o_w = joint_attention(s_wb, m_wb, vwb, s_wn, vwn)

    for m, slab in enumerate(_gated_output(
            jnp.concatenate([gate_ref[0], jnp.zeros((rows - t_len, LANES), F32)], axis=0), o_c, o_s, o_w, rows)):
        out_ref[0, :, m * LANES:(m + 1) * LANES] = slab[0:t_len]

    nwin_ref[0:n_buf - t_len, :] = win_ref[t_len:n_buf, :]
    nwin_ref[n_buf - t_len:n_buf, :] = kwn_ref[0]


def _nsa_sample(page_table, q, gate, kvs_new, kvw_new, w_cmp, pool_c, pool_s, win, layer):
    dbs, t_len, d_b = q.shape
    n_pages = page_table.shape[1]
    page, kv_w = pool_c.shape[2], pool_c.shape[3]
    n_buf = win.shape[2]
    past = n_pages * page
    tok = lambda w: pl.BlockSpec((1, t_len, w), lambda b, pt: (b, 0, 0))

    def page_spec(p):
        return pl.BlockSpec((None, None, page, kv_w), lambda b, pt, p=p: (layer, pt[b * n_pages + p], 0, 0))

    grid_spec = pltpu.PrefetchScalarGridSpec(
        num_scalar_prefetch=1,
        grid=(dbs,),
        in_specs=[tok(d_b), tok(LANES), tok(kv_w), tok(kv_w),
                  pl.BlockSpec(w_cmp.shape, lambda b, pt: (0, 0)),
                  pl.BlockSpec((None, None, n_buf, kv_w), lambda b, pt: (layer, b, 0, 0))]
                 + [page_spec(p) for p in range(n_pages)] * 2,
        out_specs=[tok(d_b), pl.BlockSpec((None, n_buf, kv_w), lambda b, pt: (b, 0, 0))],
        scratch_shapes=[pltpu.VMEM((past, 2 * LANES), MXU_DTYPE), pltpu.VMEM((past, LANES), MXU_DTYPE),
                        pltpu.VMEM((past // CMP_BLOCK, kv_w), F32), pltpu.VMEM((N_HEADS * 8, 2 * LANES), MXU_DTYPE)],
    )
    return pl.pallas_call(
        functools.partial(_nsa_sample_kernel, n_pages=n_pages, page=page, past=past, t_len=t_len),
        grid_spec=grid_spec,
        out_shape=[jax.ShapeDtypeStruct((dbs, t_len, d_b), F32), jax.ShapeDtypeStruct((dbs, n_buf, kv_w), F32)],
        compiler_params=_cparams("arbitrary"),
        name="nsa_sample",
    )(page_table.reshape(-1), q, gate, kvs_new, kvw_new, w_cmp, win, *([pool_c] * n_pages), *([pool_s] * n_pages))


def _out_proj_router_kernel(x_ref, oa_ref, ob_ref, oc_ref, oas_ref, obs_ref, ocs_ref, wo_ref, g2_ref, wr_ref, br_ref,
                            xn_ref, h2_ref, ri_ref, rw_ref, cnt_ref, run_ref, *, n_ptiles):
    tm = x_ref.shape[0]

    @pl.when(pl.program_id(0) == 0)
    def _():
        run_ref[...] = jnp.zeros(run_ref.shape, F32)

    is_sample = pl.program_id(0) >= n_ptiles
    mix = jnp.concatenate([jnp.where(is_sample, oas_ref[...], oa_ref[...]),
                           jnp.where(is_sample, obs_ref[...], ob_ref[...]),
                           jnp.where(is_sample, ocs_ref[...], oc_ref[...])], axis=1)
    xn = x_ref[...] + _split_dot(_split_act(mix, wo_ref.shape[0]), wo_ref)
    xn_ref[...] = xn
    h2 = _rmsnorm(xn, g2_ref[...])
    h2_ref[...] = h2
    logits = jnp.dot(h2, wr_ref[...], preferred_element_type=F32, precision=lax.Precision.HIGHEST) + br_ref[...]

    lane = lax.broadcasted_iota(jnp.int32, (tm, LANES), 1)
    lanef = lane.astype(F32)

    def softmax_over(mask):
        m = jnp.max(jnp.where(mask, logits, -jnp.inf), axis=-1, keepdims=True)
        e = jnp.where(mask, jnp.exp(logits - m), 0.0)
        return e / jnp.sum(e, axis=-1, keepdims=True)

    def first_max(p, mask):
        pm = jnp.max(jnp.where(mask, p, -1.0), axis=-1, keepdims=True)
        idx = jnp.min(jnp.where(mask & (p == pm), lanef, 1e9), axis=-1, keepdims=True)
        return pm, idx

    is_g = lane < N_GROUPS
    g_wt, g_sel = first_max(softmax_over(is_g), is_g)
    lo = N_GROUPS + EXP_PER_GROUP * g_sel
    in_e = (lanef >= lo) & (lanef < lo + EXP_PER_GROUP)
    p_e = softmax_over(in_e)
    p0, i0 = first_max(p_e, in_e)
    p1, i1 = first_max(p_e, in_e & (lanef != i0))
    den = p0 + p1
    e0 = i0 - N_GROUPS
    e1 = i1 - N_GROUPS

    hit0 = lanef == e0
    hit1 = lanef == e1
    onehot = jnp.where(hit0 | hit1, 1.0, 0.0)
    r_i = lax.broadcasted_iota(jnp.int32, (tm, tm), 0)
    c_i = lax.broadcasted_iota(jnp.int32, (tm, tm), 1)
    ltri = jnp.where(c_i < r_i, 1.0, 0.0).astype(jnp.bfloat16)
    before = jnp.dot(ltri, onehot.astype(jnp.bfloat16), preferred_element_type=F32) + run_ref[0:1, :]
    r0 = jnp.sum(jnp.where(hit0, before, 0.0), axis=-1, keepdims=True)
    r1 = jnp.sum(jnp.where(hit1, before, 0.0), axis=-1, keepdims=True)
    run_ref[...] = run_ref[...] + jnp.sum(onehot, axis=0, keepdims=True)
    cnt_ref[...] = run_ref[...]

    ri = jnp.where(lane == 0, e0, jnp.where(lane == 1, e1, jnp.where(lane == 2, r0, jnp.where(lane == 3, r1, 0.0))))
    ri_ref[...] = ri.astype(jnp.int32)
    rw_ref[...] = jnp.where(lane == 0, g_wt * p0 / den, jnp.where(lane == 1, g_wt * p1 / den, 0.0))


def _out_proj_router(x, mix_p, mix_s, w_out, g2, w_route, b_route):
    n, d = x.shape
    tm = TOKEN_TILE
    n_ptiles = mix_p[0].shape[0] // tm
    assert all(a.shape[0] == tm for a in mix_s) and n == (n_ptiles + 1) * tm
    row = lambda w: pl.BlockSpec((tm, w), lambda i: (i, 0))
    prow = lambda a: pl.BlockSpec((tm, a.shape[1]), lambda i: (jnp.minimum(i, n_ptiles - 1), 0))
    full = lambda a: pl.BlockSpec(a.shape, lambda i: (0,) * a.ndim)
    return pl.pallas_call(
        functools.partial(_out_proj_router_kernel, n_ptiles=n_ptiles),
        grid=(n // tm,),
        in_specs=[row(d)] + [prow(a) for a in mix_p] + [full(a) for a in mix_s]
                 + [full(w_out), full(g2), full(w_route), full(b_route)],
        out_specs=[row(d), row(d), row(LANES), row(LANES), pl.BlockSpec((8, LANES), lambda i: (0, 0))],
        out_shape=[jax.ShapeDtypeStruct((n, d), F32), jax.ShapeDtypeStruct((n, d), F32),
                   jax.ShapeDtypeStruct((n, LANES), jnp.int32), jax.ShapeDtypeStruct((n, LANES), F32),
                   jax.ShapeDtypeStruct((8, LANES), F32)],
        scratch_shapes=[pltpu.VMEM((8, LANES), F32)],
        compiler_params=_cparams("arbitrary"),
        name="out_proj_router",
    )(x, *mix_p, *mix_s, w_out, g2, w_route, b_route)


def _row_copy(src_hbm, row, dst, slot, r, sem):
    return pltpu.make_async_copy(src_hbm.at[pl.ds(row, 1)], dst.at[slot, pl.ds(r, 1)], sem.at[slot])


def _expert_kernel(te_ref, nv_ref, src_ref, h_hbm, wg_ref, wu_ref, wd_ref, ys_ref, xbuf, sem):
    t = pl.program_id(0)
    nv = nv_ref[0]
    te = xbuf.shape[1]

    def issue(tile, slot):
        def body(r, carry):
            _row_copy(h_hbm, src_ref[tile * te + r], xbuf, slot, r, sem).start()
            return carry
        lax.fori_loop(0, te, body, 0, unroll=8)

    def wait_all(slot):
        pltpu.make_async_copy(h_hbm.at[pl.ds(0, te)], xbuf.at[slot], sem.at[slot]).wait()

    @pl.when((t == 0) & (nv > 0))
    def _():
        issue(0, 0)

    @pl.when(t + 1 < nv)
    def _():
        issue(t + 1, (t + 1) % 2)

    @pl.when(t < nv)
    def _():
        slot = t % 2
        wait_all(slot)
        x = xbuf[slot].astype(MXU_DTYPE)
        hg = jnp.dot(x, wg_ref[...].astype(MXU_DTYPE), preferred_element_type=F32)
        hu = jnp.dot(x, wu_ref[...].astype(MXU_DTYPE), preferred_element_type=F32)
        act = (jax.nn.silu(hg) * hu).astype(MXU_DTYPE)
        ys_ref[...] = jnp.dot(act, wd_ref[...].astype(MXU_DTYPE), preferred_element_type=F32)

    @pl.when(t >= nv)
    def _():
        ys_ref[...] = jnp.zeros(ys_ref.shape, F32)


def _experts(tile_expert, n_valid, src_tok, h2, w_gate, w_up, w_down, layer):
    n_tiles = tile_expert.shape[0]
    te = EXPERT_TILE
    d = h2.shape[1]
    d_e = w_gate.shape[3]
    wspec = lambda r, c: pl.BlockSpec((None, None, r, c), lambda t, te_r, nv_r, src_r: (layer, te_r[t], 0, 0))
    grid_spec = pltpu.PrefetchScalarGridSpec(
        num_scalar_prefetch=3,
        grid=(n_tiles,),
        in_specs=[pl.BlockSpec(memory_space=pl.ANY), wspec(d, d_e), wspec(d, d_e), wspec(d_e, d)],
        out_specs=pl.BlockSpec((te, d), lambda t, te_r, nv_r, src_r: (t, 0)),
        scratch_shapes=[pltpu.VMEM((2, te, d), F32), pltpu.SemaphoreType.DMA((2,))],
    )
    return pl.pallas_call(
        _expert_kernel,
        grid_spec=grid_spec,
        out_shape=jax.ShapeDtypeStruct((n_tiles * te, d), F32),
        compiler_params=_cparams("arbitrary"),
        name="moe_experts",
    )(tile_expert, n_valid, src_tok, h2, w_gate, w_up, w_down)


def _combine_kernel(d0_ref, d1_ref, x_ref, rw_ref, gf_ref, ys_hbm, *rest, final):
    if final:
        xo_ref, y_ref, buf, sem = rest
    else:
        xo_ref, buf, sem = rest
    t = pl.program_id(0)
    nt = pl.num_programs(0)
    tm = x_ref.shape[0]

    def issue(tile, slot):
        def body(r, carry):
            _row_copy(ys_hbm, d0_ref[tile * tm + r], buf, slot, r, sem).start()
            _row_copy(ys_hbm, d1_ref[tile * tm + r], buf, slot, tm + r, sem).start()
            return carry
        lax.fori_loop(0, tm, body, 0, unroll=8)

    def wait_all(slot):
        pltpu.make_async_copy(ys_hbm.at[pl.ds(0, 2 * tm)], buf.at[slot], sem.at[slot]).wait()

    @pl.when(t == 0)
    def _():
        issue(0, 0)

    @pl.when(t + 1 < nt)
    def _():
        issue(t + 1, (t + 1) % 2)

    slot = t % 2
    wait_all(slot)
    rw = rw_ref[...]
    xo = x_ref[...] + rw[:, 0:1] * buf[slot, 0:tm, :] + rw[:, 1:2] * buf[slot, tm:2 * tm, :]
    xo_ref[...] = xo
    if final:
        y_ref[...] = _rmsnorm(xo, gf_ref[...])


def _combine(d0, d1, x, rw, gf, ys, final):
    n, d = x.shape
    tm = COMBINE_TILE
    row = lambda w: pl.BlockSpec((tm, w), lambda t, a, b: (t, 0))
    n_out = 2 if final else 1
    grid_spec = pltpu.PrefetchScalarGridSpec(
        num_scalar_prefetch=2,
        grid=(n // tm,),
        in_specs=[row(d), row(LANES), pl.BlockSpec((1, d), lambda t, a, b: (0, 0)), pl.BlockSpec(memory_space=pl.ANY)],
        out_specs=[row(d)] * n_out,
        scratch_shapes=[pltpu.VMEM((2, 2 * tm, d), F32), pltpu.SemaphoreType.DMA((2,))],
    )
    return pl.pallas_call(
        functools.partial(_combine_kernel, final=final),
        grid_spec=grid_spec,
        out_shape=[jax.ShapeDtypeStruct((n, d), F32)] * n_out,
        compiler_params=_cparams("arbitrary"),
        name="moe_combine_final" if final else "moe_combine",
    )(d0, d1, x, rw, gf, ys)


def _route_plan(ri, cnt, n_tiles):
    te = EXPERT_TILE
    n = ri.shape[0]
    counts = cnt[0, :N_EXPERTS].astype(jnp.int32)
    padded = ((counts + te - 1) // te) * te
    ends = jnp.cumsum(padded)
    offs = ends - padded
    d0 = offs[ri[:, 0]] + ri[:, 2]
    d1 = offs[ri[:, 1]] + ri[:, 3]
    tok = jnp.arange(n, dtype=jnp.int32)
    src = jnp.zeros((n_tiles * te,), jnp.int32).at[d0].set(tok).at[d1].set(tok)
    tile_start = jnp.arange(n_tiles, dtype=jnp.int32) * te
    tile_expert = jnp.minimum(jnp.sum((ends[None, :] <= tile_start[:, None]).astype(jnp.int32), axis=1),
                              N_EXPERTS - 1)
    n_valid = (ends[-1:] // te).astype(jnp.int32)
    return d0, d1, src, tile_expert, n_valid


def _rope_tables(pos):
    half = HEAD_DIM // 2
    inv = ROPE_THETA ** (-jnp.arange(half, dtype=F32) / half)
    ang = pos.astype(F32)[:, None] * inv[None, :]
    cos = jnp.cos(ang)
    sin = jnp.sin(ang)
    cos_t = jnp.concatenate([cos, cos] * (LANES // HEAD_DIM), axis=1)
    sin_t = jnp.concatenate([-sin, sin] * (LANES // HEAD_DIM), axis=1)
    return cos_t, sin_t


def _block_diag(w):
    nb, bw, _ = w.shape
    out = jnp.zeros((nb * bw, nb * bw), w.dtype)
    for i in range(nb):
        out = out.at[i * bw:(i + 1) * bw, i * bw:(i + 1) * bw].set(w[i])
    return out


def kernel(x_prompt, x_sample, cache_kv_cmp, cache_kv_sel, cache_kv_win, state_lru_h, state_lru_conv, state_sconv, page_table, norm1_g, w_in, lru_conv_w, lru_conv_b, lru_wa, lru_ba, lru_wx, lru_bx, lru_lambda, nsa_cmp_wk, nsa_cmp_wv, sc_conv_w, w_out, norm2_g, router_group_w, router_group_b, router_exp_w, router_exp_b, exp_w_gate, exp_w_up, exp_w_down, norm_f_g):
    bsz, seq, d = x_prompt.shape
    dbs, t_len, _ = x_sample.shape
    depth = w_in.shape[0]
    d_a = lru_conv_w.shape[2]
    d_c = sc_conv_w.shape[2]
    kv_w = 2 * N_KV * HEAD_DIM
    d_b = N_HEADS * HEAD_DIM
    n_gate = 3 * N_HEADS
    n_p = bsz * seq
    n_s = dbs * t_len
    n = n_p + n_s
    assert n_s == TOKEN_TILE and n_p % TOKEN_TILE == 0 and seq % TOKEN_TILE == 0
    page = cache_kv_cmp.shape[2]
    past = page_table.shape[1] * page
    n_buf = cache_kv_win.shape[2]

    pos = jnp.concatenate([jnp.arange(seq, dtype=jnp.int32),
                           jnp.tile(past + jnp.arange(t_len, dtype=jnp.int32), dbs)])
    cos_t, sin_t = _rope_tables(pos)

    pool_c = cache_kv_cmp.reshape(depth, -1, page, kv_w)
    pool_s = cache_kv_sel.reshape(depth, -1, page, kv_w)
    win = cache_kv_win.reshape(depth, dbs, n_buf, kv_w)

    g_off = 2 * d_a + d_b + 3 * kv_w
    x = jnp.concatenate([x_prompt.reshape(n_p, d), x_sample.reshape(n_s, d)], axis=0)
    h0_p = jnp.zeros((bsz, 1, d_a), F32)
    lb0_p = jnp.zeros((bsz, lru_conv_w.shape[1] - 1, d_a), F32)
    sb0_p = jnp.zeros((bsz, sc_conv_w.shape[1] - 1, d_c), F32)

    n_tiles = (2 * n) // EXPERT_TILE + N_EXPERTS
    proj_parts = lambda l: 2 if l < depth - 1 else 1
    states_p, states_s = [], []
    y = None
    for l in range(depth):
        w_l = w_in[l]
        w_r = _split_weight(jnp.concatenate([w_l[:, :g_off], w_l[:, g_off + n_gate:], w_l[:, g_off:g_off + n_gate],
                                             jnp.zeros((d, LANES - n_gate), F32)], axis=1), proj_parts(l))
        lru_in, sc_in, gate, q, kvc, kvs, kvw, kvs_b, kvw_b = _in_proj(
            x, norm1_g[l][None], w_r, cos_t, sin_t, n_p, seq, d_a=d_a, d_b=d_b, kv_w=kv_w, d_c=d_c)

        seq_w = (lru_conv_w[l], lru_conv_b[l][None], _block_diag(lru_wa[l]).astype(MXU_DTYPE), lru_ba[l][None],
                 _block_diag(lru_wx[l]).astype(MXU_DTYPE), lru_bx[l][None], lru_lambda[l][None], sc_conv_w[l])
        out_a, out_c, hn_p, lbn_p, sbn_p = _seq_mix_prompt(lru_in, sc_in, h0_p, lb0_p, sb0_p, seq_w, bsz, seq,
                                                           d_a=d_a, d_c=d_c)
        tmaj = lambda a: jnp.swapaxes(a.reshape(dbs, t_len, -1), 0, 1)
        oa_s, oc_s, hn_s, lbn_s, sbn_s = _seq_mix_sample(
            tmaj(lru_in[n_p:]), tmaj(sc_in[n_p:]), state_lru_h[l], jnp.swapaxes(state_lru_conv[l], 0, 1),
            jnp.swapaxes(state_sconv[l], 0, 1), seq_w, d_a=d_a, d_c=d_c)
        oa_s = jnp.swapaxes(oa_s, 0, 1).reshape(n_s, d_a)
        oc_s = jnp.swapaxes(oc_s, 0, 1).reshape(n_s, d_c)

        w_cmp = jnp.concatenate([jnp.broadcast_to(nsa_cmp_wk[l][:, None], (CMP_BLOCK, kv_w // 2)),
                                 jnp.broadcast_to(nsa_cmp_wv[l][:, None], (CMP_BLOCK, kv_w // 2))], axis=1)
        kcv = _compress_prompt(kvc, w_cmp, bsz, seq)
        out_b = _nsa_prompt(q, gate, kcv, kvs_b, kvw_b, bsz, seq)
        s3 = lambda a: a[n_p:].reshape(dbs, t_len, -1)
        ob_s, nwin_s = _nsa_sample(page_table, s3(q), s3(gate), s3(kvs), s3(kvw), w_cmp, pool_c, pool_s, win, l)
        ob_s = ob_s.reshape(n_s, d_b)

        w_route = jnp.concatenate([router_group_w[l], router_exp_w[l],
                                   jnp.zeros((d, LANES - N_GROUPS - N_EXPERTS), F32)], axis=1)
        b_route = jnp.concatenate([router_group_b[l], router_exp_b[l],
                                   jnp.zeros((LANES - N_GROUPS - N_EXPERTS,), F32)])[None]
        xn, h2, ri, rw, cnt = _out_proj_router(x, (out_a, out_b, out_c), (oa_s, ob_s, oc_s),
                                               _split_weight(w_out[l], proj_parts(l)), norm2_g[l][None],
                                               w_route, b_route)
        d0, d1, src, tile_expert, n_valid = _route_plan(ri, cnt, n_tiles)
        ys = _experts(tile_expert, n_valid, src, h2, exp_w_gate, exp_w_up, exp_w_down, l)
        final = l == depth - 1
        outs = _combine(d0, d1, xn, rw, norm_f_g[None], ys, final)
        x = outs[0]
        if final:
            y = outs[1]

        kv6 = lambda a, lead: a.reshape(lead + (2, N_KV, HEAD_DIM))
        states_p.append((kv6(kvc[:n_p], (bsz, seq)), kv6(kvs[:n_p], (bsz, seq)),
                         kv6(kvw[:n_p].reshape(bsz, seq, kv_w)[:, seq - min(WINDOW, seq):], (bsz, min(WINDOW, seq))),
                         hn_p[:, 0], lbn_p, sbn_p))
        states_s.append((kv6(kvc[n_p:], (dbs, t_len)), kv6(kvs[n_p:], (dbs, t_len)), kv6(nwin_s, (dbs, n_buf)),
                         hn_s, jnp.swapaxes(lbn_s, 0, 1), jnp.swapaxes(sbn_s, 0, 1)))

    stack = lambda sts, i: jnp.stack([s[i] for s in sts])
    res = [y[:n_p].reshape(bsz, seq, d), y[n_p:].reshape(dbs, t_len, d)]
    for i in range(6):
        res += [stack(states_p, i), stack(states_s, i)]
    return tuple(res)
```

```python
import functools

import jax
import jax.numpy as jnp
from jax import lax
from jax.experimental import pallas as pl
from jax.experimental.pallas import tpu as pltpu

F32 = jnp.float32
MXU_DTYPE = jnp.bfloat16

HEAD_DIM = 64
N_HEADS = 8
N_KV = 2
GROUP = N_HEADS // N_KV
CMP_BLOCK = 32
SEL_BLOCK = 64
TOP_N = 16
WINDOW = 512
Q_BLOCK = 128
ROPE_THETA = 10000.0
LRU_C = 8.0
N_GROUPS = 4
EXP_PER_GROUP = 8
N_EXPERTS = N_GROUPS * EXP_PER_GROUP
RMS_EPS = 1e-6
NEG_INF = -1e30
TINY = 1e-30
FORCE = 1e6

LANES = 128
VMEM_LIMIT = 56 * 2 ** 20
TOKEN_TILE = 512
TIME_CHUNK = 256
KEY_TILE = 512
EXPERT_TILE = 256
COMBINE_TILE = 256

_NT = (((1,), (1,)), ((), ()))


def _cparams(*sem):
    return pltpu.CompilerParams(dimension_semantics=sem, vmem_limit_bytes=VMEM_LIMIT)


def _split_weight(w, parts):
    hi = w.astype(MXU_DTYPE)
    if parts == 1:
        return hi[None]
    return jnp.stack([hi, (w - hi.astype(F32)).astype(MXU_DTYPE)])


def _split_act(a, parts):
    hi = a.astype(MXU_DTYPE)
    return (hi,) if parts == 1 else (hi, (a - hi.astype(F32)).astype(MXU_DTYPE))


def _split_dot(a_parts, w_ref, cols=slice(None)):
    dot = lambda a, p: jnp.dot(a, w_ref[p, :, cols], preferred_element_type=F32)
    if len(a_parts) == 1:
        return dot(a_parts[0], 0)
    return (dot(a_parts[0], 1) + dot(a_parts[1], 0)) + dot(a_parts[0], 0)


def _rmsnorm(x, g):
    return x * lax.rsqrt(jnp.mean(x * x, axis=-1, keepdims=True) + RMS_EPS) * g


def _swap_halves(x):
    w = x.shape[-1]
    lane = lax.broadcasted_iota(jnp.int32, x.shape, x.ndim - 1)
    first = (lane % HEAD_DIM) < HEAD_DIM // 2
    return jnp.where(first, pltpu.roll(x, w - HEAD_DIM // 2, axis=1), pltpu.roll(x, HEAD_DIM // 2, axis=1))


def _masked_softmax(s, mask):
    s = jnp.where(mask, s, NEG_INF)
    m = jnp.max(s, axis=-1, keepdims=True)
    p = jnp.where(mask, jnp.exp(s - m), 0.0)
    return p / jnp.maximum(jnp.sum(p, axis=-1, keepdims=True), TINY)


def _select_blocks_keys(imp, pos_q, n_sel):
    blk = lax.broadcasted_iota(jnp.int32, imp.shape, 0)
    cur = pos_q // SEL_BLOCK
    forced = (blk == 0) | (blk == cur) | (blk == cur - 1)
    w = jnp.where(forced, FORCE, imp)
    w = jnp.where(blk <= cur, w, NEG_INF)
    w = jnp.where(blk < n_sel, w, -jnp.inf)
    blkf = blk.astype(F32)

    def body(_, carry):
        w, sel = carry
        m = jnp.max(w, axis=0, keepdims=True)
        first = jnp.min(jnp.where(w == m, blkf, 1e9), axis=0, keepdims=True)
        pick = blkf == first
        return jnp.where(pick, -jnp.inf, w), jnp.where(pick, 1.0, sel)

    _, sel = lax.fori_loop(0, min(TOP_N, n_sel), body, (w, jnp.zeros_like(w)))
    return sel


def _in_proj_kernel(x_ref, g_ref, w_ref, cos_ref, sin_ref,
                    lru_ref, sc_ref, gate_ref, q_ref, kvc_ref, kvs_ref, kvw_ref, kvsb_ref, kvwb_ref,
                    *, d_a, d_b, kv_w, d_c):
    h_parts = _split_act(_rmsnorm(x_ref[...], g_ref[...]), w_ref.shape[0])

    def mm(a, b):
        return _split_dot(h_parts, w_ref, slice(a, b))

    cos = cos_ref[...]
    sin = sin_ref[...]
    off = 2 * d_a
    lru_ref[...] = mm(0, off)
    q = mm(off, off + d_b)
    reps = d_b // LANES
    cos_q = jnp.concatenate([cos] * reps, axis=1)
    sin_q = jnp.concatenate([sin] * reps, axis=1)
    q_ref[...] = (q * cos_q + _swap_halves(q) * sin_q) * (HEAD_DIM ** -0.5)
    off += d_b
    for ref, bref in ((kvc_ref, None), (kvs_ref, kvsb_ref), (kvw_ref, kvwb_ref)):
        kv = mm(off, off + kv_w)
        k = kv[:, :kv_w // 2]
        kv = jnp.concatenate([k * cos + _swap_halves(k) * sin, kv[:, kv_w // 2:]], axis=1)
        ref[...] = kv
        if bref is not None:
            bref[...] = kv.astype(MXU_DTYPE)
        off += kv_w
    sc_ref[...] = mm(off, off + 3 * d_c)
    off += 3 * d_c
    gate_ref[...] = jax.nn.sigmoid(mm(off, off + LANES))


def _in_proj(x, g1, w_r, cos_t, sin_t, n_prompt, seq, *, d_a, d_b, kv_w, d_c):
    n, d = x.shape
    tm = TOKEN_TILE
    n_ptiles = n_prompt // tm
    n_stiles = seq // tm

    def tab_map(i):
        return (jnp.where(i < n_ptiles, i % n_stiles, n_stiles), 0)

    row = lambda w: pl.BlockSpec((tm, w), lambda i: (i, 0))
    widths = (2 * d_a, 3 * d_c, LANES, d_b, kv_w, kv_w, kv_w, kv_w, kv_w)
    dtypes = (F32, F32, F32, F32, F32, F32, F32, MXU_DTYPE, MXU_DTYPE)
    return pl.pallas_call(
        functools.partial(_in_proj_kernel, d_a=d_a, d_b=d_b, kv_w=kv_w, d_c=d_c),
        grid=(n // tm,),
        in_specs=[row(d),
                  pl.BlockSpec((1, d), lambda i: (0, 0)),
                  pl.BlockSpec(w_r.shape, lambda i: (0, 0, 0)),
                  pl.BlockSpec((tm, LANES), tab_map),
                  pl.BlockSpec((tm, LANES), tab_map)],
        out_specs=[row(w) for w in widths],
        out_shape=[jax.ShapeDtypeStruct((n, w), dt) for w, dt in zip(widths, dtypes)],
        compiler_params=_cparams("parallel"),
        name="in_proj",
    )(x, g1, w_r, cos_t, sin_t)


def _scan_rows(a, u):
    t = a.shape[0]
    row = lax.broadcasted_iota(jnp.int32, a.shape, 0)
    d = 1
    while d < t:
        keep = row >= d
        a_sh = jnp.where(keep, pltpu.roll(a, d, axis=0), 1.0)
        u_sh = jnp.where(keep, pltpu.roll(u, d, axis=0), 0.0)
        u = a * u_sh + u
        a = a * a_sh
        d *= 2
    return a, u


def _lru_gates(xc, wa_ref, ba_ref, wx_ref, bx_ref, lam_ref):
    xb = xc.astype(MXU_DTYPE)
    r = jax.nn.sigmoid(jnp.dot(xb, wa_ref[...], preferred_element_type=F32) + ba_ref[...])
    i = jax.nn.sigmoid(jnp.dot(xb, wx_ref[...], preferred_element_type=F32) + bx_ref[...])
    log_a = -LRU_C * r * jax.nn.softplus(-lam_ref[...])
    a = jnp.exp(log_a)
    th = jnp.tanh(log_a)
    u = jnp.sqrt(-2.0 * th / (1.0 - th)) * (i * xc)
    return a, u


def _seq_mix_kernel(lru_ref, sc_ref, h0_ref, lb0_ref, sb0_ref, cw_ref, cb_ref, wa_ref, ba_ref, wx_ref, bx_ref,
                    lam_ref, scw_ref, oa_ref, oc_ref, hn_ref, lbn_ref, sbn_ref, xpad, vpad, hcar, *, d_a, d_c):
    c = pl.program_id(1)
    tc = lru_ref.shape[0]
    nlb = lb0_ref.shape[1]
    nsb = sb0_ref.shape[1]

    @pl.when(c == 0)
    def _():
        xpad[0:8, :] = jnp.zeros((8, d_a), F32)
        xpad[8 - nlb:8, :] = lb0_ref[0]
        vpad[0:8, :] = jnp.zeros((8, d_c), F32)
        vpad[8 - nsb:8, :] = sb0_ref[0]
        hcar[...] = h0_ref[0]

    xa = lru_ref[:, 0:d_a]
    ga = lru_ref[:, d_a:2 * d_a]
    xpad[8:8 + tc, :] = xa
    xc = cw_ref[0:1, :] * xpad[8 - nlb:8 - nlb + tc, :]
    for j in range(1, nlb):
        xc = xc + cw_ref[j:j + 1, :] * xpad[8 - nlb + j:8 - nlb + j + tc, :]
    xc = xc + cw_ref[nlb:nlb + 1, :] * xa + cb_ref[...]
    a, u = _lru_gates(xc, wa_ref, ba_ref, wx_ref, bx_ref, lam_ref)
    a_cum, hs = _scan_rows(a, u)
    hs = hs + a_cum * hcar[...]
    hcar[...] = hs[tc - 1:tc, :]
    oa_ref[...] = hs * jax.nn.gelu(ga)

    v = sc_ref[:, 0:d_c] * sc_ref[:, 2 * d_c:3 * d_c]
    vpad[8:8 + tc, :] = v
    uc = scw_ref[0:1, :] * vpad[8 - nsb:8 - nsb + tc, :]
    for j in range(1, nsb):
        uc = uc + scw_ref[j:j + 1, :] * vpad[8 - nsb + j:8 - nsb + j + tc, :]
    uc = uc + scw_ref[nsb:nsb + 1, :] * v
    oc_ref[...] = sc_ref[:, d_c:2 * d_c] * uc

    hn_ref[0] = hs[tc - 1:tc, :]
    lbn_ref[0] = xpad[8 + tc - nlb:8 + tc, :]
    sbn_ref[0] = vpad[8 + tc - nsb:8 + tc, :]
    xpad[0:8, :] = xpad[tc:tc + 8, :]
    vpad[0:8, :] = vpad[tc:tc + 8, :]


def _seq_mix_prompt(lru_in, sc_in, h0, lb0, sb0, wts, bsz, seq, *, d_a, d_c):
    n = bsz * seq
    tc = TIME_CHUNK
    nch = seq // tc
    row = lambda w: pl.BlockSpec((tc, w), lambda b, c: (b * nch + c, 0))
    full = lambda a: pl.BlockSpec(a.shape, lambda b, c: (0,) * a.ndim)
    state = lambda r, w: pl.BlockSpec((1, r, w), lambda b, c: (b, 0, 0))
    nlb, nsb = lb0.shape[1], sb0.shape[1]
    return pl.pallas_call(
        functools.partial(_seq_mix_kernel, d_a=d_a, d_c=d_c),
        grid=(bsz, nch),
        in_specs=[row(2 * d_a), row(3 * d_c), state(1, d_a), state(nlb, d_a), state(nsb, d_c)]
                 + [full(w) for w in wts],
        out_specs=[row(d_a), row(d_c), state(1, d_a), state(nlb, d_a), state(nsb, d_c)],
        out_shape=[jax.ShapeDtypeStruct((n, d_a), F32), jax.ShapeDtypeStruct((n, d_c), F32),
                   jax.ShapeDtypeStruct((bsz, 1, d_a), F32), jax.ShapeDtypeStruct((bsz, nlb, d_a), F32),
                   jax.ShapeDtypeStruct((bsz, nsb, d_c), F32)],
        scratch_shapes=[pltpu.VMEM((tc + 8, d_a), F32), pltpu.VMEM((tc + 8, d_c), F32), pltpu.VMEM((1, d_a), F32)],
        compiler_params=_cparams("arbitrary", "arbitrary"),
        name="seq_mix_prompt",
    )(lru_in, sc_in, h0, lb0, sb0, *wts)


def _seq_mix_sample_kernel(lru_ref, sc_ref, h0_ref, lb0_ref, sb0_ref, cw_ref, cb_ref, wa_ref, ba_ref, wx_ref,
                           bx_ref, lam_ref, scw_ref, oa_ref, oc_ref, hn_ref, lbn_ref, sbn_ref, *, d_a, d_c):
    t_len = lru_ref.shape[0]
    nlb = lb0_ref.shape[0]
    nsb = sb0_ref.shape[0]
    xs = [lb0_ref[j] for j in range(nlb)] + [lru_ref[t][:, 0:d_a] for t in range(t_len)]
    vs = [sb0_ref[j] for j in range(nsb)] + [sc_ref[t][:, 0:d_c] * sc_ref[t][:, 2 * d_c:3 * d_c] for t in range(t_len)]
    h = h0_ref[...]
    for t in range(t_len):
        xc = cw_ref[0:1, :] * xs[t]
        for j in range(1, nlb + 1):
            xc = xc + cw_ref[j:j + 1, :] * xs[t + j]
        xc = xc + cb_ref[...]
        a, u = _lru_gates(xc, wa_ref, ba_ref, wx_ref, bx_ref, lam_ref)
        h = a * h + u
        oa_ref[t] = h * jax.nn.gelu(lru_ref[t][:, d_a:2 * d_a])
        uc = scw_ref[0:1, :] * vs[t]
        for j in range(1, nsb + 1):
            uc = uc + scw_ref[j:j + 1, :] * vs[t + j]
        oc_ref[t] = sc_ref[t][:, d_c:2 * d_c] * uc
    hn_ref[...] = h
    for j in range(nlb):
        lbn_ref[j] = xs[t_len + j]
    for j in range(nsb):
        sbn_ref[j] = vs[t_len + j]


def _seq_mix_sample(lru_t, sc_t, h0, lb0_t, sb0_t, wts, *, d_a, d_c):
    t_len, bsz, _ = lru_t.shape
    outs = [jax.ShapeDtypeStruct((t_len, bsz, d_a), F32), jax.ShapeDtypeStruct((t_len, bsz, d_c), F32),
            jax.ShapeDtypeStruct(h0.shape, F32), jax.ShapeDtypeStruct(lb0_t.shape, F32),
            jax.ShapeDtypeStruct(sb0_t.shape, F32)]
    return pl.pallas_call(
        functools.partial(_seq_mix_sample_kernel, d_a=d_a, d_c=d_c),
        out_shape=outs,
        compiler_params=pltpu.CompilerParams(vmem_limit_bytes=VMEM_LIMIT),
        name="seq_mix_sample",
    )(lru_t, sc_t, h0, lb0_t, sb0_t, *wts)


def _compress_kernel(kvc_ref, w_ref, out_ref):
    half = out_ref.shape[0] // 2
    x = kvc_ref[...].reshape(half, 2 * CMP_BLOCK, out_ref.shape[1])
    w = w_ref[...][None]
    out_ref[0:half, :] = jnp.sum(x[:, 0:CMP_BLOCK, :] * w, axis=1)
    out_ref[half:2 * half, :] = jnp.sum(x[:, CMP_BLOCK:2 * CMP_BLOCK, :] * w, axis=1)


def _compress_prompt(kvc, w_cmp, bsz, seq):
    kv_w = kvc.shape[1]
    n_cmp = seq // CMP_BLOCK
    return pl.pallas_call(
        _compress_kernel,
        grid=(bsz,),
        in_specs=[pl.BlockSpec((seq, kv_w), lambda b: (b, 0)), pl.BlockSpec(w_cmp.shape, lambda b: (0, 0))],
        out_specs=pl.BlockSpec((n_cmp, kv_w), lambda b: (b, 0)),
        out_shape=jax.ShapeDtypeStruct((bsz * n_cmp, kv_w), F32),
        compiler_params=_cparams("parallel"),
        name="nsa_compress",
    )(kvc, w_cmp)


def _pad_heads(q, lhs_ref, rows):
    lane = lax.broadcasted_iota(jnp.int32, (rows, LANES), 1)
    for h in range(N_HEADS):
        k = h // GROUP
        slab = q[:, (h // 2) * LANES:(h // 2 + 1) * LANES]
        if h % 2 != k:
            slab = pltpu.roll(slab, HEAD_DIM, axis=1)
        keep = (lane >= k * HEAD_DIM) & (lane < (k + 1) * HEAD_DIM)
        lhs_ref[h * rows:(h + 1) * rows, 0:LANES] = jnp.where(keep, slab, 0.0).astype(MXU_DTYPE)


def _gated_output(gate, o_c, o_s, o_w, rows):
    lane = lax.broadcasted_iota(jnp.int32, (rows, LANES), 1)
    slabs = []
    for m in range(N_HEADS // 2):
        parts = []
        for h in (2 * m, 2 * m + 1):
            k = h // GROUP
            sl = slice(h * rows, (h + 1) * rows)
            r = (gate[:, 3 * h:3 * h + 1] * o_c[sl] + gate[:, 3 * h + 1:3 * h + 2] * o_s[sl]
                 + gate[:, 3 * h + 2:3 * h + 3] * o_w[sl])
            if h % 2 != k:
                r = pltpu.roll(r, HEAD_DIM, axis=1)
            parts.append(r)
        slabs.append(jnp.where(lane < HEAD_DIM, parts[0], parts[1]))
    return slabs


def _compressed_branch(qpad, kcv, pos0, rows, n_sel):
    n_cmp = kcv.shape[0]
    half = n_cmp // 2
    pos_q = pos0 + lax.broadcasted_iota(jnp.int32, (rows, 1), 0)
    kc = kcv[:, 0:LANES].astype(MXU_DTYPE)
    vc = kcv[:, LANES:2 * LANES].astype(MXU_DTYPE)
    s_c = lax.dot_general(qpad, kc, _NT, preferred_element_type=F32)
    col = lax.broadcasted_iota(jnp.int32, (rows, n_cmp), 1)
    blk = jnp.where(col < half, 2 * col, 2 * (col - half) + 1)
    m_c = (blk + 1) * CMP_BLOCK - 1 <= pos_q
    ps = []
    imp = [jnp.zeros((rows, n_cmp), F32) for _ in range(N_KV)]
    for h in range(N_HEADS):
        p = _masked_softmax(s_c[h * rows:(h + 1) * rows], m_c)
        imp[h // GROUP] = imp[h // GROUP] + p
        ps.append(p.astype(MXU_DTYPE))
    o_c = jnp.dot(jnp.concatenate(ps, axis=0), vc, preferred_element_type=F32)
    imps = []
    for k in range(N_KV):
        imp_s = imp[k][:, 0:half] + imp[k][:, half:n_cmp]
        if half < LANES:
            imp_s = jnp.concatenate([imp_s, jnp.zeros((rows, LANES - half), F32)], axis=1)
        imps.append(imp_s)
    imp_t = jnp.concatenate(imps + [jnp.zeros((LANES - N_KV * rows, LANES), F32)], axis=0).T
    n_rows = -(-n_sel // 8) * 8
    pos_l = pos0 + lax.broadcasted_iota(jnp.int32, (1, LANES), 1) % rows
    sel_t = _select_blocks_keys(imp_t[0:n_rows], pos_l, n_sel)
    sel = jnp.concatenate([sel_t, jnp.zeros((LANES - n_rows, LANES), F32)], axis=0).T
    return o_c, [sel[k * rows:(k + 1) * rows] for k in range(N_KV)]


def _store_selection(lhs_ref, sels, rows):
    for k in range(N_KV):
        neg = jnp.where(sels[k] > 0.0, 0.0, NEG_INF).astype(MXU_DTYPE)
        for g in range(GROUP):
            h = k * GROUP + g
            lhs_ref[h * rows:(h + 1) * rows, LANES:2 * LANES] = neg


def _block_onehot(n_rows):
    row = lax.broadcasted_iota(jnp.int32, (n_rows, LANES), 0)
    lane = lax.broadcasted_iota(jnp.int32, (n_rows, LANES), 1)
    return jnp.where(row // SEL_BLOCK == lane, 1.0, 0.0).astype(MXU_DTYPE)


def _masked_softmax_keys(s, mask):
    s = jnp.where(mask, s, NEG_INF)
    m = jnp.max(s, axis=0, keepdims=True)
    p = jnp.where(mask, jnp.exp(s - m), 0.0)
    return p / jnp.maximum(jnp.sum(p, axis=0, keepdims=True), TINY)


def _nsa_prompt_kernel(q_ref, gate_ref, kcv_ref, kvs_ref, kvw_ref, out_ref,
                       kaug, vst, vwt, vct, lhs, m_sc, l_sc, acc_sc, s_a, s_b, *, seq):
    qb = pl.program_id(1)
    rows = Q_BLOCK
    tk = KEY_TILE
    s0 = qb * rows
    n_sel = seq // SEL_BLOCK
    n_cmp = seq // CMP_BLOCK
    cols = N_HEADS * rows

    @pl.when(qb == 0)
    def _():
        kaug[:, 0:LANES] = kvs_ref[:, 0:LANES]
        kaug[:, LANES:2 * LANES] = _block_onehot(seq)
        vct[...] = kcv_ref[:, LANES:2 * LANES].T.astype(MXU_DTYPE)

        def tr_sel(i, c):
            r0 = pl.multiple_of(i * tk, tk)
            vst[i] = kvs_ref[pl.ds(r0, tk), LANES:2 * LANES].astype(F32).T.astype(MXU_DTYPE)
            return c

        def tr_win(i, c):
            r0 = pl.multiple_of(i * rows, rows)
            vwt[i] = kvw_ref[pl.ds(r0, rows), LANES:2 * LANES].astype(F32).T.astype(MXU_DTYPE)
            return c

        lax.fori_loop(0, seq // tk, tr_sel, 0)
        lax.fori_loop(0, seq // rows, tr_win, 0)

    _pad_heads(q_ref[...], lhs, rows)
    qpad = lhs[:, 0:LANES]
    lane = lax.broadcasted_iota(jnp.int32, (1, cols), 1)
    pos_q = s0 + lane % rows

    half = n_cmp // 2
    s_c = lax.dot_general(kcv_ref[:, 0:LANES].astype(MXU_DTYPE), qpad, _NT, preferred_element_type=F32)
    r_c = lax.broadcasted_iota(jnp.int32, (n_cmp, cols), 0)
    blk_c = jnp.where(r_c < half, 2 * r_c, 2 * (r_c - half) + 1)
    p_c = _masked_softmax_keys(s_c, (blk_c + 1) * CMP_BLOCK - 1 <= pos_q)
    o_c = jnp.dot(vct[...], p_c.astype(MXU_DTYPE), preferred_element_type=F32)

    imp = []
    for k in range(N_KV):
        acc = p_c[:, k * GROUP * rows:(k * GROUP + 1) * rows]
        for g in range(1, GROUP):
            acc = acc + p_c[:, (k * GROUP + g) * rows:(k * GROUP + g + 1) * rows]
        imp.append(acc[0:half] + acc[half:n_cmp])
    sel = _select_blocks_keys(jnp.concatenate(imp, axis=1), pos_q[:, 0:N_KV * rows], n_sel)
    for k in range(N_KV):
        sel_k = sel[:, k * rows:(k + 1) * rows]
        if n_sel < LANES:
            sel_k = jnp.concatenate([sel_k, jnp.zeros((LANES - n_sel, rows), F32)], axis=0)
        neg = jnp.where(sel_k.T > 0.0, 0.0, NEG_INF).astype(MXU_DTYPE)
        for g in range(GROUP):
            h = k * GROUP + g
            lhs[h * rows:(h + 1) * rows, LANES:2 * LANES] = neg

    m_sc[...] = jnp.full(m_sc.shape, NEG_INF, F32)
    l_sc[...] = jnp.zeros(l_sc.shape, F32)
    acc_sc[...] = jnp.zeros(acc_sc.shape, F32)

    last_tile = seq // tk - 1

    def scores(s_ref, kt):
        k0 = pl.multiple_of(jnp.minimum(kt, last_tile) * tk, tk)
        s_ref[...] = lax.dot_general(kaug[pl.ds(k0, tk), :], lhs[...], _NT, preferred_element_type=F32)

    def update(s_ref, kt, causal):
        s = s_ref[...]
        if causal:
            s = jnp.where(kt * tk + lax.broadcasted_iota(jnp.int32, (tk, cols), 0) <= pos_q, s, NEG_INF)
        m_old = m_sc[...]
        m_new = jnp.maximum(m_old, jnp.max(s, axis=0, keepdims=True))
        alpha = jnp.exp(m_old - m_new)
        p = jnp.exp(s - m_new)
        l_sc[...] = alpha * l_sc[...] + jnp.sum(p, axis=0, keepdims=True)
        m_sc[...] = m_new
        vt = vst[jnp.minimum(kt, last_tile)]
        acc_sc[...] = alpha * acc_sc[...] + jnp.dot(vt, p.astype(MXU_DTYPE), preferred_element_type=F32)

    n_pairs = (s0 // tk) // 2
    scores(s_a, 0)

    def pair_body(i, carry):
        scores(s_b, 2 * i + 1)
        update(s_a, 2 * i, False)
        scores(s_a, 2 * i + 2)
        update(s_b, 2 * i + 1, False)
        return carry

    lax.fori_loop(0, n_pairs, pair_body, 0)
    t0 = 2 * n_pairs
    scores(s_b, t0 + 1)
    update(s_a, t0, True)
    update(s_b, t0 + 1, True)
    o_s = acc_sc[...] / jnp.maximum(l_sc[...], TINY)

    band = WINDOW + rows
    start = pl.multiple_of(jnp.maximum(s0 - WINDOW, 0), rows)
    s_w = lax.dot_general(kvw_ref[pl.ds(start, band), 0:LANES], qpad, _NT, preferred_element_type=F32)
    dlt = pos_q - (start + lax.broadcasted_iota(jnp.int32, (band, cols), 0))
    p_w = _masked_softmax_keys(s_w, (dlt >= 0) & (dlt <= WINDOW))
    t0 = start // rows
    vw = jnp.concatenate([vwt[t0 + j] for j in range(band // rows)], axis=1)
    o_w = jnp.dot(vw, p_w.astype(MXU_DTYPE), preferred_element_type=F32)

    g_t = gate_ref[...].T
    for m in range(N_HEADS // 2):
        parts = []
        for h in (2 * m, 2 * m + 1):
            k = h // GROUP
            rs = slice(k * HEAD_DIM, (k + 1) * HEAD_DIM)
            ls = slice(h * rows, (h + 1) * rows)
            parts.append(g_t[3 * h:3 * h + 1] * o_c[rs, ls] + g_t[3 * h + 1:3 * h + 2] * o_s[rs, ls]
                         + g_t[3 * h + 2:3 * h + 3] * o_w[rs, ls])
        out_ref[:, m * LANES:(m + 1) * LANES] = jnp.concatenate(parts, axis=0).T


def _nsa_prompt(q, gate, kcv, kvs_b, kvw_b, bsz, seq):
    n, d_b = bsz * seq, q.shape[1]
    kv_w = kvs_b.shape[1]
    nq = seq // Q_BLOCK
    n_cmp = seq // CMP_BLOCK
    rows = N_HEADS * Q_BLOCK
    return pl.pallas_call(
        functools.partial(_nsa_prompt_kernel, seq=seq),
        grid=(bsz, nq),
        in_specs=[pl.BlockSpec((Q_BLOCK, d_b), lambda b, i: (b * nq + i, 0)),
                  pl.BlockSpec((Q_BLOCK, LANES), lambda b, i: (b * nq + i, 0)),
                  pl.BlockSpec((n_cmp, kv_w), lambda b, i: (b, 0)),
                  pl.BlockSpec((seq, kv_w), lambda b, i: (b, 0)),
                  pl.BlockSpec((seq, kv_w), lambda b, i: (b, 0))],
        out_specs=pl.BlockSpec((Q_BLOCK, d_b), lambda b, i: (b * nq + i, 0)),
        out_shape=jax.ShapeDtypeStruct((n, d_b), F32),
        scratch_shapes=[pltpu.VMEM((seq, 2 * LANES), MXU_DTYPE),
                        pltpu.VMEM((seq // KEY_TILE, LANES, KEY_TILE), MXU_DTYPE),
                        pltpu.VMEM((seq // Q_BLOCK, LANES, Q_BLOCK), MXU_DTYPE),
                        pltpu.VMEM((LANES, n_cmp), MXU_DTYPE),
                        pltpu.VMEM((rows, 2 * LANES), MXU_DTYPE),
                        pltpu.VMEM((1, rows), F32), pltpu.VMEM((1, rows), F32), pltpu.VMEM((LANES, rows), F32),
                        pltpu.VMEM((KEY_TILE, rows), F32), pltpu.VMEM((KEY_TILE, rows), F32)],
        compiler_params=_cparams("arbitrary", "arbitrary"),
        name="nsa_prompt",
    )(q, gate, kcv, kvs_b, kvw_b)


def _nsa_sample_kernel(pt_ref, q_ref, gate_ref, ksn_ref, kwn_ref, wcmp_ref, win_ref, *rest,
                       n_pages, page, past, t_len):
    cmp_pages = rest[:n_pages]
    sel_pages = rest[n_pages:2 * n_pages]
    out_ref, nwin_ref, kaug, vsel, kcv, lhs = rest[2 * n_pages:]
    rows = 8
    n_cmp = past // CMP_BLOCK
    n_sel = pl.cdiv(past + t_len, SEL_BLOCK)
    per_page = page // CMP_BLOCK

    @pl.when(pl.program_id(0) == 0)
    def _():
        kaug[:, LANES:2 * LANES] = _block_onehot(past)

    q = jnp.concatenate([q_ref[0], jnp.zeros((rows - t_len, q_ref.shape[2]), F32)], axis=0)
    _pad_heads(q, lhs, rows)
    qpad = lhs[:, 0:LANES]
    tq = lax.broadcasted_iota(jnp.int32, (rows, 1), 0)
    pos_q = past + tq

    wrep = jnp.concatenate([wcmp_ref[...]] * per_page, axis=0)
    for p in range(n_pages):
        x = cmp_pages[p][...] * wrep
        for j in range(per_page):
            blk = p * per_page + j
            dst = (blk % 2) * (n_cmp // 2) + blk // 2
            kcv[dst:dst + 1, :] = jnp.sum(x[j * CMP_BLOCK:(j + 1) * CMP_BLOCK], axis=0, keepdims=True)
        kaug[p * page:(p + 1) * page, 0:LANES] = sel_pages[p][:, 0:LANES].astype(MXU_DTYPE)
        vsel[p * page:(p + 1) * page, :] = sel_pages[p][:, LANES:2 * LANES].astype(MXU_DTYPE)

    o_c, sels = _compressed_branch(qpad, kcv[...], past, rows, n_sel)
    _store_selection(lhs, sels, rows)

    def new_rows(ref):
        kv = jnp.concatenate([ref[0], jnp.zeros((LANES - t_len, ref.shape[2]), F32)], axis=0)
        return kv[:, 0:LANES].astype(MXU_DTYPE), kv[:, LANES:2 * LANES].astype(MXU_DTYPE)

    tk_new = lax.broadcasted_iota(jnp.int32, (rows, LANES), 1)
    m_new = (tk_new < t_len) & (tk_new <= tq)

    def joint_attention(s_past, mask_past, v_past, s_new, v_new):
        outs = []
        p_past, p_new = [], []
        for h in range(N_HEADS):
            sl = slice(h * rows, (h + 1) * rows)
            sp = s_past[sl] if mask_past is None else jnp.where(mask_past, s_past[sl], NEG_INF)
            sn = jnp.where(m_new, s_new[sl], NEG_INF)
            m = jnp.maximum(jnp.max(sp, axis=-1, keepdims=True), jnp.max(sn, axis=-1, keepdims=True))
            pp = jnp.exp(sp - m) if mask_past is None else jnp.where(mask_past, jnp.exp(sp - m), 0.0)
            pn = jnp.where(m_new, jnp.exp(sn - m), 0.0)
            den = jnp.maximum(jnp.sum(pp, axis=-1, keepdims=True) + jnp.sum(pn, axis=-1, keepdims=True), TINY)
            p_past.append((pp / den).astype(MXU_DTYPE))
            p_new.append((pn / den).astype(MXU_DTYPE))
        return (jnp.dot(jnp.concatenate(p_past, axis=0), v_past, preferred_element_type=F32)
                + jnp.dot(jnp.concatenate(p_new, axis=0), v_new, preferred_element_type=F32))

    kn, vn = new_rows(ksn_ref)
    s_past = lax.dot_general(lhs[...], kaug[...], _NT, preferred_element_type=F32)
    s_new = lax.dot_general(qpad, kn, _NT, preferred_element_type=F32)
    o_s = joint_attention(s_past, None, vsel[...], s_new, vn)

    n_buf = win_ref.shape[0]
    kwn, vwn = new_rows(kwn_ref)
    kwb = win_ref[:, 0:LANES].astype(MXU_DTYPE)
    vwb = win_ref[:, LANES:2 * LANES].astype(MXU_DTYPE)
    s_wb = lax.dot_general(qpad, kwb, _NT, preferred_element_type=F32)
    s_wn = lax.dot_general(qpad, kwn, _NT, preferred_element_type=F32)
    pos_w = past - n_buf + lax.broadcasted_iota(jnp.int32, (rows, n_buf), 1)
    dlt = pos_q - pos_w
    m_wb = (dlt >= 0) & (dlt <= WINDOW) & (pos_w >= 0)
    o_w = joint_attention(s_wb, m_wb, vwb, s_wn, vwn)

    for m, slab in enumerate(_gated_output(
            jnp.concatenate([gate_ref[0], jnp.zeros((rows - t_len, LANES), F32)], axis=0), o_c, o_s, o_w, rows)):
        out_ref[0, :, m * LANES:(m + 1) * LANES] = slab[0:t_len]

    nwin_ref[0:n_buf - t_len, :] = win_ref[t_len:n_buf, :]
    nwin_ref[n_buf - t_len:n_buf, :] = kwn_ref[0]


def _nsa_sample(page_table, q, gate, kvs_new, kvw_new, w_cmp, pool_c, pool_s, win, layer):
    dbs, t_len, d_b = q.shape
    n_pages = page_table.shape[1]
    page, kv_w = pool_c.shape[2], pool_c.shape[3]
    n_buf = win.shape[2]
    past = n_pages * page
    tok = lambda w: pl.BlockSpec((1, t_len, w), lambda b, pt: (b, 0, 0))

    def page_spec(p):
        return pl.BlockSpec((None, None, page, kv_w), lambda b, pt, p=p: (layer, pt[b * n_pages + p], 0, 0))

    grid_spec = pltpu.PrefetchScalarGridSpec(
        num_scalar_prefetch=1,
        grid=(dbs,),
        in_specs=[tok(d_b), tok(LANES), tok(kv_w), tok(kv_w),
                  pl.BlockSpec(w_cmp.shape, lambda b, pt: (0, 0)),
                  pl.BlockSpec((None, None, n_buf, kv_w), lambda b, pt: (layer, b, 0, 0))]
                 + [page_spec(p) for p in range(n_pages)] * 2,
        out_specs=[tok(d_b), pl.BlockSpec((None, n_buf, kv_w), lambda b, pt: (b, 0, 0))],
        scratch_shapes=[pltpu.VMEM((past, 2 * LANES), MXU_DTYPE), pltpu.VMEM((past, LANES), MXU_DTYPE),
                        pltpu.VMEM((past // CMP_BLOCK, kv_w), F32), pltpu.VMEM((N_HEADS * 8, 2 * LANES), MXU_DTYPE)],
    )
    return pl.pallas_call(
        functools.partial(_nsa_sample_kernel, n_pages=n_pages, page=page, past=past, t_len=t_len),
        grid_spec=grid_spec,
        out_shape=[jax.ShapeDtypeStruct((dbs, t_len, d_b), F32), jax.ShapeDtypeStruct((dbs, n_buf, kv_w), F32)],
        compiler_params=_cparams("arbitrary"),
        name="nsa_sample",
    )(page_table.reshape(-1), q, gate, kvs_new, kvw_new, w_cmp, win, *([pool_c] * n_pages), *([pool_s] * n_pages))


def _out_proj_router_kernel(x_ref, oa_ref, ob_ref, oc_ref, oas_ref, obs_ref, ocs_ref, wo_ref, g2_ref, wr_ref, br_ref,
                            xn_ref, h2_ref, ri_ref, rw_ref, cnt_ref, run_ref, *, n_ptiles):
    tm = x_ref.shape[0]

    @pl.when(pl.program_id(0) == 0)
    def _():
        run_ref[...] = jnp.zeros(run_ref.shape, F32)

    is_sample = pl.program_id(0) >= n_ptiles
    mix = jnp.concatenate([jnp.where(is_sample, oas_ref[...], oa_ref[...]),
                           jnp.where(is_sample, obs_ref[...], ob_ref[...]),
                           jnp.where(is_sample, ocs_ref[...], oc_ref[...])], axis=1)
    xn = x_ref[...] + _split_dot(_split_act(mix, wo_ref.shape[0]), wo_ref)
    xn_ref[...] = xn
    h2 = _rmsnorm(xn, g2_ref[...])
    h2_ref[...] = h2
    logits = jnp.dot(h2, wr_ref[...], preferred_element_type=F32, precision=lax.Precision.HIGHEST) + br_ref[...]

    lane = lax.broadcasted_iota(jnp.int32, (tm, LANES), 1)
    lanef = lane.astype(F32)

    def softmax_over(mask):
        m = jnp.max(jnp.where(mask, logits, -jnp.inf), axis=-1, keepdims=True)
        e = jnp.where(mask, jnp.exp(logits - m), 0.0)
        return e / jnp.sum(e, axis=-1, keepdims=True)

    def first_max(p, mask):
        pm = jnp.max(jnp.where(mask, p, -1.0), axis=-1, keepdims=True)
        idx = jnp.min(jnp.where(mask & (p == pm), lanef, 1e9), axis=-1, keepdims=True)
        return pm, idx

    is_g = lane < N_GROUPS
    g_wt, g_sel = first_max(softmax_over(is_g), is_g)
    lo = N_GROUPS + EXP_PER_GROUP * g_sel
    in_e = (lanef >= lo) & (lanef < lo + EXP_PER_GROUP)
    p_e = softmax_over(in_e)
    p0, i0 = first_max(p_e, in_e)
    p1, i1 = first_max(p_e, in_e & (lanef != i0))
    den = p0 + p1
    e0 = i0 - N_GROUPS
    e1 = i1 - N_GROUPS

    hit0 = lanef == e0
    hit1 = lanef == e1
    onehot = jnp.where(hit0 | hit1, 1.0, 0.0)
    r_i = lax.broadcasted_iota(jnp.int32, (tm, tm), 0)
    c_i = lax.broadcasted_iota(jnp.int32, (tm, tm), 1)
    ltri = jnp.where(c_i < r_i, 1.0, 0.0).astype(jnp.bfloat16)
    before = jnp.dot(ltri, onehot.astype(jnp.bfloat16), preferred_element_type=F32) + run_ref[0:1, :]
    r0 = jnp.sum(jnp.where(hit0, before, 0.0), axis=-1, keepdims=True)
    r1 = jnp.sum(jnp.where(hit1, before, 0.0), axis=-1, keepdims=True)
    run_ref[...] = run_ref[...] + jnp.sum(onehot, axis=0, keepdims=True)
    cnt_ref[...] = run_ref[...]

    ri = jnp.where(lane == 0, e0, jnp.where(lane == 1, e1, jnp.where(lane == 2, r0, jnp.where(lane == 3, r1, 0.0))))
    ri_ref[...] = ri.astype(jnp.int32)
    rw_ref[...] = jnp.where(lane == 0, g_wt * p0 / den, jnp.where(lane == 1, g_wt * p1 / den, 0.0))


def _out_proj_router(x, mix_p, mix_s, w_out, g2, w_route, b_route):
    n, d = x.shape
    tm = TOKEN_TILE
    n_ptiles = mix_p[0].shape[0] // tm
    assert all(a.shape[0] == tm for a in mix_s) and n == (n_ptiles + 1) * tm
    row = lambda w: pl.BlockSpec((tm, w), lambda i: (i, 0))
    prow = lambda a: pl.BlockSpec((tm, a.shape[1]), lambda i: (jnp.minimum(i, n_ptiles - 1), 0))
    full = lambda a: pl.BlockSpec(a.shape, lambda i: (0,) * a.ndim)
    return pl.pallas_call(
        functools.partial(_out_proj_router_kernel, n_ptiles=n_ptiles),
        grid=(n // tm,),
        in_specs=[row(d)] + [prow(a) for a in mix_p] + [full(a) for a in mix_s]
                 + [full(w_out), full(g2), full(w_route), full(b_route)],
        out_specs=[row(d), row(d), row(LANES), row(LANES), pl.BlockSpec((8, LANES), lambda i: (0, 0))],
        out_shape=[jax.ShapeDtypeStruct((n, d), F32), jax.ShapeDtypeStruct((n, d), F32),
                   jax.ShapeDtypeStruct((n, LANES), jnp.int32), jax.ShapeDtypeStruct((n, LANES), F32),
                   jax.ShapeDtypeStruct((8, LANES), F32)],
        scratch_shapes=[pltpu.VMEM((8, LANES), F32)],
        compiler_params=_cparams("arbitrary"),
        name="out_proj_router",
    )(x, *mix_p, *mix_s, w_out, g2, w_route, b_route)


def _row_copy(src_hbm, row, dst, slot, r, sem):
    return pltpu.make_async_copy(src_hbm.at[pl.ds(row, 1)], dst.at[slot, pl.ds(r, 1)], sem.at[slot])


def _expert_kernel(te_ref, nv_ref, src_ref, h_hbm, wg_ref, wu_ref, wd_ref, ys_ref, xbuf, sem):
    t = pl.program_id(0)
    nv = nv_ref[0]
    te = xbuf.shape[1]

    def issue(tile, slot):
        def body(i, carry):
            for j in range(2):
                r = 2 * i + j
                _row_copy(h_hbm, src_ref[tile * te + r], xbuf, slot, r, sem).start(priority=j)
            return carry
        lax.fori_loop(0, te // 2, body, 0, unroll=4)

    def wait_all(slot):
        pltpu.make_async_copy(h_hbm.at[pl.ds(0, te)], xbuf.at[slot], sem.at[slot]).wait()

    @pl.when((t == 0) & (nv > 0))
    def _():
        issue(0, 0)

    @pl.when(t + 1 < nv)
    def _():
        issue(t + 1, (t + 1) % 2)

    @pl.when(t < nv)
    def _():
        slot = t % 2
        wait_all(slot)
        x = xbuf[slot].astype(MXU_DTYPE)
        hg = jnp.dot(x, wg_ref[...].astype(MXU_DTYPE), preferred_element_type=F32)
        hu = jnp.dot(x, wu_ref[...].astype(MXU_DTYPE), preferred_element_type=F32)
        act = (jax.nn.silu(hg) * hu).astype(MXU_DTYPE)
        ys_ref[...] = jnp.dot(act, wd_ref[...].astype(MXU_DTYPE), preferred_element_type=F32)

    @pl.when(t >= nv)
    def _():
        ys_ref[...] = jnp.zeros(ys_ref.shape, F32)


def _experts(tile_expert, n_valid, src_tok, h2, w_gate, w_up, w_down, layer):
    n_tiles = tile_expert.shape[0]
    te = EXPERT_TILE
    d = h2.shape[1]
    d_e = w_gate.shape[3]
    wspec = lambda r, c: pl.BlockSpec((None, None, r, c), lambda t, te_r, nv_r, src_r: (layer, te_r[t], 0, 0))
    grid_spec = pltpu.PrefetchScalarGridSpec(
        num_scalar_prefetch=3,
        grid=(n_tiles,),
        in_specs=[pl.BlockSpec(memory_space=pl.ANY), wspec(d, d_e), wspec(d, d_e), wspec(d_e, d)],
        out_specs=pl.BlockSpec((te, d), lambda t, te_r, nv_r, src_r: (t, 0)),
        scratch_shapes=[pltpu.VMEM((2, te, d), F32), pltpu.SemaphoreType.DMA((2,))],
    )
    return pl.pallas_call(
        _expert_kernel,
        grid_spec=grid_spec,
        out_shape=jax.ShapeDtypeStruct((n_tiles * te, d), F32),
        compiler_params=_cparams("arbitrary"),
        name="moe_experts",
    )(tile_expert, n_valid, src_tok, h2, w_gate, w_up, w_down)


def _combine_kernel(d0_ref, d1_ref, x_ref, rw_ref, gf_ref, ys_hbm, *rest, final):
    if final:
        xo_ref, y_ref, buf, sem = rest
    else:
        xo_ref, buf, sem = rest
    t = pl.program_id(0)
    nt = pl.num_programs(0)
    tm = x_ref.shape[0]

    def issue(tile, slot):
        def body(r, carry):
            _row_copy(ys_hbm, d0_ref[tile * tm + r], buf, slot, r, sem).start(priority=0)
            _row_copy(ys_hbm, d1_ref[tile * tm + r], buf, slot, tm + r, sem).start(priority=1)
            return carry
        lax.fori_loop(0, tm, body, 0, unroll=8)

    def wait_all(slot):
        pltpu.make_async_copy(ys_hbm.at[pl.ds(0, 2 * tm)], buf.at[slot], sem.at[slot]).wait()

    @pl.when(t == 0)
    def _():
        issue(0, 0)

    @pl.when(t + 1 < nt)
    def _():
        issue(t + 1, (t + 1) % 2)

    slot = t % 2
    wait_all(slot)
    rw = rw_ref[...]
    xo = x_ref[...] + rw[:, 0:1] * buf[slot, 0:tm, :] + rw[:, 1:2] * buf[slot, tm:2 * tm, :]
    xo_ref[...] = xo
    if final:
        y_ref[...] = _rmsnorm(xo, gf_ref[...])


def _combine(d0, d1, x, rw, gf, ys, final):
    n, d = x.shape
    tm = COMBINE_TILE
    row = lambda w: pl.BlockSpec((tm, w), lambda t, a, b: (t, 0))
    n_out = 2 if final else 1
    grid_spec = pltpu.PrefetchScalarGridSpec(
        num_scalar_prefetch=2,
        grid=(n // tm,),
        in_specs=[row(d), row(LANES), pl.BlockSpec((1, d), lambda t, a, b: (0, 0)), pl.BlockSpec(memory_space=pl.ANY)],
        out_specs=[row(d)] * n_out,
        scratch_shapes=[pltpu.VMEM((2, 2 * tm, d), F32), pltpu.SemaphoreType.DMA((2,))],
    )
    return pl.pallas_call(
        functools.partial(_combine_kernel, final=final),
        grid_spec=grid_spec,
        out_shape=[jax.ShapeDtypeStruct((n, d), F32)] * n_out,
        compiler_params=_cparams("arbitrary"),
        name="moe_combine_final" if final else "moe_combine",
    )(d0, d1, x, rw, gf, ys)


def _route_plan(ri, cnt, n_tiles):
    te = EXPERT_TILE
    n = ri.shape[0]
    counts = cnt[0, :N_EXPERTS].astype(jnp.int32)
    padded = ((counts + te - 1) // te) * te
    ends = jnp.cumsum(padded)
    offs = ends - padded
    d0 = offs[ri[:, 0]] + ri[:, 2]
    d1 = offs[ri[:, 1]] + ri[:, 3]
    tok = jnp.arange(n, dtype=jnp.int32)
    src = jnp.zeros((n_tiles * te,), jnp.int32).at[d0].set(tok).at[d1].set(tok)
    tile_start = jnp.arange(n_tiles, dtype=jnp.int32) * te
    tile_expert = jnp.minimum(jnp.sum((ends[None, :] <= tile_start[:, None]).astype(jnp.int32), axis=1),
                              N_EXPERTS - 1)
    n_valid = (ends[-1:] // te).astype(jnp.int32)
    return d0, d1, src, tile_expert, n_valid


def _rope_tables(pos):
    half = HEAD_DIM // 2
    inv = ROPE_THETA ** (-jnp.arange(half, dtype=F32) / half)
    ang = pos.astype(F32)[:, None] * inv[None, :]
    cos = jnp.cos(ang)
    sin = jnp.sin(ang)
    cos_t = jnp.concatenate([cos, cos] * (LANES // HEAD_DIM), axis=1)
    sin_t = jnp.concatenate([-sin, sin] * (LANES // HEAD_DIM), axis=1)
    return cos_t, sin_t


def _block_diag(w):
    nb, bw, _ = w.shape
    out = jnp.zeros((nb * bw, nb * bw), w.dtype)
    for i in range(nb):
        out = out.at[i * bw:(i + 1) * bw, i * bw:(i + 1) * bw].set(w[i])
    return out


def kernel(x_prompt, x_sample, cache_kv_cmp, cache_kv_sel, cache_kv_win, state_lru_h, state_lru_conv, state_sconv, page_table, norm1_g, w_in, lru_conv_w, lru_conv_b, lru_wa, lru_ba, lru_wx, lru_bx, lru_lambda, nsa_cmp_wk, nsa_cmp_wv, sc_conv_w, w_out, norm2_g, router_group_w, router_group_b, router_exp_w, router_exp_b, exp_w_gate, exp_w_up, exp_w_down, norm_f_g):
    bsz, seq, d = x_prompt.shape
    dbs, t_len, _ = x_sample.shape
    depth = w_in.shape[0]
    d_a = lru_conv_w.shape[2]
    d_c = sc_conv_w.shape[2]
    kv_w = 2 * N_KV * HEAD_DIM
    d_b = N_HEADS * HEAD_DIM
    n_gate = 3 * N_HEADS
    n_p = bsz * seq
    n_s = dbs * t_len
    n = n_p + n_s
    assert n_s == TOKEN_TILE and n_p % TOKEN_TILE == 0 and seq % TOKEN_TILE == 0
    page = cache_kv_cmp.shape[2]
    past = page_table.shape[1] * page
    n_buf = cache_kv_win.shape[2]

    pos = jnp.concatenate([jnp.arange(seq, dtype=jnp.int32),
                           jnp.tile(past + jnp.arange(t_len, dtype=jnp.int32), dbs)])
    cos_t, sin_t = _rope_tables(pos)

    pool_c = cache_kv_cmp.reshape(depth, -1, page, kv_w)
    pool_s = cache_kv_sel.reshape(depth, -1, page, kv_w)
    win = cache_kv_win.reshape(depth, dbs, n_buf, kv_w)

    g_off = 2 * d_a + d_b + 3 * kv_w
    x = jnp.concatenate([x_prompt.reshape(n_p, d), x_sample.reshape(n_s, d)], axis=0)
    h0_p = jnp.zeros((bsz, 1, d_a), F32)
    lb0_p = jnp.zeros((bsz, lru_conv_w.shape[1] - 1, d_a), F32)
    sb0_p = jnp.zeros((bsz, sc_conv_w.shape[1] - 1, d_c), F32)

    n_tiles = (2 * n) // EXPERT_TILE + N_EXPERTS
    proj_parts = lambda l: 2 if l < depth - 1 else 1
    states_p, states_s = [], []
    y = None
    for l in range(depth):
        w_l = w_in[l]
        w_r = _split_weight(jnp.concatenate([w_l[:, :g_off], w_l[:, g_off + n_gate:], w_l[:, g_off:g_off + n_gate],
                                             jnp.zeros((d, LANES - n_gate), F32)], axis=1), proj_parts(l))
        lru_in, sc_in, gate, q, kvc, kvs, kvw, kvs_b, kvw_b = _in_proj(
            x, norm1_g[l][None], w_r, cos_t, sin_t, n_p, seq, d_a=d_a, d_b=d_b, kv_w=kv_w, d_c=d_c)

        seq_w = (lru_conv_w[l], lru_conv_b[l][None], _block_diag(lru_wa[l]).astype(MXU_DTYPE), lru_ba[l][None],
                 _block_diag(lru_wx[l]).astype(MXU_DTYPE), lru_bx[l][None], lru_lambda[l][None], sc_conv_w[l])
        out_a, out_c, hn_p, lbn_p, sbn_p = _seq_mix_prompt(lru_in, sc_in, h0_p, lb0_p, sb0_p, seq_w, bsz, seq,
                                                           d_a=d_a, d_c=d_c)
        tmaj = lambda a: jnp.swapaxes(a.reshape(dbs, t_len, -1), 0, 1)
        oa_s, oc_s, hn_s, lbn_s, sbn_s = _seq_mix_sample(
            tmaj(lru_in[n_p:]), tmaj(sc_in[n_p:]), state_lru_h[l], jnp.swapaxes(state_lru_conv[l], 0, 1),
            jnp.swapaxes(state_sconv[l], 0, 1), seq_w, d_a=d_a, d_c=d_c)
        oa_s = jnp.swapaxes(oa_s, 0, 1).reshape(n_s, d_a)
        oc_s = jnp.swapaxes(oc_s, 0, 1).reshape(n_s, d_c)

        w_cmp = jnp.concatenate([jnp.broadcast_to(nsa_cmp_wk[l][:, None], (CMP_BLOCK, kv_w // 2)),
                                 jnp.broadcast_to(nsa_cmp_wv[l][:, None], (CMP_BLOCK, kv_w // 2))], axis=1)
        kcv = _compress_prompt(kvc, w_cmp, bsz, seq)
        out_b = _nsa_prompt(q, gate, kcv, kvs_b, kvw_b, bsz, seq)
        s3 = lambda a: a[n_p:].reshape(dbs, t_len, -1)
        ob_s, nwin_s = _nsa_sample(page_table, s3(q), s3(gate), s3(kvs), s3(kvw), w_cmp, pool_c, pool_s, win, l)
        ob_s = ob_s.reshape(n_s, d_b)

        w_route = jnp.concatenate([router_group_w[l], router_exp_w[l],
                                   jnp.zeros((d, LANES - N_GROUPS - N_EXPERTS), F32)], axis=1)
        b_route = jnp.concatenate([router_group_b[l], router_exp_b[l],
                                   jnp.zeros((LANES - N_GROUPS - N_EXPERTS,), F32)])[None]
        xn, h2, ri, rw, cnt = _out_proj_router(x, (out_a, out_b, out_c), (oa_s, ob_s, oc_s),
                                               _split_weight(w_out[l], proj_parts(l)), norm2_g[l][None],
                                               w_route, b_route)
        d0, d1, src, tile_expert, n_valid = _route_plan(ri, cnt, n_tiles)
        ys = _experts(tile_expert, n_valid, src, h2, exp_w_gate, exp_w_up, exp_w_down, l)
        final = l == depth - 1
        outs = _combine(d0, d1, xn, rw, norm_f_g[None], ys, final)
        x = outs[0]
        if final:
            y = outs[1]

        kv6 = lambda a, lead: a.reshape(lead + (2, N_KV, HEAD_DIM))
        states_p.append((kv6(kvc[:n_p], (bsz, seq)), kv6(kvs[:n_p], (bsz, seq)),
                         kv6(kvw[:n_p].reshape(bsz, seq, kv_w)[:, seq - min(WINDOW, seq):], (bsz, min(WINDOW, seq))),
                         hn_p[:, 0], lbn_p, sbn_p))
        states_s.append((kv6(kvc[n_p:], (dbs, t_len)), kv6(kvs[n_p:], (dbs, t_len)), kv6(nwin_s, (dbs, n_buf)),
                         hn_s, jnp.swapaxes(lbn_s, 0, 1), jnp.swapaxes(sbn_s, 0, 1)))

    stack = lambda sts, i: jnp.stack([s[i] for s in sts])
    res = [y[:n_p].reshape(bsz, seq, d), y[n_p:].reshape(dbs, t_len, d)]
    for i in range(6):
        res += [stack(states_p, i), stack(states_s, i)]
    return tuple(res)
```

```python
import functools

import jax
import jax.numpy as jnp
from jax import lax
from jax.experimental import pallas as pl
from jax.experimental.pallas import tpu as pltpu

F32 = jnp.float32
MXU_DTYPE = jnp.bfloat16

HEAD_DIM = 64
N_HEADS = 8
N_KV = 2
GROUP = N_HEADS // N_KV
CMP_BLOCK = 32
SEL_BLOCK = 64
TOP_N = 16
WINDOW = 512
Q_BLOCK = 128
ROPE_THETA = 10000.0
LRU_C = 8.0
N_GROUPS = 4
EXP_PER_GROUP = 8
N_EXPERTS = N_GROUPS * EXP_PER_GROUP
RMS_EPS = 1e-6
NEG_INF = -1e30
TINY = 1e-30
FORCE = 1e6

LANES = 128
SUBLANES = 8
SUM_ROWS = 2 * SUBLANES
VMEM_LIMIT = 56 * 2 ** 20
TOKEN_TILE = 512
TIME_CHUNK = 256
KEY_TILE = 512
EXPERT_TILE = 256
COMBINE_TILE = 256

_NT = (((1,), (1,)), ((), ()))


def _cparams(*sem):
    return pltpu.CompilerParams(dimension_semantics=sem, vmem_limit_bytes=VMEM_LIMIT)


def _split_weight(w, parts):
    if parts == 1:
        return w.astype(MXU_DTYPE)[None]
    hi = lax.bitcast_convert_type(lax.bitcast_convert_type(w, jnp.uint32) & jnp.uint32(0xFFFF0000), F32)
    return jnp.stack([hi.astype(MXU_DTYPE), (w - hi).astype(MXU_DTYPE)])


def _split_act(a, parts):
    if parts == 1:
        return (a.astype(MXU_DTYPE),)
    hi = lax.bitcast_convert_type(lax.bitcast_convert_type(a, jnp.uint32) & jnp.uint32(0xFFFF0000), F32)
    return hi.astype(MXU_DTYPE), (a - hi).astype(MXU_DTYPE)


def _split_dot(a_parts, w_ref, cols=slice(None)):
    dot = lambda a, p: jnp.dot(a, w_ref[p, :, cols], preferred_element_type=F32)
    if len(a_parts) == 1:
        return dot(a_parts[0], 0)
    return (dot(a_parts[0], 1) + dot(a_parts[1], 0)) + dot(a_parts[0], 0)


def _rmsnorm(x, g):
    return x * lax.rsqrt(jnp.mean(x * x, axis=-1, keepdims=True) + RMS_EPS) * g


def _swap_halves(x):
    w = x.shape[-1]
    lane = lax.broadcasted_iota(jnp.int32, x.shape, x.ndim - 1)
    first = (lane % HEAD_DIM) < HEAD_DIM // 2
    return jnp.where(first, pltpu.roll(x, w - HEAD_DIM // 2, axis=1), pltpu.roll(x, HEAD_DIM // 2, axis=1))


def _masked_softmax(s, mask):
    s = jnp.where(mask, s, NEG_INF)
    m = jnp.max(s, axis=-1, keepdims=True)
    p = jnp.where(mask, jnp.exp(s - m), 0.0)
    return p / jnp.maximum(jnp.sum(p, axis=-1, keepdims=True), TINY)


def _select_blocks_keys(imp, pos_q, n_sel):
    blk = lax.broadcasted_iota(jnp.int32, imp.shape, 0)
    cur = pos_q // SEL_BLOCK
    forced = (blk == 0) | (blk == cur) | (blk == cur - 1)
    w = jnp.where(forced, FORCE, imp)
    w = jnp.where(blk <= cur, w, NEG_INF)
    w = jnp.where(blk < n_sel, w, -jnp.inf)
    blkf = blk.astype(F32)

    def body(_, carry):
        w, sel = carry
        m = jnp.max(w, axis=0, keepdims=True)
        first = jnp.min(jnp.where(w == m, blkf, 1e9), axis=0, keepdims=True)
        pick = blkf == first
        return jnp.where(pick, -jnp.inf, w), jnp.where(pick, 1.0, sel)

    _, sel = lax.fori_loop(0, min(TOP_N, n_sel), body, (w, jnp.zeros_like(w)))
    return sel


def _in_proj_kernel(x_ref, g_ref, w_ref, cos_ref, sin_ref,
                    lru_ref, sc_ref, gate_ref, q_ref, kvc_ref, kvs_ref, kvw_ref, kvsb_ref, kvwb_ref,
                    *, d_a, d_b, kv_w, d_c):
    h_parts = _split_act(_rmsnorm(x_ref[...], g_ref[...]), w_ref.shape[0])

    def mm(a, b):
        return _split_dot(h_parts, w_ref, slice(a, b))

    cos = cos_ref[...]
    sin = sin_ref[...]
    off = 2 * d_a
    lru_ref[...] = mm(0, off)
    q = mm(off, off + d_b)
    reps = d_b // LANES
    cos_q = jnp.concatenate([cos] * reps, axis=1)
    sin_q = jnp.concatenate([sin] * reps, axis=1)
    q_ref[...] = (q * cos_q + _swap_halves(q) * sin_q) * (HEAD_DIM ** -0.5)
    off += d_b
    for ref, bref in ((kvc_ref, None), (kvs_ref, kvsb_ref), (kvw_ref, kvwb_ref)):
        kv = mm(off, off + kv_w)
        k = kv[:, :kv_w // 2]
        kv = jnp.concatenate([k * cos + _swap_halves(k) * sin, kv[:, kv_w // 2:]], axis=1)
        ref[...] = kv
        if bref is not None:
            bref[...] = kv.astype(MXU_DTYPE)
        off += kv_w
    sc_ref[...] = mm(off, off + 3 * d_c)
    off += 3 * d_c
    gate_ref[...] = jax.nn.sigmoid(mm(off, off + LANES))


def _in_proj(x, g1, w_r, cos_t, sin_t, n_prompt, seq, *, d_a, d_b, kv_w, d_c):
    n, d = x.shape
    tm = TOKEN_TILE
    n_ptiles = n_prompt // tm
    n_stiles = seq // tm

    def tab_map(i):
        return (jnp.where(i < n_ptiles, i % n_stiles, n_stiles), 0)

    row = lambda w: pl.BlockSpec((tm, w), lambda i: (i, 0))
    widths = (2 * d_a, 3 * d_c, LANES, d_b, kv_w, kv_w, kv_w, kv_w, kv_w)
    dtypes = (F32, F32, F32, F32, F32, F32, F32, MXU_DTYPE, MXU_DTYPE)
    return pl.pallas_call(
        functools.partial(_in_proj_kernel, d_a=d_a, d_b=d_b, kv_w=kv_w, d_c=d_c),
        grid=(n // tm,),
        in_specs=[row(d),
                  pl.BlockSpec((1, d), lambda i: (0, 0)),
                  pl.BlockSpec(w_r.shape, lambda i: (0, 0, 0)),
                  pl.BlockSpec((tm, LANES), tab_map),
                  pl.BlockSpec((tm, LANES), tab_map)],
        out_specs=[row(w) for w in widths],
        out_shape=[jax.ShapeDtypeStruct((n, w), dt) for w, dt in zip(widths, dtypes)],
        compiler_params=_cparams("parallel"),
        name="in_proj",
    )(x, g1, w_r, cos_t, sin_t)


def _scan_rows(a, u):
    t = a.shape[0]
    row = lax.broadcasted_iota(jnp.int32, a.shape, 0)
    d = 1
    while d < t:
        keep = row >= d
        a_sh = jnp.where(keep, pltpu.roll(a, d, axis=0), 1.0)
        u_sh = jnp.where(keep, pltpu.roll(u, d, axis=0), 0.0)
        u = a * u_sh + u
        a = a * a_sh
        d *= 2
    return a, u


def _lru_gates(xc, wa_ref, ba_ref, wx_ref, bx_ref, lam_ref):
    xb = xc.astype(MXU_DTYPE)
    r = jax.nn.sigmoid(jnp.dot(xb, wa_ref[...], preferred_element_type=F32) + ba_ref[...])
    i = jax.nn.sigmoid(jnp.dot(xb, wx_ref[...], preferred_element_type=F32) + bx_ref[...])
    log_a = -LRU_C * r * jax.nn.softplus(-lam_ref[...])
    a = jnp.exp(log_a)
    th = jnp.tanh(log_a)
    u = jnp.sqrt(-2.0 * th / (1.0 - th)) * (i * xc)
    return a, u


def _seq_mix_kernel(lru_ref, sc_ref, h0_ref, lb0_ref, sb0_ref, cw_ref, cb_ref, wa_ref, ba_ref, wx_ref, bx_ref,
                    lam_ref, scw_ref, oa_ref, oc_ref, hn_ref, lbn_ref, sbn_ref, xpad, vpad, hcar, *, d_a, d_c):
    c = pl.program_id(1)
    tc = lru_ref.shape[0]
    nlb = lb0_ref.shape[1]
    nsb = sb0_ref.shape[1]

    @pl.when(c == 0)
    def _():
        xpad[0:8, :] = jnp.zeros((8, d_a), F32)
        xpad[8 - nlb:8, :] = lb0_ref[0]
        vpad[0:8, :] = jnp.zeros((8, d_c), F32)
        vpad[8 - nsb:8, :] = sb0_ref[0]
        hcar[...] = h0_ref[0]

    xa = lru_ref[:, 0:d_a]
    ga = lru_ref[:, d_a:2 * d_a]
    xpad[8:8 + tc, :] = xa
    xc = cw_ref[0:1, :] * xpad[8 - nlb:8 - nlb + tc, :]
    for j in range(1, nlb):
        xc = xc + cw_ref[j:j + 1, :] * xpad[8 - nlb + j:8 - nlb + j + tc, :]
    xc = xc + cw_ref[nlb:nlb + 1, :] * xa + cb_ref[...]
    a, u = _lru_gates(xc, wa_ref, ba_ref, wx_ref, bx_ref, lam_ref)
    a_cum, hs = _scan_rows(a, u)
    hs = hs + a_cum * hcar[...]
    hcar[...] = hs[tc - 1:tc, :]
    oa_ref[...] = hs * jax.nn.gelu(ga)

    v = sc_ref[:, 0:d_c] * sc_ref[:, 2 * d_c:3 * d_c]
    vpad[8:8 + tc, :] = v
    uc = scw_ref[0:1, :] * vpad[8 - nsb:8 - nsb + tc, :]
    for j in range(1, nsb):
        uc = uc + scw_ref[j:j + 1, :] * vpad[8 - nsb + j:8 - nsb + j + tc, :]
    uc = uc + scw_ref[nsb:nsb + 1, :] * v
    oc_ref[...] = sc_ref[:, d_c:2 * d_c] * uc

    hn_ref[0] = hs[tc - 1:tc, :]
    lbn_ref[0] = xpad[8 + tc - nlb:8 + tc, :]
    sbn_ref[0] = vpad[8 + tc - nsb:8 + tc, :]
    xpad[0:8, :] = xpad[tc:tc + 8, :]
    vpad[0:8, :] = vpad[tc:tc + 8, :]


def _seq_mix_prompt(lru_in, sc_in, h0, lb0, sb0, wts, bsz, seq, *, d_a, d_c):
    n = bsz * seq
    tc = TIME_CHUNK
    nch = seq // tc
    row = lambda w: pl.BlockSpec((tc, w), lambda b, c: (b * nch + c, 0))
    full = lambda a: pl.BlockSpec(a.shape, lambda b, c: (0,) * a.ndim)
    state = lambda r, w: pl.BlockSpec((1, r, w), lambda b, c: (b, 0, 0))
    nlb, nsb = lb0.shape[1], sb0.shape[1]
    return pl.pallas_call(
        functools.partial(_seq_mix_kernel, d_a=d_a, d_c=d_c),
        grid=(bsz, nch),
        in_specs=[row(2 * d_a), row(3 * d_c), state(1, d_a), state(nlb, d_a), state(nsb, d_c)]
                 + [full(w) for w in wts],
        out_specs=[row(d_a), row(d_c), state(1, d_a), state(nlb, d_a), state(nsb, d_c)],
        out_shape=[jax.ShapeDtypeStruct((n, d_a), F32), jax.ShapeDtypeStruct((n, d_c), F32),
                   jax.ShapeDtypeStruct((bsz, 1, d_a), F32), jax.ShapeDtypeStruct((bsz, nlb, d_a), F32),
                   jax.ShapeDtypeStruct((bsz, nsb, d_c), F32)],
        scratch_shapes=[pltpu.VMEM((tc + 8, d_a), F32), pltpu.VMEM((tc + 8, d_c), F32), pltpu.VMEM((1, d_a), F32)],
        compiler_params=_cparams("arbitrary", "arbitrary"),
        name="seq_mix_prompt",
    )(lru_in, sc_in, h0, lb0, sb0, *wts)


def _seq_mix_sample_kernel(lru_ref, sc_ref, h0_ref, lb0_ref, sb0_ref, cw_ref, cb_ref, wa_ref, ba_ref, wx_ref,
                           bx_ref, lam_ref, scw_ref, oa_ref, oc_ref, hn_ref, lbn_ref, sbn_ref, *, d_a, d_c):
    t_len = lru_ref.shape[0]
    nlb = lb0_ref.shape[0]
    nsb = sb0_ref.shape[0]
    xs = [lb0_ref[j] for j in range(nlb)] + [lru_ref[t][:, 0:d_a] for t in range(t_len)]
    vs = [sb0_ref[j] for j in range(nsb)] + [sc_ref[t][:, 0:d_c] * sc_ref[t][:, 2 * d_c:3 * d_c] for t in range(t_len)]
    h = h0_ref[...]
    for t in range(t_len):
        xc = cw_ref[0:1, :] * xs[t]
        for j in range(1, nlb + 1):
            xc = xc + cw_ref[j:j + 1, :] * xs[t + j]
        xc = xc + cb_ref[...]
        a, u = _lru_gates(xc, wa_ref, ba_ref, wx_ref, bx_ref, lam_ref)
        h = a * h + u
        oa_ref[t] = h * jax.nn.gelu(lru_ref[t][:, d_a:2 * d_a])
        uc = scw_ref[0:1, :] * vs[t]
        for j in range(1, nsb + 1):
            uc = uc + scw_ref[j:j + 1, :] * vs[t + j]
        oc_ref[t] = sc_ref[t][:, d_c:2 * d_c] * uc
    hn_ref[...] = h
    for j in range(nlb):
        lbn_ref[j] = xs[t_len + j]
    for j in range(nsb):
        sbn_ref[j] = vs[t_len + j]


def _seq_mix_sample(lru_t, sc_t, h0, lb0_t, sb0_t, wts, *, d_a, d_c):
    t_len, bsz, _ = lru_t.shape
    outs = [jax.ShapeDtypeStruct((t_len, bsz, d_a), F32), jax.ShapeDtypeStruct((t_len, bsz, d_c), F32),
            jax.ShapeDtypeStruct(h0.shape, F32), jax.ShapeDtypeStruct(lb0_t.shape, F32),
            jax.ShapeDtypeStruct(sb0_t.shape, F32)]
    return pl.pallas_call(
        functools.partial(_seq_mix_sample_kernel, d_a=d_a, d_c=d_c),
        out_shape=outs,
        compiler_params=pltpu.CompilerParams(vmem_limit_bytes=VMEM_LIMIT),
        name="seq_mix_sample",
    )(lru_t, sc_t, h0, lb0_t, sb0_t, *wts)


def _compress_kernel(kvc_ref, w_ref, out_ref):
    half = out_ref.shape[0] // 2
    x = kvc_ref[...].reshape(half, 2 * CMP_BLOCK, out_ref.shape[1])
    w = w_ref[...][None]
    out_ref[0:half, :] = jnp.sum(x[:, 0:CMP_BLOCK, :] * w, axis=1)
    out_ref[half:2 * half, :] = jnp.sum(x[:, CMP_BLOCK:2 * CMP_BLOCK, :] * w, axis=1)


def _compress_prompt(kvc, w_cmp, bsz, seq):
    kv_w = kvc.shape[1]
    n_cmp = seq // CMP_BLOCK
    return pl.pallas_call(
        _compress_kernel,
        grid=(bsz,),
        in_specs=[pl.BlockSpec((seq, kv_w), lambda b: (b, 0)), pl.BlockSpec(w_cmp.shape, lambda b: (0, 0))],
        out_specs=pl.BlockSpec((n_cmp, kv_w), lambda b: (b, 0)),
        out_shape=jax.ShapeDtypeStruct((bsz * n_cmp, kv_w), F32),
        compiler_params=_cparams("parallel"),
        name="nsa_compress",
    )(kvc, w_cmp)


def _pad_heads(q, lhs_ref, rows):
    lane = lax.broadcasted_iota(jnp.int32, (rows, LANES), 1)
    for h in range(N_HEADS):
        k = h // GROUP
        slab = q[:, (h // 2) * LANES:(h // 2 + 1) * LANES]
        if h % 2 != k:
            slab = pltpu.roll(slab, HEAD_DIM, axis=1)
        keep = (lane >= k * HEAD_DIM) & (lane < (k + 1) * HEAD_DIM)
        lhs_ref[h * rows:(h + 1) * rows, 0:LANES] = jnp.where(keep, slab, 0.0).astype(MXU_DTYPE)


def _gated_output(gate, o_c, o_s, o_w, rows):
    lane = lax.broadcasted_iota(jnp.int32, (rows, LANES), 1)
    slabs = []
    for m in range(N_HEADS // 2):
        parts = []
        for h in (2 * m, 2 * m + 1):
            k = h // GROUP
            sl = slice(h * rows, (h + 1) * rows)
            r = (gate[:, 3 * h:3 * h + 1] * o_c[sl] + gate[:, 3 * h + 1:3 * h + 2] * o_s[sl]
                 + gate[:, 3 * h + 2:3 * h + 3] * o_w[sl])
            if h % 2 != k:
                r = pltpu.roll(r, HEAD_DIM, axis=1)
            parts.append(r)
        slabs.append(jnp.where(lane < HEAD_DIM, parts[0], parts[1]))
    return slabs


def _compressed_branch(qpad, kcv, pos0, rows, n_sel):
    n_cmp = kcv.shape[0]
    half = n_cmp // 2
    pos_q = pos0 + lax.broadcasted_iota(jnp.int32, (rows, 1), 0)
    kc = kcv[:, 0:LANES].astype(MXU_DTYPE)
    vc = kcv[:, LANES:2 * LANES].astype(MXU_DTYPE)
    s_c = lax.dot_general(qpad, kc, _NT, preferred_element_type=F32)
    col = lax.broadcasted_iota(jnp.int32, (rows, n_cmp), 1)
    blk = jnp.where(col < half, 2 * col, 2 * (col - half) + 1)
    m_c = (blk + 1) * CMP_BLOCK - 1 <= pos_q
    ps = []
    imp = [jnp.zeros((rows, n_cmp), F32) for _ in range(N_KV)]
    for h in range(N_HEADS):
        p = _masked_softmax(s_c[h * rows:(h + 1) * rows], m_c)
        imp[h // GROUP] = imp[h // GROUP] + p
        ps.append(p.astype(MXU_DTYPE))
    o_c = jnp.dot(jnp.concatenate(ps, axis=0), vc, preferred_element_type=F32)
    imps = []
    for k in range(N_KV):
        imp_s = imp[k][:, 0:half] + imp[k][:, half:n_cmp]
        if half < LANES:
            imp_s = jnp.concatenate([imp_s, jnp.zeros((rows, LANES - half), F32)], axis=1)
        imps.append(imp_s)
    imp_t = jnp.concatenate(imps + [jnp.zeros((LANES - N_KV * rows, LANES), F32)], axis=0).T
    n_rows = -(-n_sel // 8) * 8
    pos_l = pos0 + lax.broadcasted_iota(jnp.int32, (1, LANES), 1) % rows
    sel_t = _select_blocks_keys(imp_t[0:n_rows], pos_l, n_sel)
    sel = jnp.concatenate([sel_t, jnp.zeros((LANES - n_rows, LANES), F32)], axis=0).T
    return o_c, [sel[k * rows:(k + 1) * rows] for k in range(N_KV)]


def _store_selection(lhs_ref, sels, rows):
    for k in range(N_KV):
        neg = jnp.where(sels[k] > 0.0, 0.0, NEG_INF).astype(MXU_DTYPE)
        for g in range(GROUP):
            h = k * GROUP + g
            lhs_ref[h * rows:(h + 1) * rows, LANES:2 * LANES] = neg


def _block_onehot(n_rows):
    row = lax.broadcasted_iota(jnp.int32, (n_rows, LANES), 0)
    lane = lax.broadcasted_iota(jnp.int32, (n_rows, LANES), 1)
    return jnp.where(row // SEL_BLOCK == lane, 1.0, 0.0).astype(MXU_DTYPE)


def _masked_softmax_keys(s, mask):
    s = jnp.where(mask, s, NEG_INF)
    m = jnp.max(s, axis=0, keepdims=True)
    p = jnp.where(mask, jnp.exp(s - m), 0.0)
    return p / jnp.maximum(jnp.sum(p, axis=0, keepdims=True), TINY)


def _nsa_prompt_kernel(q_ref, gate_ref, kcv_ref, kvs_ref, kvw_ref, out_ref,
                       kaug, vst, vwt, vct, lhs, m_sc, acc_sc, s_a, s_b, *, seq):
    qb = pl.program_id(1)
    rows = Q_BLOCK
    tk = KEY_TILE
    s0 = qb * rows
    n_sel = seq // SEL_BLOCK
    n_cmp = seq // CMP_BLOCK
    cols = N_HEADS * rows

    @pl.when(qb == 0)
    def _():
        kaug[:, 0:LANES] = kvs_ref[:, 0:LANES]
        kaug[:, LANES:2 * LANES] = _block_onehot(seq)
        vct[...] = kcv_ref[:, LANES:2 * LANES].T.astype(MXU_DTYPE)

        def tr_sel(i, c):
            r0 = pl.multiple_of(i * tk, tk)
            ones_row = jnp.where(lax.broadcasted_iota(jnp.int32, (SUM_ROWS, tk), 0) == 0, 1.0, 0.0)
            v_t = kvs_ref[pl.ds(r0, tk), LANES:2 * LANES].astype(F32).T
            vst[i] = jnp.concatenate([v_t, ones_row], axis=0).astype(MXU_DTYPE)
            return c

        def tr_win(i, c):
            r0 = pl.multiple_of(i * rows, rows)
            vwt[i] = kvw_ref[pl.ds(r0, rows), LANES:2 * LANES].astype(F32).T.astype(MXU_DTYPE)
            return c

        lax.fori_loop(0, seq // tk, tr_sel, 0)
        lax.fori_loop(0, seq // rows, tr_win, 0)

    _pad_heads(q_ref[...], lhs, rows)
    qpad = lhs[:, 0:LANES]
    lane = lax.broadcasted_iota(jnp.int32, (1, cols), 1)
    pos_q = s0 + lane % rows

    half = n_cmp // 2
    s_c = lax.dot_general(kcv_ref[:, 0:LANES].astype(MXU_DTYPE), qpad, _NT, preferred_element_type=F32)
    r_c = lax.broadcasted_iota(jnp.int32, (n_cmp, cols), 0)
    blk_c = jnp.where(r_c < half, 2 * r_c, 2 * (r_c - half) + 1)
    p_c = _masked_softmax_keys(s_c, (blk_c + 1) * CMP_BLOCK - 1 <= pos_q)
    o_c = jnp.dot(vct[...], p_c.astype(MXU_DTYPE), preferred_element_type=F32)

    imp = []
    for k in range(N_KV):
        acc = p_c[:, k * GROUP * rows:(k * GROUP + 1) * rows]
        for g in range(1, GROUP):
            acc = acc + p_c[:, (k * GROUP + g) * rows:(k * GROUP + g + 1) * rows]
        imp.append(acc[0:half] + acc[half:n_cmp])
    sel = _select_blocks_keys(jnp.concatenate(imp, axis=1), pos_q[:, 0:N_KV * rows], n_sel)
    for k in range(N_KV):
        sel_k = sel[:, k * rows:(k + 1) * rows]
        if n_sel < LANES:
            sel_k = jnp.concatenate([sel_k, jnp.zeros((LANES - n_sel, rows), F32)], axis=0)
        neg = jnp.where(sel_k.T > 0.0, 0.0, NEG_INF).astype(MXU_DTYPE)
        for g in range(GROUP):
            h = k * GROUP + g
            lhs[h * rows:(h + 1) * rows, LANES:2 * LANES] = neg

    m_sc[...] = jnp.full(m_sc.shape, NEG_INF, F32)
    acc_sc[...] = jnp.zeros(acc_sc.shape, F32)

    last_tile = seq // tk - 1

    def scores(s_ref, kt):
        k0 = pl.multiple_of(jnp.minimum(kt, last_tile) * tk, tk)
        s_ref[...] = lax.dot_general(kaug[pl.ds(k0, tk), :], lhs[...], _NT, preferred_element_type=F32)

    def update(s_ref, kt, causal):
        s = s_ref[...]
        if causal:
            s = jnp.where(kt * tk + lax.broadcasted_iota(jnp.int32, (tk, cols), 0) <= pos_q, s, NEG_INF)
        m_old = m_sc[...]
        m_new = jnp.maximum(m_old, jnp.max(s, axis=0, keepdims=True))
        alpha = jnp.exp(m_old - m_new)
        p = jnp.exp(s - m_new)
        m_sc[...] = m_new
        vt = vst[jnp.minimum(kt, last_tile)]
        acc_sc[...] = alpha * acc_sc[...] + jnp.dot(vt, p.astype(MXU_DTYPE), preferred_element_type=F32)

    n_pairs = (s0 // tk) // 2
    scores(s_a, 0)

    def pair_body(i, carry):
        scores(s_b, 2 * i + 1)
        update(s_a, 2 * i, False)
        scores(s_a, 2 * i + 2)
        update(s_b, 2 * i + 1, False)
        return carry

    lax.fori_loop(0, n_pairs, pair_body, 0)
    t0 = 2 * n_pairs
    scores(s_b, t0 + 1)
    update(s_a, t0, True)
    update(s_b, t0 + 1, True)
    o_s = acc_sc[0:LANES, :] / jnp.maximum(acc_sc[LANES:LANES + 1, :], TINY)

    band = WINDOW + rows
    start = pl.multiple_of(jnp.maximum(s0 - WINDOW, 0), rows)
    s_w = lax.dot_general(kvw_ref[pl.ds(start, band), 0:LANES], qpad, _NT, preferred_element_type=F32)
    dlt = pos_q - (start + lax.broadcasted_iota(jnp.int32, (band, cols), 0))
    p_w = _masked_softmax_keys(s_w, (dlt >= 0) & (dlt <= WINDOW))
    t0 = start // rows
    vw = jnp.concatenate([vwt[t0 + j] for j in range(band // rows)], axis=1)
    o_w = jnp.dot(vw, p_w.astype(MXU_DTYPE), preferred_element_type=F32)

    g_t = gate_ref[...].T
    for m in range(N_HEADS // 2):
        parts = []
        for h in (2 * m, 2 * m + 1):
            k = h // GROUP
            rs = slice(k * HEAD_DIM, (k + 1) * HEAD_DIM)
            ls = slice(h * rows, (h + 1) * rows)
            parts.append(g_t[3 * h:3 * h + 1] * o_c[rs, ls] + g_t[3 * h + 1:3 * h + 2] * o_s[rs, ls]
                         + g_t[3 * h + 2:3 * h + 3] * o_w[rs, ls])
        out_ref[:, m * LANES:(m + 1) * LANES] = jnp.concatenate(parts, axis=0).T


def _nsa_prompt(q, gate, kcv, kvs_b, kvw_b, bsz, seq):
    n, d_b = bsz * seq, q.shape[1]
    kv_w = kvs_b.shape[1]
    nq = seq // Q_BLOCK
    n_cmp = seq // CMP_BLOCK
    rows = N_HEADS * Q_BLOCK
    return pl.pallas_call(
        functools.partial(_nsa_prompt_kernel, seq=seq),
        grid=(bsz, nq),
        in_specs=[pl.BlockSpec((Q_BLOCK, d_b), lambda b, i: (b * nq + i, 0)),
                  pl.BlockSpec((Q_BLOCK, LANES), lambda b, i: (b * nq + i, 0)),
                  pl.BlockSpec((n_cmp, kv_w), lambda b, i: (b, 0)),
                  pl.BlockSpec((seq, kv_w), lambda b, i: (b, 0)),
                  pl.BlockSpec((seq, kv_w), lambda b, i: (b, 0))],
        out_specs=pl.BlockSpec((Q_BLOCK, d_b), lambda b, i: (b * nq + i, 0)),
        out_shape=jax.ShapeDtypeStruct((n, d_b), F32),
        scratch_shapes=[pltpu.VMEM((seq, 2 * LANES), MXU_DTYPE),
                        pltpu.VMEM((seq // KEY_TILE, LANES + SUM_ROWS, KEY_TILE), MXU_DTYPE),
                        pltpu.VMEM((seq // Q_BLOCK, LANES, Q_BLOCK), MXU_DTYPE),
                        pltpu.VMEM((LANES, n_cmp), MXU_DTYPE),
                        pltpu.VMEM((rows, 2 * LANES), MXU_DTYPE),
                        pltpu.VMEM((1, rows), F32), pltpu.VMEM((LANES + SUM_ROWS, rows), F32),
                        pltpu.VMEM((KEY_TILE, rows), F32), pltpu.VMEM((KEY_TILE, rows), F32)],
        compiler_params=_cparams("arbitrary", "arbitrary"),
        name="nsa_prompt",
    )(q, gate, kcv, kvs_b, kvw_b)


def _nsa_sample_kernel(pt_ref, q_ref, gate_ref, ksn_ref, kwn_ref, wcmp_ref, win_ref, *rest,
                       n_pages, page, past, t_len):
    cmp_pages = rest[:n_pages]
    sel_pages = rest[n_pages:2 * n_pages]
    out_ref, nwin_ref, kaug, vsel, kcv, lhs = rest[2 * n_pages:]
    rows = 8
    n_cmp = past // CMP_BLOCK
    n_sel = pl.cdiv(past + t_len, SEL_BLOCK)
    per_page = page // CMP_BLOCK

    @pl.when(pl.program_id(0) == 0)
    def _():
        kaug[:, LANES:2 * LANES] = _block_onehot(past)

    q = jnp.concatenate([q_ref[0], jnp.zeros((rows - t_len, q_ref.shape[2]), F32)], axis=0)
    _pad_heads(q, lhs, rows)
    qpad = lhs[:, 0:LANES]
    tq = lax.broadcasted_iota(jnp.int32, (rows, 1), 0)
    pos_q = past + tq

    wrep = jnp.concatenate([wcmp_ref[...]] * per_page, axis=0)
    for p in range(n_pages):
        x = cmp_pages[p][...] * wrep
        for j in range(per_page):
            blk = p * per_page + j
            dst = (blk % 2) * (n_cmp // 2) + blk // 2
            kcv[dst:dst + 1, :] = jnp.sum(x[j * CMP_BLOCK:(j + 1) * CMP_BLOCK], axis=0, keepdims=True)
        kaug[p * page:(p + 1) * page, 0:LANES] = sel_pages[p][:, 0:LANES].astype(MXU_DTYPE)
        vsel[p * page:(p + 1) * page, :] = sel_pages[p][:, LANES:2 * LANES].astype(MXU_DTYPE)

    o_c, sels = _compressed_branch(qpad, kcv[...], past, rows, n_sel)
    _store_selection(lhs, sels, rows)

    def new_rows(ref):
        kv = jnp.concatenate([ref[0], jnp.zeros((LANES - t_len, ref.shape[2]), F32)], axis=0)
        return kv[:, 0:LANES].astype(MXU_DTYPE), kv[:, LANES:2 * LANES].astype(MXU_DTYPE)

    tk_new = lax.broadcasted_iota(jnp.int32, (rows, LANES), 1)
    m_new = (tk_new < t_len) & (tk_new <= tq)

    def joint_attention(s_past, mask_past, v_past, s_new, v_new):
        outs = []
        p_past, p_new = [], []
        for h in range(N_HEADS):
            sl = slice(h * rows, (h + 1) * rows)
            sp = s_past[sl] if mask_past is None else jnp.where(mask_past, s_past[sl], NEG_INF)
            sn = jnp.where(m_new, s_new[sl], NEG_INF)
            m = jnp.maximum(jnp.max(sp, axis=-1, keepdims=True), jnp.max(sn, axis=-1, keepdims=True))
            pp = jnp.exp(sp - m) if mask_past is None else jnp.where(mask_past, jnp.exp(sp - m), 0.0)
            pn = jnp.where(m_new, jnp.exp(sn - m), 0.0)
            den = jnp.maximum(jnp.sum(pp, axis=-1, keepdims=True) + jnp.sum(pn, axis=-1, keepdims=True), TINY)
            p_past.append((pp / den).astype(MXU_DTYPE))
            p_new.append((pn / den).astype(MXU_DTYPE))
        return (jnp.dot(jnp.concatenate(p_past, axis=0), v_past, preferred_element_type=F32)
                + jnp.dot(jnp.concatenate(p_new, axis=0), v_new, preferred_element_type=F32))

    kn, vn = new_rows(ksn_ref)
    s_past = lax.dot_general(lhs[...], kaug[...], _NT, preferred_element_type=F32)
    s_new = lax.dot_general(qpad, kn, _NT, preferred_element_type=F32)
    o_s = joint_attention(s_past, None, vsel[...], s_new, vn)

    n_buf = win_ref.shape[0]
    kwn, vwn = new_rows(kwn_ref)
    kwb = win_ref[:, 0:LANES].astype(MXU_DTYPE)
    vwb = win_ref[:, LANES:2 * LANES].astype(MXU_DTYPE)
    s_wb = lax.dot_general(qpad, kwb, _NT, preferred_element_type=F32)
    s_wn = lax.dot_general(qpad, kwn, _NT, preferred_element_type=F32)
    pos_w = past - n_buf + lax.broadcasted_iota(jnp.int32, (rows, n_buf), 1)
    dlt = pos_q - pos_w
    m_wb = (dlt >= 0) & (dlt <= WINDOW) & (pos_w >= 0)
    o_w = joint_attention(s_wb, m_wb, vwb, s_wn, vwn)

    for m, slab in enumerate(_gated_output(
            jnp.concatenate([gate_ref[0], jnp.zeros((rows - t_len, LANES), F32)], axis=0), o_c, o_s, o_w, rows)):
        out_ref[0, :, m * LANES:(m + 1) * LANES] = slab[0:t_len]

    nwin_ref[0:n_buf - t_len, :] = win_ref[t_len:n_buf, :]
    nwin_ref[n_buf - t_len:n_buf, :] = kwn_ref[0]


def _nsa_sample(page_table, q, gate, kvs_new, kvw_new, w_cmp, pool_c, pool_s, win, layer):
    dbs, t_len, d_b = q.shape
    n_pages = page_table.shape[1]
    page, kv_w = pool_c.shape[2], pool_c.shape[3]
    n_buf = win.shape[2]
    past = n_pages * page
    tok = lambda w: pl.BlockSpec((1, t_len, w), lambda b, pt: (b, 0, 0))

    def page_spec(p):
        return pl.BlockSpec((None, None, page, kv_w), lambda b, pt, p=p: (layer, pt[b * n_pages + p], 0, 0))

    grid_spec = pltpu.PrefetchScalarGridSpec(
        num_scalar_prefetch=1,
        grid=(dbs,),
        in_specs=[tok(d_b), tok(LANES), tok(kv_w), tok(kv_w),
                  pl.BlockSpec(w_cmp.shape, lambda b, pt: (0, 0)),
                  pl.BlockSpec((None, None, n_buf, kv_w), lambda b, pt: (layer, b, 0, 0))]
                 + [page_spec(p) for p in range(n_pages)] * 2,
        out_specs=[tok(d_b), pl.BlockSpec((None, n_buf, kv_w), lambda b, pt: (b, 0, 0))],
        scratch_shapes=[pltpu.VMEM((past, 2 * LANES), MXU_DTYPE), pltpu.VMEM((past, LANES), MXU_DTYPE),
                        pltpu.VMEM((past // CMP_BLOCK, kv_w), F32), pltpu.VMEM((N_HEADS * 8, 2 * LANES), MXU_DTYPE)],
    )
    return pl.pallas_call(
        functools.partial(_nsa_sample_kernel, n_pages=n_pages, page=page, past=past, t_len=t_len),
        grid_spec=grid_spec,
        out_shape=[jax.ShapeDtypeStruct((dbs, t_len, d_b), F32), jax.ShapeDtypeStruct((dbs, n_buf, kv_w), F32)],
        compiler_params=_cparams("arbitrary"),
        name="nsa_sample",
    )(page_table.reshape(-1), q, gate, kvs_new, kvw_new, w_cmp, win, *([pool_c] * n_pages), *([pool_s] * n_pages))


def _out_proj_router_kernel(x_ref, oa_ref, ob_ref, oc_ref, oas_ref, obs_ref, ocs_ref, wo_ref, g2_ref, wr_ref, br_ref,
                            xn_ref, h2_ref, ri_ref, rw_ref, cnt_ref, run_ref, *, n_ptiles):
    tm = x_ref.shape[0]

    @pl.when(pl.program_id(0) == 0)
    def _():
        run_ref[...] = jnp.zeros(run_ref.shape, F32)

    is_sample = pl.program_id(0) >= n_ptiles
    mix = jnp.concatenate([jnp.where(is_sample, oas_ref[...], oa_ref[...]),
                           jnp.where(is_sample, obs_ref[...], ob_ref[...]),
                           jnp.where(is_sample, ocs_ref[...], oc_ref[...])], axis=1)
    xn = x_ref[...] + _split_dot(_split_act(mix, wo_ref.shape[0]), wo_ref)
    xn_ref[...] = xn
    h2 = _rmsnorm(xn, g2_ref[...])
    _store_token_tiles(h2_ref, h2)
    logits = jnp.dot(h2, wr_ref[...], preferred_element_type=F32, precision=lax.Precision.HIGHEST) + br_ref[...]

    lane = lax.broadcasted_iota(jnp.int32, (tm, LANES), 1)
    lanef = lane.astype(F32)

    def softmax_over(mask):
        m = jnp.max(jnp.where(mask, logits, -jnp.inf), axis=-1, keepdims=True)
        e = jnp.where(mask, jnp.exp(logits - m), 0.0)
        return e / jnp.sum(e, axis=-1, keepdims=True)

    def first_max(p, mask):
        pm = jnp.max(jnp.where(mask, p, -1.0), axis=-1, keepdims=True)
        idx = jnp.min(jnp.where(mask & (p == pm), lanef, 1e9), axis=-1, keepdims=True)
        return pm, idx

    is_g = lane < N_GROUPS
    g_wt, g_sel = first_max(softmax_over(is_g), is_g)
    lo = N_GROUPS + EXP_PER_GROUP * g_sel
    in_e = (lanef >= lo) & (lanef < lo + EXP_PER_GROUP)
    p_e = softmax_over(in_e)
    p0, i0 = first_max(p_e, in_e)
    p1, i1 = first_max(p_e, in_e & (lanef != i0))
    den = p0 + p1
    e0 = i0 - N_GROUPS
    e1 = i1 - N_GROUPS

    hit0 = lanef == e0
    hit1 = lanef == e1
    onehot = jnp.where(hit0 | hit1, 1.0, 0.0)
    r_i = lax.broadcasted_iota(jnp.int32, (tm, tm), 0)
    c_i = lax.broadcasted_iota(jnp.int32, (tm, tm), 1)
    ltri = jnp.where(c_i < r_i, 1.0, 0.0).astype(jnp.bfloat16)
    before = jnp.dot(ltri, onehot.astype(jnp.bfloat16), preferred_element_type=F32) + run_ref[0:1, :]
    r0 = jnp.sum(jnp.where(hit0, before, 0.0), axis=-1, keepdims=True)
    r1 = jnp.sum(jnp.where(hit1, before, 0.0), axis=-1, keepdims=True)
    run_ref[...] = run_ref[...] + jnp.sum(onehot, axis=0, keepdims=True)
    cnt_ref[...] = run_ref[...]

    ri = jnp.where(lane == 0, e0, jnp.where(lane == 1, e1, jnp.where(lane == 2, r0, jnp.where(lane == 3, r1, 0.0))))
    ri_ref[...] = ri.T[0:8].astype(jnp.int32)
    rw_ref[...] = jnp.where(lane == 0, g_wt * p0 / den, jnp.where(lane == 1, g_wt * p1 / den, 0.0))


def _out_proj_router(x, mix_p, mix_s, w_out, g2, w_route, b_route):
    n, d = x.shape
    tm = TOKEN_TILE
    n_ptiles = mix_p[0].shape[0] // tm
    assert all(a.shape[0] == tm for a in mix_s) and n == (n_ptiles + 1) * tm
    row = lambda w: pl.BlockSpec((tm, w), lambda i: (i, 0))
    prow = lambda a: pl.BlockSpec((tm, a.shape[1]), lambda i: (jnp.minimum(i, n_ptiles - 1), 0))
    full = lambda a: pl.BlockSpec(a.shape, lambda i: (0,) * a.ndim)
    return pl.pallas_call(
        functools.partial(_out_proj_router_kernel, n_ptiles=n_ptiles),
        grid=(n // tm,),
        in_specs=[row(d)] + [prow(a) for a in mix_p] + [full(a) for a in mix_s]
                 + [full(w_out), full(g2), full(w_route), full(b_route)],
        out_specs=[row(d), pl.BlockSpec((tm * SUBLANES, LANES), lambda i: (i, 0)),
                   pl.BlockSpec((8, tm), lambda i: (0, i)), row(LANES), pl.BlockSpec((8, LANES), lambda i: (0, 0))],
        out_shape=[jax.ShapeDtypeStruct((n, d), F32), jax.ShapeDtypeStruct((n * SUBLANES, LANES), F32),
                   jax.ShapeDtypeStruct((8, n), jnp.int32), jax.ShapeDtypeStruct((n, LANES), F32),
                   jax.ShapeDtypeStruct((8, LANES), F32)],
        scratch_shapes=[pltpu.VMEM((8, LANES), F32)],
        compiler_params=_cparams("arbitrary"),
        name="out_proj_router",
    )(x, *mix_p, *mix_s, w_out, g2, w_route, b_route)


def _store_token_tiles(ref, x):
    t = x.shape[0]
    for c in range(SUBLANES):
        ref[pl.ds(c, t, stride=SUBLANES), :] = x[:, c * LANES:(c + 1) * LANES]


def _load_token_tiles(ref, first_row, t):
    return jnp.concatenate([ref[pl.ds(first_row + c, t, stride=SUBLANES), :] for c in range(SUBLANES)], axis=1)


def _row_copy(src_hbm, row, dst, slot, r, sem):
    return pltpu.make_async_copy(src_hbm.at[pl.ds(pl.multiple_of(row * SUBLANES, SUBLANES), SUBLANES)],
                                 dst.at[slot, pl.ds(pl.multiple_of(r * SUBLANES, SUBLANES), SUBLANES)],
                                 sem.at[slot])


def _expert_kernel(te_ref, nv_ref, src_ref, h_hbm, wg_ref, wu_ref, wd_ref, ys_ref, xbuf, sem):
    t = pl.program_id(0)
    nv = nv_ref[0]
    te = xbuf.shape[1] // SUBLANES

    def issue(tile, slot):
        def body(r, carry):
            _row_copy(h_hbm, src_ref[tile * te + r], xbuf, slot, r, sem).start()
            return carry
        lax.fori_loop(0, te, body, 0, unroll=8)

    def wait_all(slot):
        pltpu.make_async_copy(h_hbm.at[pl.ds(0, te * SUBLANES)], xbuf.at[slot], sem.at[slot]).wait()

    @pl.when((t == 0) & (nv > 0))
    def _():
        issue(0, 0)

    @pl.when(t + 1 < nv)
    def _():
        issue(t + 1, (t + 1) % 2)

    @pl.when(t < nv)
    def _():
        slot = t % 2
        wait_all(slot)
        x = _load_token_tiles(xbuf.at[slot], 0, te).astype(MXU_DTYPE)
        hg = jnp.dot(x, wg_ref[...].astype(MXU_DTYPE), preferred_element_type=F32)
        hu = jnp.dot(x, wu_ref[...].astype(MXU_DTYPE), preferred_element_type=F32)
        act = (jax.nn.silu(hg) * hu).astype(MXU_DTYPE)
        _store_token_tiles(ys_ref, jnp.dot(act, wd_ref[...].astype(MXU_DTYPE), preferred_element_type=F32))

    @pl.when(t >= nv)
    def _():
        ys_ref[...] = jnp.zeros(ys_ref.shape, F32)


def _experts(tile_expert, n_valid, src_tok, h2, w_gate, w_up, w_down, layer):
    n_tiles = tile_expert.shape[0]
    te = EXPERT_TILE
    d = SUBLANES * LANES
    d_e = w_gate.shape[3]
    assert w_gate.shape[2] == d and h2.shape[1] == LANES
    wspec = lambda r, c: pl.BlockSpec((None, None, r, c), lambda t, te_r, nv_r, src_r: (layer, te_r[t], 0, 0))
    grid_spec = pltpu.PrefetchScalarGridSpec(
        num_scalar_prefetch=3,
        grid=(n_tiles,),
        in_specs=[pl.BlockSpec(memory_space=pl.ANY), wspec(d, d_e), wspec(d, d_e), wspec(d_e, d)],
        out_specs=pl.BlockSpec((te * SUBLANES, LANES), lambda t, te_r, nv_r, src_r: (t, 0)),
        scratch_shapes=[pltpu.VMEM((2, te * SUBLANES, LANES), F32), pltpu.SemaphoreType.DMA((2,))],
    )
    return pl.pallas_call(
        _expert_kernel,
        grid_spec=grid_spec,
        out_shape=jax.ShapeDtypeStruct((n_tiles * te * SUBLANES, LANES), F32),
        compiler_params=_cparams("arbitrary"),
        name="moe_experts",
    )(tile_expert, n_valid, src_tok, h2, w_gate, w_up, w_down)


def _combine_kernel(d0_ref, d1_ref, x_ref, rw_ref, gf_ref, ys_hbm, *rest, final):
    if final:
        xo_ref, y_ref, buf, sem = rest
    else:
        xo_ref, buf, sem = rest
    t = pl.program_id(0)
    nt = pl.num_programs(0)
    tm = x_ref.shape[0]

    def issue(tile, slot):
        def body(r, carry):
            _row_copy(ys_hbm, d0_ref[tile * tm + r], buf, slot, r, sem).start()
            _row_copy(ys_hbm, d1_ref[tile * tm + r], buf, slot, tm + r, sem).start()
            return carry
        lax.fori_loop(0, tm, body, 0, unroll=8)

    def wait_all(slot):
        pltpu.make_async_copy(ys_hbm.at[pl.ds(0, 2 * tm * SUBLANES)], buf.at[slot], sem.at[slot]).wait()

    @pl.when(t == 0)
    def _():
        issue(0, 0)

    @pl.when(t + 1 < nt)
    def _():
        issue(t + 1, (t + 1) % 2)

    slot = t % 2
    wait_all(slot)
    rw = rw_ref[...]
    xo = (x_ref[...] + rw[:, 0:1] * _load_token_tiles(buf.at[slot], 0, tm)
          + rw[:, 1:2] * _load_token_tiles(buf.at[slot], tm * SUBLANES, tm))
    xo_ref[...] = xo
    if final:
        y_ref[...] = _rmsnorm(xo, gf_ref[...])


def _combine(d0, d1, x, rw, gf, ys, final):
    n, d = x.shape
    tm = COMBINE_TILE
    row = lambda w: pl.BlockSpec((tm, w), lambda t, a, b: (t, 0))
    n_out = 2 if final else 1
    grid_spec = pltpu.PrefetchScalarGridSpec(
        num_scalar_prefetch=2,
        grid=(n // tm,),
        in_specs=[row(d), row(LANES), pl.BlockSpec((1, d), lambda t, a, b: (0, 0)), pl.BlockSpec(memory_space=pl.ANY)],
        out_specs=[row(d)] * n_out,
        scratch_shapes=[pltpu.VMEM((2, 2 * tm * SUBLANES, LANES), F32), pltpu.SemaphoreType.DMA((2,))],
    )
    return pl.pallas_call(
        functools.partial(_combine_kernel, final=final),
        grid_spec=grid_spec,
        out_shape=[jax.ShapeDtypeStruct((n, d), F32)] * n_out,
        compiler_params=_cparams("arbitrary"),
        name="moe_combine_final" if final else "moe_combine",
    )(d0, d1, x, rw, gf, ys)


def _route_plan(ri, cnt, n_tiles):
    te = EXPERT_TILE
    n = ri.shape[1]
    counts = cnt[0, :N_EXPERTS].astype(jnp.int32)
    padded = ((counts + te - 1) // te) * te
    ends = jnp.cumsum(padded)
    offs = ends - padded
    d0 = offs[ri[0]] + ri[2]
    d1 = offs[ri[1]] + ri[3]
    tok = jnp.arange(n, dtype=jnp.int32)
    src = jnp.zeros((n_tiles * te,), jnp.int32).at[jnp.concatenate([d0, d1])].set(jnp.concatenate([tok, tok]))
    tile_start = jnp.arange(n_tiles, dtype=jnp.int32) * te
    tile_expert = jnp.minimum(jnp.sum((ends[None, :] <= tile_start[:, None]).astype(jnp.int32), axis=1),
                              N_EXPERTS - 1)
    n_valid = (ends[-1:] // te).astype(jnp.int32)
    return d0, d1, src, tile_expert, n_valid


def _rope_tables(pos):
    half = HEAD_DIM // 2
    inv = ROPE_THETA ** (-jnp.arange(half, dtype=F32) / half)
    ang = pos.astype(F32)[:, None] * inv[None, :]
    cos = jnp.cos(ang)
    sin = jnp.sin(ang)
    cos_t = jnp.concatenate([cos, cos] * (LANES // HEAD_DIM), axis=1)
    sin_t = jnp.concatenate([-sin, sin] * (LANES // HEAD_DIM), axis=1)
    return cos_t, sin_t


def _block_diag(w):
    nb, bw, _ = w.shape
    out = jnp.zeros((nb * bw, nb * bw), w.dtype)
    for i in range(nb):
        out = out.at[i * bw:(i + 1) * bw, i * bw:(i + 1) * bw].set(w[i])
    return out


def kernel(x_prompt, x_sample, cache_kv_cmp, cache_kv_sel, cache_kv_win, state_lru_h, state_lru_conv, state_sconv, page_table, norm1_g, w_in, lru_conv_w, lru_conv_b, lru_wa, lru_ba, lru_wx, lru_bx, lru_lambda, nsa_cmp_wk, nsa_cmp_wv, sc_conv_w, w_out, norm2_g, router_group_w, router_group_b, router_exp_w, router_exp_b, exp_w_gate, exp_w_up, exp_w_down, norm_f_g):
    bsz, seq, d = x_prompt.shape
    dbs, t_len, _ = x_sample.shape
    depth = w_in.shape[0]
    d_a = lru_conv_w.shape[2]
    d_c = sc_conv_w.shape[2]
    kv_w = 2 * N_KV * HEAD_DIM
    d_b = N_HEADS * HEAD_DIM
    n_gate = 3 * N_HEADS
    n_p = bsz * seq
    n_s = dbs * t_len
    n = n_p + n_s
    assert n_s == TOKEN_TILE and n_p % TOKEN_TILE == 0 and seq % TOKEN_TILE == 0
    page = cache_kv_cmp.shape[2]
    past = page_table.shape[1] * page
    n_buf = cache_kv_win.shape[2]

    pos = jnp.concatenate([jnp.arange(seq, dtype=jnp.int32),
                           jnp.tile(past + jnp.arange(t_len, dtype=jnp.int32), dbs)])
    cos_t, sin_t = _rope_tables(pos)

    pool_c = cache_kv_cmp.reshape(depth, -1, page, kv_w)
    pool_s = cache_kv_sel.reshape(depth, -1, page, kv_w)
    win = cache_kv_win.reshape(depth, dbs, n_buf, kv_w)

    g_off = 2 * d_a + d_b + 3 * kv_w
    x = jnp.concatenate([x_prompt.reshape(n_p, d), x_sample.reshape(n_s, d)], axis=0)
    h0_p = jnp.zeros((bsz, 1, d_a), F32)
    lb0_p = jnp.zeros((bsz, lru_conv_w.shape[1] - 1, d_a), F32)
    sb0_p = jnp.zeros((bsz, sc_conv_w.shape[1] - 1, d_c), F32)

    n_tiles = (2 * n) // EXPERT_TILE + N_EXPERTS
    proj_parts = lambda l: 2 if l < depth - 1 else 1
    states_p, states_s = [], []
    y = None
    for l in range(depth):
        w_l = w_in[l]
        w_r = _split_weight(jnp.concatenate([w_l[:, :g_off], w_l[:, g_off + n_gate:], w_l[:, g_off:g_off + n_gate],
                                             jnp.zeros((d, LANES - n_gate), F32)], axis=1), proj_parts(l))
        lru_in, sc_in, gate, q, kvc, kvs, kvw, kvs_b, kvw_b = _in_proj(
            x, norm1_g[l][None], w_r, cos_t, sin_t, n_p, seq, d_a=d_a, d_b=d_b, kv_w=kv_w, d_c=d_c)

        seq_w = (lru_conv_w[l], lru_conv_b[l][None], _block_diag(lru_wa[l]).astype(MXU_DTYPE), lru_ba[l][None],
                 _block_diag(lru_wx[l]).astype(MXU_DTYPE), lru_bx[l][None], lru_lambda[l][None], sc_conv_w[l])
        out_a, out_c, hn_p, lbn_p, sbn_p = _seq_mix_prompt(lru_in, sc_in, h0_p, lb0_p, sb0_p, seq_w, bsz, seq,
                                                           d_a=d_a, d_c=d_c)
        tmaj = lambda a: jnp.swapaxes(a.reshape(dbs, t_len, -1), 0, 1)
        oa_s, oc_s, hn_s, lbn_s, sbn_s = _seq_mix_sample(
            tmaj(lru_in[n_p:]), tmaj(sc_in[n_p:]), state_lru_h[l], jnp.swapaxes(state_lru_conv[l], 0, 1),
            jnp.swapaxes(state_sconv[l], 0, 1), seq_w, d_a=d_a, d_c=d_c)
        oa_s = jnp.swapaxes(oa_s, 0, 1).reshape(n_s, d_a)
        oc_s = jnp.swapaxes(oc_s, 0, 1).reshape(n_s, d_c)

        w_cmp = jnp.concatenate([jnp.broadcast_to(nsa_cmp_wk[l][:, None], (CMP_BLOCK, kv_w // 2)),
                                 jnp.broadcast_to(nsa_cmp_wv[l][:, None], (CMP_BLOCK, kv_w // 2))], axis=1)
        kcv = _compress_prompt(kvc, w_cmp, bsz, seq)
        out_b = _nsa_prompt(q, gate, kcv, kvs_b, kvw_b, bsz, seq)
        s3 = lambda a: a[n_p:].reshape(dbs, t_len, -1)
        ob_s, nwin_s = _nsa_sample(page_table, s3(q), s3(gate), s3(kvs), s3(kvw), w_cmp, pool_c, pool_s, win, l)
        ob_s = ob_s.reshape(n_s, d_b)

        w_route = jnp.concatenate([router_group_w[l], router_exp_w[l],
                                   jnp.zeros((d, LANES - N_GROUPS - N_EXPERTS), F32)], axis=1)
        b_route = jnp.concatenate([router_group_b[l], router_exp_b[l],
                                   jnp.zeros((LANES - N_GROUPS - N_EXPERTS,), F32)])[None]
        xn, h2, ri, rw, cnt = _out_proj_router(x, (out_a, out_b, out_c), (oa_s, ob_s, oc_s),
                                               _split_weight(w_out[l], proj_parts(l)), norm2_g[l][None],
                                               w_route, b_route)
        d0, d1, src, tile_expert, n_valid = _route_plan(ri, cnt, n_tiles)
        ys = _experts(tile_expert, n_valid, src, h2, exp_w_gate, exp_w_up, exp_w_down, l)
        final = l == depth - 1
        outs = _combine(d0, d1, xn, rw, norm_f_g[None], ys, final)
        x = outs[0]
        if final:
            y = outs[1]

        kv6 = lambda a, lead: a.reshape(lead + (2, N_KV, HEAD_DIM))
        states_p.append((kv6(kvc[:n_p], (bsz, seq)), kv6(kvs[:n_p], (bsz, seq)),
                         kv6(kvw[:n_p].reshape(bsz, seq, kv_w)[:, seq - min(WINDOW, seq):], (bsz, min(WINDOW, seq))),
                         hn_p[:, 0], lbn_p, sbn_p))
        states_s.append((kv6(kvc[n_p:], (dbs, t_len)), kv6(kvs[n_p:], (dbs, t_len)), kv6(nwin_s, (dbs, n_buf)),
                         hn_s, jnp.swapaxes(lbn_s, 0, 1), jnp.swapaxes(sbn_s, 0, 1)))

    stack = lambda sts, i: jnp.stack([s[i] for s in sts])
    res = [y[:n_p].reshape(bsz, seq, d), y[n_p:].reshape(dbs, t_len, d)]
    for i in range(6):
        res += [stack(states_p, i), stack(states_s, i)]
    return tuple(res)
```

```python
import functools

import jax
import jax.numpy as jnp
from jax import lax
from jax.experimental import pallas as pl
from jax.experimental.pallas import tpu as pltpu

F32 = jnp.float32
MXU_DTYPE = jnp.bfloat16

HEAD_DIM = 64
N_HEADS = 8
N_KV = 2
GROUP = N_HEADS // N_KV
CMP_BLOCK = 32
SEL_BLOCK = 64
TOP_N = 16
WINDOW = 512
Q_BLOCK = 128
ROPE_THETA = 10000.0
LRU_C = 8.0
N_GROUPS = 4
EXP_PER_GROUP = 8
N_EXPERTS = N_GROUPS * EXP_PER_GROUP
RMS_EPS = 1e-6
NEG_INF = -1e30
TINY = 1e-30
FORCE = 1e6

LANES = 128
SUBLANES = 8
SUM_ROWS = 2 * SUBLANES
VMEM_LIMIT = 56 * 2 ** 20
TOKEN_TILE = 512
TIME_CHUNK = 256
KEY_TILE = 512
EXPERT_TILE = 256
COMBINE_TILE = 256

_NT = (((1,), (1,)), ((), ()))


def _cparams(*sem):
    return pltpu.CompilerParams(dimension_semantics=sem, vmem_limit_bytes=VMEM_LIMIT)


def _split_weight(w, parts):
    if parts == 1:
        return w.astype(MXU_DTYPE)[None]
    hi = lax.bitcast_convert_type(lax.bitcast_convert_type(w, jnp.uint32) & jnp.uint32(0xFFFF0000), F32)
    return jnp.stack([hi.astype(MXU_DTYPE), (w - hi).astype(MXU_DTYPE)])


def _split_act(a, parts):
    if parts == 1:
        return (a.astype(MXU_DTYPE),)
    hi = lax.bitcast_convert_type(lax.bitcast_convert_type(a, jnp.uint32) & jnp.uint32(0xFFFF0000), F32)
    return hi.astype(MXU_DTYPE), (a - hi).astype(MXU_DTYPE)


def _split_dot(a_parts, w_ref, cols=slice(None)):
    dot = lambda a, p: jnp.dot(a, w_ref[p, :, cols], preferred_element_type=F32)
    if len(a_parts) == 1:
        return dot(a_parts[0], 0)
    return (dot(a_parts[1], 1) + (dot(a_parts[0], 1) + dot(a_parts[1], 0))) + dot(a_parts[0], 0)


def _rmsnorm(x, g):
    return x * lax.rsqrt(jnp.mean(x * x, axis=-1, keepdims=True) + RMS_EPS) * g


def _swap_halves(x):
    w = x.shape[-1]
    lane = lax.broadcasted_iota(jnp.int32, x.shape, x.ndim - 1)
    first = (lane % HEAD_DIM) < HEAD_DIM // 2
    return jnp.where(first, pltpu.roll(x, w - HEAD_DIM // 2, axis=1), pltpu.roll(x, HEAD_DIM // 2, axis=1))


def _masked_softmax(s, mask):
    s = jnp.where(mask, s, NEG_INF)
    m = jnp.max(s, axis=-1, keepdims=True)
    p = jnp.where(mask, jnp.exp(s - m), 0.0)
    return p / jnp.maximum(jnp.sum(p, axis=-1, keepdims=True), TINY)


def _select_blocks_keys(imp, pos_q, n_sel):
    blk = lax.broadcasted_iota(jnp.int32, imp.shape, 0)
    cur = pos_q // SEL_BLOCK
    forced = (blk == 0) | (blk == cur) | (blk == cur - 1)
    w = jnp.where(forced, FORCE, imp)
    w = jnp.where(blk <= cur, w, NEG_INF)
    w = jnp.where(blk < n_sel, w, -jnp.inf)
    blkf = blk.astype(F32)

    def body(_, carry):
        w, sel = carry
        m = jnp.max(w, axis=0, keepdims=True)
        first = jnp.min(jnp.where(w == m, blkf, 1e9), axis=0, keepdims=True)
        pick = blkf == first
        return jnp.where(pick, -jnp.inf, w), jnp.where(pick, 1.0, sel)

    _, sel = lax.fori_loop(0, min(TOP_N, n_sel), body, (w, jnp.zeros_like(w)))
    return sel


def _in_proj_kernel(x_ref, g_ref, w_ref, cos_ref, sin_ref,
                    lru_ref, sc_ref, gate_ref, q_ref, kvc_ref, kvs_ref, kvw_ref, kvsb_ref, kvwb_ref,
                    *, d_a, d_b, kv_w, d_c):
    h_parts = _split_act(_rmsnorm(x_ref[...], g_ref[...]), w_ref.shape[0])

    def mm(a, b):
        return _split_dot(h_parts, w_ref, slice(a, b))

    cos = cos_ref[...]
    sin = sin_ref[...]
    off = 2 * d_a
    lru_ref[...] = mm(0, off)
    q = mm(off, off + d_b)
    reps = d_b // LANES
    cos_q = jnp.concatenate([cos] * reps, axis=1)
    sin_q = jnp.concatenate([sin] * reps, axis=1)
    q_ref[...] = (q * cos_q + _swap_halves(q) * sin_q) * (HEAD_DIM ** -0.5)
    off += d_b
    for ref, bref in ((kvc_ref, None), (kvs_ref, kvsb_ref), (kvw_ref, kvwb_ref)):
        kv = mm(off, off + kv_w)
        k = kv[:, :kv_w // 2]
        kv = jnp.concatenate([k * cos + _swap_halves(k) * sin, kv[:, kv_w // 2:]], axis=1)
        ref[...] = kv
        if bref is not None:
            bref[...] = kv.astype(MXU_DTYPE)
        off += kv_w
    sc_ref[...] = mm(off, off + 3 * d_c)
    off += 3 * d_c
    gate_ref[...] = jax.nn.sigmoid(mm(off, off + LANES))


def _in_proj(x, g1, w_r, cos_t, sin_t, n_prompt, seq, *, d_a, d_b, kv_w, d_c):
    n, d = x.shape
    tm = TOKEN_TILE
    n_ptiles = n_prompt // tm
    n_stiles = seq // tm

    def tab_map(i):
        return (jnp.where(i < n_ptiles, i % n_stiles, n_stiles), 0)

    row = lambda w: pl.BlockSpec((tm, w), lambda i: (i, 0))
    widths = (2 * d_a, 3 * d_c, LANES, d_b, kv_w, kv_w, kv_w, kv_w, kv_w)
    dtypes = (F32, F32, F32, F32, F32, F32, F32, MXU_DTYPE, MXU_DTYPE)
    return pl.pallas_call(
        functools.partial(_in_proj_kernel, d_a=d_a, d_b=d_b, kv_w=kv_w, d_c=d_c),
        grid=(n // tm,),
        in_specs=[row(d),
                  pl.BlockSpec((1, d), lambda i: (0, 0)),
                  pl.BlockSpec(w_r.shape, lambda i: (0, 0, 0)),
                  pl.BlockSpec((tm, LANES), tab_map),
                  pl.BlockSpec((tm, LANES), tab_map)],
        out_specs=[row(w) for w in widths],
        out_shape=[jax.ShapeDtypeStruct((n, w), dt) for w, dt in zip(widths, dtypes)],
        compiler_params=_cparams("parallel"),
        name="in_proj",
    )(x, g1, w_r, cos_t, sin_t)


def _scan_rows(a, u):
    t = a.shape[0]
    row = lax.broadcasted_iota(jnp.int32, a.shape, 0)
    d = 1
    while d < t:
        keep = row >= d
        a_sh = jnp.where(keep, pltpu.roll(a, d, axis=0), 1.0)
        u_sh = jnp.where(keep, pltpu.roll(u, d, axis=0), 0.0)
        u = a * u_sh + u
        a = a * a_sh
        d *= 2
    return a, u


def _lru_gates(xc, wa_ref, ba_ref, wx_ref, bx_ref, lam_ref):
    xb = xc.astype(MXU_DTYPE)
    r = jax.nn.sigmoid(jnp.dot(xb, wa_ref[...], preferred_element_type=F32) + ba_ref[...])
    i = jax.nn.sigmoid(jnp.dot(xb, wx_ref[...], preferred_element_type=F32) + bx_ref[...])
    log_a = -LRU_C * r * jax.nn.softplus(-lam_ref[...])
    a = jnp.exp(log_a)
    th = jnp.tanh(log_a)
    u = jnp.sqrt(-2.0 * th / (1.0 - th)) * (i * xc)
    return a, u


def _seq_mix_kernel(lru_ref, sc_ref, h0_ref, lb0_ref, sb0_ref, cw_ref, cb_ref, wa_ref, ba_ref, wx_ref, bx_ref,
                    lam_ref, scw_ref, oa_ref, oc_ref, hn_ref, lbn_ref, sbn_ref, xpad, vpad, hcar, *, d_a, d_c):
    c = pl.program_id(1)
    tc = lru_ref.shape[0]
    nlb = lb0_ref.shape[1]
    nsb = sb0_ref.shape[1]

    @pl.when(c == 0)
    def _():
        xpad[0:8, :] = jnp.zeros((8, d_a), F32)
        xpad[8 - nlb:8, :] = lb0_ref[0]
        vpad[0:8, :] = jnp.zeros((8, d_c), F32)
        vpad[8 - nsb:8, :] = sb0_ref[0]
        hcar[...] = h0_ref[0]

    xa = lru_ref[:, 0:d_a]
    ga = lru_ref[:, d_a:2 * d_a]
    xpad[8:8 + tc, :] = xa
    xc = cw_ref[0:1, :] * xpad[8 - nlb:8 - nlb + tc, :]
    for j in range(1, nlb):
        xc = xc + cw_ref[j:j + 1, :] * xpad[8 - nlb + j:8 - nlb + j + tc, :]
    xc = xc + cw_ref[nlb:nlb + 1, :] * xa + cb_ref[...]
    a, u = _lru_gates(xc, wa_ref, ba_ref, wx_ref, bx_ref, lam_ref)
    a_cum, hs = _scan_rows(a, u)
    hs = hs + a_cum * hcar[...]
    hcar[...] = hs[tc - 1:tc, :]
    oa_ref[...] = hs * jax.nn.gelu(ga)

    v = sc_ref[:, 0:d_c] * sc_ref[:, 2 * d_c:3 * d_c]
    vpad[8:8 + tc, :] = v
    uc = scw_ref[0:1, :] * vpad[8 - nsb:8 - nsb + tc, :]
    for j in range(1, nsb):
        uc = uc + scw_ref[j:j + 1, :] * vpad[8 - nsb + j:8 - nsb + j + tc, :]
    uc = uc + scw_ref[nsb:nsb + 1, :] * v
    oc_ref[...] = sc_ref[:, d_c:2 * d_c] * uc

    hn_ref[0] = hs[tc - 1:tc, :]
    lbn_ref[0] = xpad[8 + tc - nlb:8 + tc, :]
    sbn_ref[0] = vpad[8 + tc - nsb:8 + tc, :]
    xpad[0:8, :] = xpad[tc:tc + 8, :]
    vpad[0:8, :] = vpad[tc:tc + 8, :]


def _seq_mix_prompt(lru_in, sc_in, h0, lb0, sb0, wts, bsz, seq, *, d_a, d_c):
    n = bsz * seq
    tc = TIME_CHUNK
    nch = seq // tc
    row = lambda w: pl.BlockSpec((tc, w), lambda b, c: (b * nch + c, 0))
    full = lambda a: pl.BlockSpec(a.shape, lambda b, c: (0,) * a.ndim)
    state = lambda r, w: pl.BlockSpec((1, r, w), lambda b, c: (b, 0, 0))
    nlb, nsb = lb0.shape[1], sb0.shape[1]
    return pl.pallas_call(
        functools.partial(_seq_mix_kernel, d_a=d_a, d_c=d_c),
        grid=(bsz, nch),
        in_specs=[row(2 * d_a), row(3 * d_c), state(1, d_a), state(nlb, d_a), state(nsb, d_c)]
                 + [full(w) for w in wts],
        out_specs=[row(d_a), row(d_c), state(1, d_a), state(nlb, d_a), state(nsb, d_c)],
        out_shape=[jax.ShapeDtypeStruct((n, d_a), F32), jax.ShapeDtypeStruct((n, d_c), F32),
                   jax.ShapeDtypeStruct((bsz, 1, d_a), F32), jax.ShapeDtypeStruct((bsz, nlb, d_a), F32),
                   jax.ShapeDtypeStruct((bsz, nsb, d_c), F32)],
        scratch_shapes=[pltpu.VMEM((tc + 8, d_a), F32), pltpu.VMEM((tc + 8, d_c), F32), pltpu.VMEM((1, d_a), F32)],
        compiler_params=_cparams("arbitrary", "arbitrary"),
        name="seq_mix_prompt",
    )(lru_in, sc_in, h0, lb0, sb0, *wts)


def _seq_mix_sample_kernel(lru_ref, sc_ref, h0_ref, lb0_ref, sb0_ref, cw_ref, cb_ref, wa_ref, ba_ref, wx_ref,
                           bx_ref, lam_ref, scw_ref, oa_ref, oc_ref, hn_ref, lbn_ref, sbn_ref, *, d_a, d_c):
    t_len = lru_ref.shape[0]
    nlb = lb0_ref.shape[0]
    nsb = sb0_ref.shape[0]
    xs = [lb0_ref[j] for j in range(nlb)] + [lru_ref[t][:, 0:d_a] for t in range(t_len)]
    vs = [sb0_ref[j] for j in range(nsb)] + [sc_ref[t][:, 0:d_c] * sc_ref[t][:, 2 * d_c:3 * d_c] for t in range(t_len)]
    h = h0_ref[...]
    for t in range(t_len):
        xc = cw_ref[0:1, :] * xs[t]
        for j in range(1, nlb + 1):
            xc = xc + cw_ref[j:j + 1, :] * xs[t + j]
        xc = xc + cb_ref[...]
        a, u = _lru_gates(xc, wa_ref, ba_ref, wx_ref, bx_ref, lam_ref)
        h = a * h + u
        oa_ref[t] = h * jax.nn.gelu(lru_ref[t][:, d_a:2 * d_a])
        uc = scw_ref[0:1, :] * vs[t]
        for j in range(1, nsb + 1):
            uc = uc + scw_ref[j:j + 1, :] * vs[t + j]
        oc_ref[t] = sc_ref[t][:, d_c:2 * d_c] * uc
    hn_ref[...] = h
    for j in range(nlb):
        lbn_ref[j] = xs[t_len + j]
    for j in range(nsb):
        sbn_ref[j] = vs[t_len + j]


def _seq_mix_sample(lru_t, sc_t, h0, lb0_t, sb0_t, wts, *, d_a, d_c):
    t_len, bsz, _ = lru_t.shape
    outs = [jax.ShapeDtypeStruct((t_len, bsz, d_a), F32), jax.ShapeDtypeStruct((t_len, bsz, d_c), F32),
            jax.ShapeDtypeStruct(h0.shape, F32), jax.ShapeDtypeStruct(lb0_t.shape, F32),
            jax.ShapeDtypeStruct(sb0_t.shape, F32)]
    return pl.pallas_call(
        functools.partial(_seq_mix_sample_kernel, d_a=d_a, d_c=d_c),
        out_shape=outs,
        compiler_params=pltpu.CompilerParams(vmem_limit_bytes=VMEM_LIMIT),
        name="seq_mix_sample",
    )(lru_t, sc_t, h0, lb0_t, sb0_t, *wts)


def _compress_kernel(kvc_ref, w_ref, out_ref):
    half = out_ref.shape[0] // 2
    x = kvc_ref[...].reshape(half, 2 * CMP_BLOCK, out_ref.shape[1])
    w = w_ref[...][None]
    out_ref[0:half, :] = jnp.sum(x[:, 0:CMP_BLOCK, :] * w, axis=1)
    out_ref[half:2 * half, :] = jnp.sum(x[:, CMP_BLOCK:2 * CMP_BLOCK, :] * w, axis=1)


def _compress_prompt(kvc, w_cmp, bsz, seq):
    kv_w = kvc.shape[1]
    n_cmp = seq // CMP_BLOCK
    return pl.pallas_call(
        _compress_kernel,
        grid=(bsz,),
        in_specs=[pl.BlockSpec((seq, kv_w), lambda b: (b, 0)), pl.BlockSpec(w_cmp.shape, lambda b: (0, 0))],
        out_specs=pl.BlockSpec((n_cmp, kv_w), lambda b: (b, 0)),
        out_shape=jax.ShapeDtypeStruct((bsz * n_cmp, kv_w), F32),
        compiler_params=_cparams("parallel"),
        name="nsa_compress",
    )(kvc, w_cmp)


def _pad_heads(q, lhs_ref, rows):
    lane = lax.broadcasted_iota(jnp.int32, (rows, LANES), 1)
    for h in range(N_HEADS):
        k = h // GROUP
        slab = q[:, (h // 2) * LANES:(h // 2 + 1) * LANES]
        if h % 2 != k:
            slab = pltpu.roll(slab, HEAD_DIM, axis=1)
        keep = (lane >= k * HEAD_DIM) & (lane < (k + 1) * HEAD_DIM)
        lhs_ref[h * rows:(h + 1) * rows, 0:LANES] = jnp.where(keep, slab, 0.0).astype(MXU_DTYPE)


def _gated_output(gate, o_c, o_s, o_w, rows):
    lane = lax.broadcasted_iota(jnp.int32, (rows, LANES), 1)
    slabs = []
    for m in range(N_HEADS // 2):
        parts = []
        for h in (2 * m, 2 * m + 1):
            k = h // GROUP
            sl = slice(h * rows, (h + 1) * rows)
            r = (gate[:, 3 * h:3 * h + 1] * o_c[sl] + gate[:, 3 * h + 1:3 * h + 2] * o_s[sl]
                 + gate[:, 3 * h + 2:3 * h + 3] * o_w[sl])
            if h % 2 != k:
                r = pltpu.roll(r, HEAD_DIM, axis=1)
            parts.append(r)
        slabs.append(jnp.where(lane < HEAD_DIM, parts[0], parts[1]))
    return slabs


def _compressed_branch(qpad, kcv, pos0, rows, n_sel):
    n_cmp = kcv.shape[0]
    half = n_cmp // 2
    pos_q = pos0 + lax.broadcasted_iota(jnp.int32, (rows, 1), 0)
    kc = kcv[:, 0:LANES].astype(MXU_DTYPE)
    vc = kcv[:, LANES:2 * LANES].astype(MXU_DTYPE)
    s_c = lax.dot_general(qpad, kc, _NT, preferred_element_type=F32)
    col = lax.broadcasted_iota(jnp.int32, (rows, n_cmp), 1)
    blk = jnp.where(col < half, 2 * col, 2 * (col - half) + 1)
    m_c = (blk + 1) * CMP_BLOCK - 1 <= pos_q
    ps = []
    imp = [jnp.zeros((rows, n_cmp), F32) for _ in range(N_KV)]
    for h in range(N_HEADS):
        p = _masked_softmax(s_c[h * rows:(h + 1) * rows], m_c)
        imp[h // GROUP] = imp[h // GROUP] + p
        ps.append(p.astype(MXU_DTYPE))
    o_c = jnp.dot(jnp.concatenate(ps, axis=0), vc, preferred_element_type=F32)
    imps = []
    for k in range(N_KV):
        imp_s = imp[k][:, 0:half] + imp[k][:, half:n_cmp]
        if half < LANES:
            imp_s = jnp.concatenate([imp_s, jnp.zeros((rows, LANES - half), F32)], axis=1)
        imps.append(imp_s)
    imp_t = jnp.concatenate(imps + [jnp.zeros((LANES - N_KV * rows, LANES), F32)], axis=0).T
    n_rows = -(-n_sel // 8) * 8
    pos_l = pos0 + lax.broadcasted_iota(jnp.int32, (1, LANES), 1) % rows
    sel_t = _select_blocks_keys(imp_t[0:n_rows], pos_l, n_sel)
    sel = jnp.concatenate([sel_t, jnp.zeros((LANES - n_rows, LANES), F32)], axis=0).T
    return o_c, [sel[k * rows:(k + 1) * rows] for k in range(N_KV)]


def _store_selection(lhs_ref, sels, rows):
    for k in range(N_KV):
        neg = jnp.where(sels[k] > 0.0, 0.0, NEG_INF).astype(MXU_DTYPE)
        for g in range(GROUP):
            h = k * GROUP + g
            lhs_ref[h * rows:(h + 1) * rows, LANES:2 * LANES] = neg


def _block_onehot(n_rows):
    row = lax.broadcasted_iota(jnp.int32, (n_rows, LANES), 0)
    lane = lax.broadcasted_iota(jnp.int32, (n_rows, LANES), 1)
    return jnp.where(row // SEL_BLOCK == lane, 1.0, 0.0).astype(MXU_DTYPE)


def _masked_softmax_keys(s, mask):
    s = jnp.where(mask, s, NEG_INF)
    m = jnp.max(s, axis=0, keepdims=True)
    p = jnp.where(mask, jnp.exp(s - m), 0.0)
    return p / jnp.maximum(jnp.sum(p, axis=0, keepdims=True), TINY)


def _nsa_prompt_kernel(q_ref, gate_ref, kcv_ref, kvs_ref, kvw_ref, out_ref,
                       kaug, vst, vwt, vct, lhs, m_sc, acc_sc, s_a, s_b, *, seq):
    qb = pl.program_id(1)
    rows = Q_BLOCK
    tk = KEY_TILE
    s0 = qb * rows
    n_sel = seq // SEL_BLOCK
    n_cmp = seq // CMP_BLOCK
    cols = N_HEADS * rows

    @pl.when(qb == 0)
    def _():
        kaug[:, 0:LANES] = kvs_ref[:, 0:LANES]
        kaug[:, LANES:2 * LANES] = _block_onehot(seq)
        vct[...] = kcv_ref[:, LANES:2 * LANES].T.astype(MXU_DTYPE)

        def tr_sel(i, c):
            r0 = pl.multiple_of(i * tk, tk)
            ones_row = jnp.where(lax.broadcasted_iota(jnp.int32, (SUM_ROWS, tk), 0) == 0, 1.0, 0.0)
            v_t = kvs_ref[pl.ds(r0, tk), LANES:2 * LANES].astype(F32).T
            vst[i] = jnp.concatenate([v_t, ones_row], axis=0).astype(MXU_DTYPE)
            return c

        def tr_win(i, c):
            r0 = pl.multiple_of(i * rows, rows)
            vwt[i] = kvw_ref[pl.ds(r0, rows), LANES:2 * LANES].astype(F32).T.astype(MXU_DTYPE)
            return c

        lax.fori_loop(0, seq // tk, tr_sel, 0)
        lax.fori_loop(0, seq // rows, tr_win, 0)

    _pad_heads(q_ref[...], lhs, rows)
    qpad = lhs[:, 0:LANES]
    lane = lax.broadcasted_iota(jnp.int32, (1, cols), 1)
    pos_q = s0 + lane % rows

    half = n_cmp // 2
    s_c = lax.dot_general(kcv_ref[:, 0:LANES].astype(MXU_DTYPE), qpad, _NT, preferred_element_type=F32)
    r_c = lax.broadcasted_iota(jnp.int32, (n_cmp, cols), 0)
    blk_c = jnp.where(r_c < half, 2 * r_c, 2 * (r_c - half) + 1)
    p_c = _masked_softmax_keys(s_c, (blk_c + 1) * CMP_BLOCK - 1 <= pos_q)
    o_c = jnp.dot(vct[...], p_c.astype(MXU_DTYPE), preferred_element_type=F32)

    imp = []
    for k in range(N_KV):
        acc = p_c[:, k * GROUP * rows:(k * GROUP + 1) * rows]
        for g in range(1, GROUP):
            acc = acc + p_c[:, (k * GROUP + g) * rows:(k * GROUP + g + 1) * rows]
        imp.append(acc[0:half] + acc[half:n_cmp])
    sel = _select_blocks_keys(jnp.concatenate(imp, axis=1), pos_q[:, 0:N_KV * rows], n_sel)
    for k in range(N_KV):
        sel_k = sel[:, k * rows:(k + 1) * rows]
        if n_sel < LANES:
            sel_k = jnp.concatenate([sel_k, jnp.zeros((LANES - n_sel, rows), F32)], axis=0)
        neg = jnp.where(sel_k.T > 0.0, 0.0, NEG_INF).astype(MXU_DTYPE)
        for g in range(GROUP):
            h = k * GROUP + g
            lhs[h * rows:(h + 1) * rows, LANES:2 * LANES] = neg

    m_sc[...] = jnp.full(m_sc.shape, NEG_INF, F32)
    acc_sc[...] = jnp.zeros(acc_sc.shape, F32)

    last_tile = seq // tk - 1

    def scores(s_ref, kt):
        k0 = pl.multiple_of(jnp.minimum(kt, last_tile) * tk, tk)
        s_ref[...] = lax.dot_general(kaug[pl.ds(k0, tk), :], lhs[...], _NT, preferred_element_type=F32)

    def update(s_ref, kt, causal):
        s = s_ref[...]
        if causal:
            s = jnp.where(kt * tk + lax.broadcasted_iota(jnp.int32, (tk, cols), 0) <= pos_q, s, NEG_INF)
        m_old = m_sc[...]
        m_new = jnp.maximum(m_old, jnp.max(s, axis=0, keepdims=True))
        alpha = jnp.exp(m_old - m_new)
        p = jnp.exp(s - m_new)
        m_sc[...] = m_new
        vt = vst[jnp.minimum(kt, last_tile)]
        acc_sc[...] = alpha * acc_sc[...] + jnp.dot(vt, p.astype(MXU_DTYPE), preferred_element_type=F32)

    n_pairs = (s0 // tk) // 2
    scores(s_a, 0)

    def pair_body(i, carry):
        scores(s_b, 2 * i + 1)
        update(s_a, 2 * i, False)
        scores(s_a, 2 * i + 2)
        update(s_b, 2 * i + 1, False)
        return carry

    lax.fori_loop(0, n_pairs, pair_body, 0)
    t0 = 2 * n_pairs
    scores(s_b, t0 + 1)
    update(s_a, t0, True)

    @pl.when((t0 + 1) * tk <= s0 + rows - 1)
    def _():
        update(s_b, t0 + 1, True)

    o_s = acc_sc[0:LANES, :] / jnp.maximum(acc_sc[LANES:LANES + 1, :], TINY)

    band = WINDOW + rows
    start = pl.multiple_of(jnp.maximum(s0 - WINDOW, 0), rows)
    s_w = lax.dot_general(kvw_ref[pl.ds(start, band), 0:LANES], qpad, _NT, preferred_element_type=F32)
    dlt = pos_q - (start + lax.broadcasted_iota(jnp.int32, (band, cols), 0))
    p_w = _masked_softmax_keys(s_w, (dlt >= 0) & (dlt <= WINDOW))
    t0 = start // rows
    vw = jnp.concatenate([vwt[t0 + j] for j in range(band // rows)], axis=1)
    o_w = jnp.dot(vw, p_w.astype(MXU_DTYPE), preferred_element_type=F32)

    g_t = gate_ref[...].T
    for m in range(N_HEADS // 2):
        parts = []
        for h in (2 * m, 2 * m + 1):
            k = h // GROUP
            rs = slice(k * HEAD_DIM, (k + 1) * HEAD_DIM)
            ls = slice(h * rows, (h + 1) * rows)
            parts.append(g_t[3 * h:3 * h + 1] * o_c[rs, ls] + g_t[3 * h + 1:3 * h + 2] * o_s[rs, ls]
                         + g_t[3 * h + 2:3 * h + 3] * o_w[rs, ls])
        out_ref[:, m * LANES:(m + 1) * LANES] = jnp.concatenate(parts, axis=0).T


def _nsa_prompt(q, gate, kcv, kvs_b, kvw_b, bsz, seq):
    n, d_b = bsz * seq, q.shape[1]
    kv_w = kvs_b.shape[1]
    nq = seq // Q_BLOCK
    n_cmp = seq // CMP_BLOCK
    rows = N_HEADS * Q_BLOCK
    return pl.pallas_call(
        functools.partial(_nsa_prompt_kernel, seq=seq),
        grid=(bsz, nq),
        in_specs=[pl.BlockSpec((Q_BLOCK, d_b), lambda b, i: (b * nq + i, 0)),
                  pl.BlockSpec((Q_BLOCK, LANES), lambda b, i: (b * nq + i, 0)),
                  pl.BlockSpec((n_cmp, kv_w), lambda b, i: (b, 0)),
                  pl.BlockSpec((seq, kv_w), lambda b, i: (b, 0)),
                  pl.BlockSpec((seq, kv_w), lambda b, i: (b, 0))],
        out_specs=pl.BlockSpec((Q_BLOCK, d_b), lambda b, i: (b * nq + i, 0)),
        out_shape=jax.ShapeDtypeStruct((n, d_b), F32),
        scratch_shapes=[pltpu.VMEM((seq, 2 * LANES), MXU_DTYPE),
                        pltpu.VMEM((seq // KEY_TILE, LANES + SUM_ROWS, KEY_TILE), MXU_DTYPE),
                        pltpu.VMEM((seq // Q_BLOCK, LANES, Q_BLOCK), MXU_DTYPE),
                        pltpu.VMEM((LANES, n_cmp), MXU_DTYPE),
                        pltpu.VMEM((rows, 2 * LANES), MXU_DTYPE),
                        pltpu.VMEM((1, rows), F32), pltpu.VMEM((LANES + SUM_ROWS, rows), F32),
                        pltpu.VMEM((KEY_TILE, rows), F32), pltpu.VMEM((KEY_TILE, rows), F32)],
        compiler_params=_cparams("arbitrary", "arbitrary"),
        name="nsa_prompt",
    )(q, gate, kcv, kvs_b, kvw_b)


def _nsa_sample_kernel(pt_ref, q_ref, gate_ref, ksn_ref, kwn_ref, wcmp_ref, win_ref, *rest,
                       n_pages, page, past, t_len):
    cmp_pages = rest[:n_pages]
    sel_pages = rest[n_pages:2 * n_pages]
    out_ref, nwin_ref, kaug, vsel, kcv, lhs = rest[2 * n_pages:]
    rows = 8
    n_cmp = past // CMP_BLOCK
    n_sel = pl.cdiv(past + t_len, SEL_BLOCK)
    per_page = page // CMP_BLOCK

    @pl.when(pl.program_id(0) == 0)
    def _():
        kaug[:, LANES:2 * LANES] = _block_onehot(past)

    q = jnp.concatenate([q_ref[0], jnp.zeros((rows - t_len, q_ref.shape[2]), F32)], axis=0)
    _pad_heads(q, lhs, rows)
    qpad = lhs[:, 0:LANES]
    tq = lax.broadcasted_iota(jnp.int32, (rows, 1), 0)
    pos_q = past + tq

    wrep = jnp.concatenate([wcmp_ref[...]] * per_page, axis=0)
    for p in range(n_pages):
        x = cmp_pages[p][...] * wrep
        for j in range(per_page):
            blk = p * per_page + j
            dst = (blk % 2) * (n_cmp // 2) + blk // 2
            kcv[dst:dst + 1, :] = jnp.sum(x[j * CMP_BLOCK:(j + 1) * CMP_BLOCK], axis=0, keepdims=True)
        kaug[p * page:(p + 1) * page, 0:LANES] = sel_pages[p][:, 0:LANES].astype(MXU_DTYPE)
        vsel[p * page:(p + 1) * page, :] = sel_pages[p][:, LANES:2 * LANES].astype(MXU_DTYPE)

    o_c, sels = _compressed_branch(qpad, kcv[...], past, rows, n_sel)
    _store_selection(lhs, sels, rows)

    def new_rows(ref):
        kv = jnp.concatenate([ref[0], jnp.zeros((LANES - t_len, ref.shape[2]), F32)], axis=0)
        return kv[:, 0:LANES].astype(MXU_DTYPE), kv[:, LANES:2 * LANES].astype(MXU_DTYPE)

    tk_new = lax.broadcasted_iota(jnp.int32, (rows, LANES), 1)
    m_new = (tk_new < t_len) & (tk_new <= tq)

    def joint_attention(s_past, mask_past, v_past, s_new, v_new):
        outs = []
        p_past, p_new = [], []
        for h in range(N_HEADS):
            sl = slice(h * rows, (h + 1) * rows)
            sp = s_past[sl] if mask_past is None else jnp.where(mask_past, s_past[sl], NEG_INF)
            sn = jnp.where(m_new, s_new[sl], NEG_INF)
            m = jnp.maximum(jnp.max(sp, axis=-1, keepdims=True), jnp.max(sn, axis=-1, keepdims=True))
            pp = jnp.exp(sp - m) if mask_past is None else jnp.where(mask_past, jnp.exp(sp - m), 0.0)
            pn = jnp.where(m_new, jnp.exp(sn - m), 0.0)
            den = jnp.maximum(jnp.sum(pp, axis=-1, keepdims=True) + jnp.sum(pn, axis=-1, keepdims=True), TINY)
            p_past.append((pp / den).astype(MXU_DTYPE))
            p_new.append((pn / den).astype(MXU_DTYPE))
        return (jnp.dot(jnp.concatenate(p_past, axis=0), v_past, preferred_element_type=F32)
                + jnp.dot(jnp.concatenate(p_new, axis=0), v_new, preferred_element_type=F32))

    kn, vn = new_rows(ksn_ref)
    s_past = lax.dot_general(lhs[...], kaug[...], _NT, preferred_element_type=F32)
    s_new = lax.dot_general(qpad, kn, _NT, preferred_element_type=F32)
    o_s = joint_attention(s_past, None, vsel[...], s_new, vn)

    n_buf = win_ref.shape[0]
    kwn, vwn = new_rows(kwn_ref)
    kwb = win_ref[:, 0:LANES].astype(MXU_DTYPE)
    vwb = win_ref[:, LANES:2 * LANES].astype(MXU_DTYPE)
    s_wb = lax.dot_general(qpad, kwb, _NT, preferred_element_type=F32)
    s_wn = lax.dot_general(qpad, kwn, _NT, preferred_element_type=F32)
    pos_w = past - n_buf + lax.broadcasted_iota(jnp.int32, (rows, n_buf), 1)
    dlt = pos_q - pos_w
    m_wb = (dlt >= 0) & (dlt <= WINDOW) & (pos_w >= 0)
    o_w = joint_attention(s_wb, m_wb, vwb, s_wn, vwn)

    for m, slab in enumerate(_gated_output(
            jnp.concatenate([gate_ref[0], jnp.zeros((rows - t_len, LANES), F32)], axis=0), o_c, o_s, o_w, rows)):
        out_ref[0, :, m * LANES:(m + 1) * LANES] = slab[0:t_len]

    nwin_ref[0:n_buf - t_len, :] = win_ref[t_len:n_buf, :]
    nwin_ref[n_buf - t_len:n_buf, :] = kwn_ref[0]


def _nsa_sample(page_table, q, gate, kvs_new, kvw_new, w_cmp, pool_c, pool_s, win, layer):
    dbs, t_len, d_b = q.shape
    n_pages = page_table.shape[1]
    page, kv_w = pool_c.shape[2], pool_c.shape[3]
    n_buf = win.shape[2]
    past = n_pages * page
    tok = lambda w: pl.BlockSpec((1, t_len, w), lambda b, pt: (b, 0, 0))

    def page_spec(p):
        return pl.BlockSpec((None, None, page, kv_w), lambda b, pt, p=p: (layer, pt[b * n_pages + p], 0, 0))

    grid_spec = pltpu.PrefetchScalarGridSpec(
        num_scalar_prefetch=1,
        grid=(dbs,),
        in_specs=[tok(d_b), tok(LANES), tok(kv_w), tok(kv_w),
                  pl.BlockSpec(w_cmp.shape, lambda b, pt: (0, 0)),
                  pl.BlockSpec((None, None, n_buf, kv_w), lambda b, pt: (layer, b, 0, 0))]
                 + [page_spec(p) for p in range(n_pages)] * 2,
        out_specs=[tok(d_b), pl.BlockSpec((None, n_buf, kv_w), lambda b, pt: (b, 0, 0))],
        scratch_shapes=[pltpu.VMEM((past, 2 * LANES), MXU_DTYPE), pltpu.VMEM((past, LANES), MXU_DTYPE),
                        pltpu.VMEM((past // CMP_BLOCK, kv_w), F32), pltpu.VMEM((N_HEADS * 8, 2 * LANES), MXU_DTYPE)],
    )
    return pl.pallas_call(
        functools.partial(_nsa_sample_kernel, n_pages=n_pages, page=page, past=past, t_len=t_len),
        grid_spec=grid_spec,
        out_shape=[jax.ShapeDtypeStruct((dbs, t_len, d_b), F32), jax.ShapeDtypeStruct((dbs, n_buf, kv_w), F32)],
        compiler_params=_cparams("arbitrary"),
        name="nsa_sample",
    )(page_table.reshape(-1), q, gate, kvs_new, kvw_new, w_cmp, win, *([pool_c] * n_pages), *([pool_s] * n_pages))


def _out_proj_router_kernel(x_ref, oa_ref, ob_ref, oc_ref, oas_ref, obs_ref, ocs_ref, wo_ref, g2_ref, wr_ref, br_ref,
                            xn_ref, h2_ref, ri_ref, rw_ref, cnt_ref, run_ref, *, n_ptiles):
    tm = x_ref.shape[0]

    @pl.when(pl.program_id(0) == 0)
    def _():
        run_ref[...] = jnp.zeros(run_ref.shape, F32)

    is_sample = pl.program_id(0) >= n_ptiles
    mix = jnp.concatenate([jnp.where(is_sample, oas_ref[...], oa_ref[...]),
                           jnp.where(is_sample, obs_ref[...], ob_ref[...]),
                           jnp.where(is_sample, ocs_ref[...], oc_ref[...])], axis=1)
    xn = x_ref[...] + _split_dot(_split_act(mix, wo_ref.shape[0]), wo_ref)
    xn_ref[...] = xn
    h2 = _rmsnorm(xn, g2_ref[...])
    _store_token_tiles(h2_ref, h2)
    logits = _split_dot(_split_act(h2, wr_ref.shape[0]), wr_ref) + br_ref[...]

    lane = lax.broadcasted_iota(jnp.int32, (tm, LANES), 1)
    lanef = lane.astype(F32)

    def softmax_over(mask):
        m = jnp.max(jnp.where(mask, logits, -jnp.inf), axis=-1, keepdims=True)
        e = jnp.where(mask, jnp.exp(logits - m), 0.0)
        return e / jnp.sum(e, axis=-1, keepdims=True)

    def first_max(p, mask):
        pm = jnp.max(jnp.where(mask, p, -1.0), axis=-1, keepdims=True)
        idx = jnp.min(jnp.where(mask & (p == pm), lanef, 1e9), axis=-1, keepdims=True)
        return pm, idx

    is_g = lane < N_GROUPS
    g_wt, g_sel = first_max(softmax_over(is_g), is_g)
    lo = N_GROUPS + EXP_PER_GROUP * g_sel
    in_e = (lanef >= lo) & (lanef < lo + EXP_PER_GROUP)
    p_e = softmax_over(in_e)
    p0, i0 = first_max(p_e, in_e)
    p1, i1 = first_max(p_e, in_e & (lanef != i0))
    den = p0 + p1
    e0 = i0 - N_GROUPS
    e1 = i1 - N_GROUPS

    hit0 = lanef == e0
    hit1 = lanef == e1
    onehot = jnp.where(hit0 | hit1, 1.0, 0.0)
    r_i = lax.broadcasted_iota(jnp.int32, (tm, tm), 0)
    c_i = lax.broadcasted_iota(jnp.int32, (tm, tm), 1)
    ltri = jnp.where(c_i < r_i, 1.0, 0.0).astype(jnp.bfloat16)
    before = jnp.dot(ltri, onehot.astype(jnp.bfloat16), preferred_element_type=F32) + run_ref[0:1, :]
    r0 = jnp.sum(jnp.where(hit0, before, 0.0), axis=-1, keepdims=True)
    r1 = jnp.sum(jnp.where(hit1, before, 0.0), axis=-1, keepdims=True)
    run_ref[...] = run_ref[...] + jnp.sum(onehot, axis=0, keepdims=True)
    cnt_ref[...] = run_ref[...]

    ri = jnp.where(lane == 0, e0, jnp.where(lane == 1, e1, jnp.where(lane == 2, r0, jnp.where(lane == 3, r1, 0.0))))
    ri_ref[...] = ri.T[0:8].astype(jnp.int32)
    rw_ref[...] = jnp.where(lane == 0, g_wt * p0 / den, jnp.where(lane == 1, g_wt * p1 / den, 0.0))


def _out_proj_router(x, mix_p, mix_s, w_out, g2, w_route, b_route):
    n, d = x.shape
    tm = TOKEN_TILE
    n_ptiles = mix_p[0].shape[0] // tm
    assert all(a.shape[0] == tm for a in mix_s) and n == (n_ptiles + 1) * tm
    row = lambda w: pl.BlockSpec((tm, w), lambda i: (i, 0))
    prow = lambda a: pl.BlockSpec((tm, a.shape[1]), lambda i: (jnp.minimum(i, n_ptiles - 1), 0))
    full = lambda a: pl.BlockSpec(a.shape, lambda i: (0,) * a.ndim)
    return pl.pallas_call(
        functools.partial(_out_proj_router_kernel, n_ptiles=n_ptiles),
        grid=(n // tm,),
        in_specs=[row(d)] + [prow(a) for a in mix_p] + [full(a) for a in mix_s]
                 + [full(w_out), full(g2), full(w_route), full(b_route)],
        out_specs=[row(d), pl.BlockSpec((tm * SUBLANES, LANES), lambda i: (i, 0)),
                   pl.BlockSpec((8, tm), lambda i: (0, i)), row(LANES), pl.BlockSpec((8, LANES), lambda i: (0, 0))],
        out_shape=[jax.ShapeDtypeStruct((n, d), F32), jax.ShapeDtypeStruct((n * SUBLANES, LANES), F32),
                   jax.ShapeDtypeStruct((8, n), jnp.int32), jax.ShapeDtypeStruct((n, LANES), F32),
                   jax.ShapeDtypeStruct((8, LANES), F32)],
        scratch_shapes=[pltpu.VMEM((8, LANES), F32)],
        compiler_params=_cparams("arbitrary"),
        name="out_proj_router",
    )(x, *mix_p, *mix_s, w_out, g2, w_route, b_route)


def _store_token_tiles(ref, x):
    t = x.shape[0]
    for c in range(SUBLANES):
        ref[pl.ds(c, t, stride=SUBLANES), :] = x[:, c * LANES:(c + 1) * LANES]


def _load_token_tiles(ref, first_row, t):
    return jnp.concatenate([ref[pl.ds(first_row + c, t, stride=SUBLANES), :] for c in range(SUBLANES)], axis=1)


def _row_copy(src_hbm, row, dst, slot, r, sem):
    return pltpu.make_async_copy(src_hbm.at[pl.ds(pl.multiple_of(row * SUBLANES, SUBLANES), SUBLANES)],
                                 dst.at[slot, pl.ds(pl.multiple_of(r * SUBLANES, SUBLANES), SUBLANES)],
                                 sem.at[slot])


def _expert_kernel(te_ref, nv_ref, src_ref, h_hbm, wg_ref, wu_ref, wd_ref, ys_ref, xbuf, sem):
    t = pl.program_id(0)
    nv = nv_ref[0]
    te = xbuf.shape[1] // SUBLANES

    def issue(tile, slot):
        def body(i, carry):
            for j in range(2):
                r = 2 * i + j
                _row_copy(h_hbm, src_ref[tile * te + r], xbuf, slot, r, sem).start(priority=j)
            return carry
        lax.fori_loop(0, te // 2, body, 0, unroll=4)

    def wait_all(slot):
        pltpu.make_async_copy(h_hbm.at[pl.ds(0, te * SUBLANES)], xbuf.at[slot], sem.at[slot]).wait()

    @pl.when((t == 0) & (nv > 0))
    def _():
        issue(0, 0)

    @pl.when(t + 1 < nv)
    def _():
        issue(t + 1, (t + 1) % 2)

    @pl.when(t < nv)
    def _():
        slot = t % 2
        wait_all(slot)
        x = _load_token_tiles(xbuf.at[slot], 0, te).astype(MXU_DTYPE)
        hg = jnp.dot(x, wg_ref[...].astype(MXU_DTYPE), preferred_element_type=F32)
        hu = jnp.dot(x, wu_ref[...].astype(MXU_DTYPE), preferred_element_type=F32)
        act = (jax.nn.silu(hg) * hu).astype(MXU_DTYPE)
        _store_token_tiles(ys_ref, jnp.dot(act, wd_ref[...].astype(MXU_DTYPE), preferred_element_type=F32))

    @pl.when(t >= nv)
    def _():
        ys_ref[...] = jnp.zeros(ys_ref.shape, F32)


def _experts(tile_expert, n_valid, src_tok, h2, w_gate, w_up, w_down, layer):
    n_tiles = tile_expert.shape[0]
    te = EXPERT_TILE
    d = SUBLANES * LANES
    d_e = w_gate.shape[3]
    assert w_gate.shape[2] == d and h2.shape[1] == LANES
    wspec = lambda r, c: pl.BlockSpec((None, None, r, c), lambda t, te_r, nv_r, src_r: (layer, te_r[t], 0, 0))
    grid_spec = pltpu.PrefetchScalarGridSpec(
        num_scalar_prefetch=3,
        grid=(n_tiles,),
        in_specs=[pl.BlockSpec(memory_space=pl.ANY), wspec(d, d_e), wspec(d, d_e), wspec(d_e, d)],
        out_specs=pl.BlockSpec((te * SUBLANES, LANES), lambda t, te_r, nv_r, src_r: (t, 0)),
        scratch_shapes=[pltpu.VMEM((2, te * SUBLANES, LANES), F32), pltpu.SemaphoreType.DMA((2,))],
    )
    return pl.pallas_call(
        _expert_kernel,
        grid_spec=grid_spec,
        out_shape=jax.ShapeDtypeStruct((n_tiles * te * SUBLANES, LANES), F32),
        compiler_params=_cparams("arbitrary"),
        name="moe_experts",
    )(tile_expert, n_valid, src_tok, h2, w_gate, w_up, w_down)


def _combine_kernel(d0_ref, d1_ref, x_ref, rw_ref, gf_ref, ys_hbm, *rest, final):
    if final:
        xo_ref, y_ref, buf, sem = rest
    else:
        xo_ref, buf, sem = rest
    t = pl.program_id(0)
    nt = pl.num_programs(0)
    tm = x_ref.shape[0]

    def issue(tile, slot):
        def body(r, carry):
            _row_copy(ys_hbm, d0_ref[tile * tm + r], buf, slot, r, sem).start(priority=0)
            _row_copy(ys_hbm, d1_ref[tile * tm + r], buf, slot, tm + r, sem).start(priority=1)
            return carry
        lax.fori_loop(0, tm, body, 0, unroll=8)

    def wait_all(slot):
        pltpu.make_async_copy(ys_hbm.at[pl.ds(0, 2 * tm * SUBLANES)], buf.at[slot], sem.at[slot]).wait()

    @pl.when(t == 0)
    def _():
        issue(0, 0)

    @pl.when(t + 1 < nt)
    def _():
        issue(t + 1, (t + 1) % 2)

    slot = t % 2
    wait_all(slot)
    rw = rw_ref[...]
    xo = (x_ref[...] + rw[:, 0:1] * _load_token_tiles(buf.at[slot], 0, tm)
          + rw[:, 1:2] * _load_token_tiles(buf.at[slot], tm * SUBLANES, tm))
    xo_ref[...] = xo
    if final:
        y_ref[...] = _rmsnorm(xo, gf_ref[...])


def _combine(d0, d1, x, rw, gf, ys, final):
    n, d = x.shape
    tm = COMBINE_TILE
    row = lambda w: pl.BlockSpec((tm, w), lambda t, a, b: (t, 0))
    n_out = 2 if final else 1
    grid_spec = pltpu.PrefetchScalarGridSpec(
        num_scalar_prefetch=2,
        grid=(n // tm,),
        in_specs=[row(d), row(LANES), pl.BlockSpec((1, d), lambda t, a, b: (0, 0)), pl.BlockSpec(memory_space=pl.ANY)],
        out_specs=[row(d)] * n_out,
        scratch_shapes=[pltpu.VMEM((2, 2 * tm * SUBLANES, LANES), F32), pltpu.SemaphoreType.DMA((2,))],
    )
    return pl.pallas_call(
        functools.partial(_combine_kernel, final=final),
        grid_spec=grid_spec,
        out_shape=[jax.ShapeDtypeStruct((n, d), F32)] * n_out,
        compiler_params=_cparams("arbitrary"),
        name="moe_combine_final" if final else "moe_combine",
    )(d0, d1, x, rw, gf, ys)


def _route_plan(ri, cnt, n_tiles):
    te = EXPERT_TILE
    n = ri.shape[1]
    counts = cnt[0, :N_EXPERTS].astype(jnp.int32)
    padded = ((counts + te - 1) // te) * te
    ends = jnp.cumsum(padded)
    offs = ends - padded
    d0 = offs[ri[0]] + ri[2]
    d1 = offs[ri[1]] + ri[3]
    tok = jnp.arange(n, dtype=jnp.int32)
    src = jnp.zeros((n_tiles * te,), jnp.int32).at[jnp.concatenate([d0, d1])].set(jnp.concatenate([tok, tok]))
    tile_start = jnp.arange(n_tiles, dtype=jnp.int32) * te
    tile_expert = jnp.minimum(jnp.sum((ends[None, :] <= tile_start[:, None]).astype(jnp.int32), axis=1),
                              N_EXPERTS - 1)
    n_valid = (ends[-1:] // te).astype(jnp.int32)
    return d0, d1, src, tile_expert, n_valid


def _rope_tables(pos):
    half = HEAD_DIM // 2
    inv = ROPE_THETA ** (-jnp.arange(half, dtype=F32) / half)
    ang = pos.astype(F32)[:, None] * inv[None, :]
    cos = jnp.cos(ang)
    sin = jnp.sin(ang)
    cos_t = jnp.concatenate([cos, cos] * (LANES // HEAD_DIM), axis=1)
    sin_t = jnp.concatenate([-sin, sin] * (LANES // HEAD_DIM), axis=1)
    return cos_t, sin_t


def _block_diag(w):
    nb, bw, _ = w.shape
    out = jnp.zeros((nb * bw, nb * bw), w.dtype)
    for i in range(nb):
        out = out.at[i * bw:(i + 1) * bw, i * bw:(i + 1) * bw].set(w[i])
    return out


def kernel(x_prompt, x_sample, cache_kv_cmp, cache_kv_sel, cache_kv_win, state_lru_h, state_lru_conv, state_sconv, page_table, norm1_g, w_in, lru_conv_w, lru_conv_b, lru_wa, lru_ba, lru_wx, lru_bx, lru_lambda, nsa_cmp_wk, nsa_cmp_wv, sc_conv_w, w_out, norm2_g, router_group_w, router_group_b, router_exp_w, router_exp_b, exp_w_gate, exp_w_up, exp_w_down, norm_f_g):
    bsz, seq, d = x_prompt.shape
    dbs, t_len, _ = x_sample.shape
    depth = w_in.shape[0]
    d_a = lru_conv_w.shape[2]
    d_c = sc_conv_w.shape[2]
    kv_w = 2 * N_KV * HEAD_DIM
    d_b = N_HEADS * HEAD_DIM
    n_gate = 3 * N_HEADS
    n_p = bsz * seq
    n_s = dbs * t_len
    n = n_p + n_s
    assert n_s == TOKEN_TILE and n_p % TOKEN_TILE == 0 and seq % TOKEN_TILE == 0
    page = cache_kv_cmp.shape[2]
    past = page_table.shape[1] * page
    n_buf = cache_kv_win.shape[2]

    pos = jnp.concatenate([jnp.arange(seq, dtype=jnp.int32),
                           jnp.tile(past + jnp.arange(t_len, dtype=jnp.int32), dbs)])
    cos_t, sin_t = _rope_tables(pos)

    pool_c = cache_kv_cmp.reshape(depth, -1, page, kv_w)
    pool_s = cache_kv_sel.reshape(depth, -1, page, kv_w)
    win = cache_kv_win.reshape(depth, dbs, n_buf, kv_w)

    g_off = 2 * d_a + d_b + 3 * kv_w
    x = jnp.concatenate([x_prompt.reshape(n_p, d), x_sample.reshape(n_s, d)], axis=0)
    h0_p = jnp.zeros((bsz, 1, d_a), F32)
    lb0_p = jnp.zeros((bsz, lru_conv_w.shape[1] - 1, d_a), F32)
    sb0_p = jnp.zeros((bsz, sc_conv_w.shape[1] - 1, d_c), F32)

    n_tiles = (2 * n) // EXPERT_TILE + N_EXPERTS
    proj_parts = lambda l: 2 if l < depth - 1 else 1
    states_p, states_s = [], []
    y = None
    for l in range(depth):
        w_l = w_in[l]
        w_r = _split_weight(jnp.concatenate([w_l[:, :g_off], w_l[:, g_off + n_gate:], w_l[:, g_off:g_off + n_gate],
                                             jnp.zeros((d, LANES - n_gate), F32)], axis=1), proj_parts(l))
        lru_in, sc_in, gate, q, kvc, kvs, kvw, kvs_b, kvw_b = _in_proj(
            x, norm1_g[l][None], w_r, cos_t, sin_t, n_p, seq, d_a=d_a, d_b=d_b, kv_w=kv_w, d_c=d_c)

        seq_w = (lru_conv_w[l], lru_conv_b[l][None], _block_diag(lru_wa[l]).astype(MXU_DTYPE), lru_ba[l][None],
                 _block_diag(lru_wx[l]).astype(MXU_DTYPE), lru_bx[l][None], lru_lambda[l][None], sc_conv_w[l])
        out_a, out_c, hn_p, lbn_p, sbn_p = _seq_mix_prompt(lru_in, sc_in, h0_p, lb0_p, sb0_p, seq_w, bsz, seq,
                                                           d_a=d_a, d_c=d_c)
        tmaj = lambda a: jnp.swapaxes(a.reshape(dbs, t_len, -1), 0, 1)
        oa_s, oc_s, hn_s, lbn_s, sbn_s = _seq_mix_sample(
            tmaj(lru_in[n_p:]), tmaj(sc_in[n_p:]), state_lru_h[l], jnp.swapaxes(state_lru_conv[l], 0, 1),
            jnp.swapaxes(state_sconv[l], 0, 1), seq_w, d_a=d_a, d_c=d_c)
        oa_s = jnp.swapaxes(oa_s, 0, 1).reshape(n_s, d_a)
        oc_s = jnp.swapaxes(oc_s, 0, 1).reshape(n_s, d_c)

        w_cmp = jnp.concatenate([jnp.broadcast_to(nsa_cmp_wk[l][:, None], (CMP_BLOCK, kv_w // 2)),
                                 jnp.broadcast_to(nsa_cmp_wv[l][:, None], (CMP_BLOCK, kv_w // 2))], axis=1)
        kcv = _compress_prompt(kvc, w_cmp, bsz, seq)
        out_b = _nsa_prompt(q, gate, kcv, kvs_b, kvw_b, bsz, seq)
        s3 = lambda a: a[n_p:].reshape(dbs, t_len, -1)
        ob_s, nwin_s = _nsa_sample(page_table, s3(q), s3(gate), s3(kvs), s3(kvw), w_cmp, pool_c, pool_s, win, l)
        ob_s = ob_s.reshape(n_s, d_b)

        w_route = _split_weight(jnp.concatenate([router_group_w[l], router_exp_w[l],
                                                 jnp.zeros((d, LANES - N_GROUPS - N_EXPERTS), F32)], axis=1), 2)
        b_route = jnp.concatenate([router_group_b[l], router_exp_b[l],
                                   jnp.zeros((LANES - N_GROUPS - N_EXPERTS,), F32)])[None]
        xn, h2, ri, rw, cnt = _out_proj_router(x, (out_a, out_b, out_c), (oa_s, ob_s, oc_s),
                                               _split_weight(w_out[l], proj_parts(l)), norm2_g[l][None],
                                               w_route, b_route)
        d0, d1, src, tile_expert, n_valid = _route_plan(ri, cnt, n_tiles)
        ys = _experts(tile_expert, n_valid, src, h2, exp_w_gate, exp_w_up, exp_w_down, l)
        final = l == depth - 1
        outs = _combine(d0, d1, xn, rw, norm_f_g[None], ys, final)
        x = outs[0]
        if final:
            y = outs[1]

        kv6 = lambda a, lead: a.reshape(lead + (2, N_KV, HEAD_DIM))
        states_p.append((kv6(kvc[:n_p], (bsz, seq)), kv6(kvs[:n_p], (bsz, seq)),
                         kv6(kvw[:n_p].reshape(bsz, seq, kv_w)[:, seq - min(WINDOW, seq):], (bsz, min(WINDOW, seq))),
                         hn_p[:, 0], lbn_p, sbn_p))
        states_s.append((kv6(kvc[n_p:], (dbs, t_len)), kv6(kvs[n_p:], (dbs, t_len)), kv6(nwin_s, (dbs, n_buf)),
                         hn_s, jnp.swapaxes(lbn_s, 0, 1), jnp.swapaxes(sbn_s, 0, 1)))

    stack = lambda sts, i: jnp.stack([s[i] for s in sts])
    res = [y[:n_p].reshape(bsz, seq, d), y[n_p:].reshape(dbs, t_len, d)]
    for i in range(6):
        res += [stack(states_p, i), stack(states_s, i)]
    return tuple(res)
```

```python
import functools

import jax
import jax.numpy as jnp
from jax import lax
from jax.experimental import pallas as pl
from jax.experimental.pallas import tpu as pltpu

F32 = jnp.float32
MXU_DTYPE = jnp.bfloat16

HEAD_DIM = 64
N_HEADS = 8
N_KV = 2
GROUP = N_HEADS // N_KV
CMP_BLOCK = 32
SEL_BLOCK = 64
TOP_N = 16
WINDOW = 512
Q_BLOCK = 128
ROPE_THETA = 10000.0
LRU_C = 8.0
N_GROUPS = 4
EXP_PER_GROUP = 8
N_EXPERTS = N_GROUPS * EXP_PER_GROUP
RMS_EPS = 1e-6
NEG_INF = -1e30
TINY = 1e-30
FORCE = 1e6

LANES = 128
SUBLANES = 8
SUM_ROWS = 2 * SUBLANES
VMEM_LIMIT = 56 * 2 ** 20
TOKEN_TILE = 512
TIME_CHUNK = 256
KEY_TILE = 512
EXPERT_TILE = 256
COMBINE_TILE = 256

_NT = (((1,), (1,)), ((), ()))


def _cparams(*sem):
    return pltpu.CompilerParams(dimension_semantics=sem, vmem_limit_bytes=VMEM_LIMIT)


def _split_weight(w, parts):
    if parts == 1:
        return w.astype(MXU_DTYPE)[None]
    hi = lax.bitcast_convert_type(lax.bitcast_convert_type(w, jnp.uint32) & jnp.uint32(0xFFFF0000), F32)
    return jnp.stack([hi.astype(MXU_DTYPE), (w - hi).astype(MXU_DTYPE)])


def _split_act(a, parts):
    if parts == 1:
        return (a.astype(MXU_DTYPE),)
    hi = lax.bitcast_convert_type(lax.bitcast_convert_type(a, jnp.uint32) & jnp.uint32(0xFFFF0000), F32)
    return hi.astype(MXU_DTYPE), (a - hi).astype(MXU_DTYPE)


def _split_dot(a_parts, w_ref, cols=slice(None)):
    dot = lambda a, p: jnp.dot(a, w_ref[p, :, cols], preferred_element_type=F32)
    if len(a_parts) == 1:
        return dot(a_parts[0], 0)
    return (dot(a_parts[1], 1) + (dot(a_parts[0], 1) + dot(a_parts[1], 0))) + dot(a_parts[0], 0)


def _rmsnorm(x, g):
    return x * lax.rsqrt(jnp.mean(x * x, axis=-1, keepdims=True) + RMS_EPS) * g


def _swap_halves(x):
    w = x.shape[-1]
    lane = lax.broadcasted_iota(jnp.int32, x.shape, x.ndim - 1)
    first = (lane % HEAD_DIM) < HEAD_DIM // 2
    return jnp.where(first, pltpu.roll(x, w - HEAD_DIM // 2, axis=1), pltpu.roll(x, HEAD_DIM // 2, axis=1))


def _masked_softmax(s, mask):
    s = jnp.where(mask, s, NEG_INF)
    m = jnp.max(s, axis=-1, keepdims=True)
    p = jnp.where(mask, jnp.exp(s - m), 0.0)
    return p / jnp.maximum(jnp.sum(p, axis=-1, keepdims=True), TINY)


def _select_blocks_keys(imp, pos_q, n_sel):
    blk = lax.broadcasted_iota(jnp.int32, imp.shape, 0)
    cur = pos_q // SEL_BLOCK
    forced = (blk == 0) | (blk == cur) | (blk == cur - 1)
    w = jnp.where(forced, FORCE, imp)
    w = jnp.where(blk <= cur, w, NEG_INF)
    w = jnp.where(blk < n_sel, w, -jnp.inf)
    blkf = blk.astype(F32)

    def body(_, carry):
        w, sel = carry
        m = jnp.max(w, axis=0, keepdims=True)
        first = jnp.min(jnp.where(w == m, blkf, 1e9), axis=0, keepdims=True)
        pick = blkf == first
        return jnp.where(pick, -jnp.inf, w), jnp.where(pick, 1.0, sel)

    _, sel = lax.fori_loop(0, min(TOP_N, n_sel), body, (w, jnp.zeros_like(w)))
    return sel


def _token_specs(x, tm, n_ptiles):
    if not isinstance(x, tuple):
        return [x], [pl.BlockSpec((tm, x.shape[1]), lambda i: (i, 0))]
    x_p, x_s = x
    assert x_p.shape[0] == n_ptiles * tm and x_s.shape[0] == tm
    return [x_p, x_s], [pl.BlockSpec((tm, x_p.shape[1]), lambda i: (jnp.minimum(i, n_ptiles - 1), 0)),
                        pl.BlockSpec((tm, x_s.shape[1]), lambda i: (0, 0))]


def _token_tile(x_refs, n_ptiles):
    if len(x_refs) == 1:
        return x_refs[0][...]
    return jnp.where(pl.program_id(0) >= n_ptiles, x_refs[1][...], x_refs[0][...])


def _in_proj_kernel(*refs, d_a, d_b, kv_w, d_c, n_x, n_ptiles):
    x_refs = refs[:n_x]
    (g_ref, w_ref, cos_ref, sin_ref,
     lru_ref, sc_ref, gate_ref, q_ref, kvc_ref, kvs_ref, kvw_ref, kvsb_ref, kvwb_ref) = refs[n_x:]
    h_parts = _split_act(_rmsnorm(_token_tile(x_refs, n_ptiles), g_ref[...]), w_ref.shape[0])

    def mm(a, b):
        return _split_dot(h_parts, w_ref, slice(a, b))

    cos = cos_ref[...]
    sin = sin_ref[...]
    off = 2 * d_a
    lru_ref[...] = mm(0, off)
    q = mm(off, off + d_b)
    reps = d_b // LANES
    cos_q = jnp.concatenate([cos] * reps, axis=1)
    sin_q = jnp.concatenate([sin] * reps, axis=1)
    q_ref[...] = (q * cos_q + _swap_halves(q) * sin_q) * (HEAD_DIM ** -0.5)
    off += d_b
    for ref, bref in ((kvc_ref, None), (kvs_ref, kvsb_ref), (kvw_ref, kvwb_ref)):
        kv = mm(off, off + kv_w)
        k = kv[:, :kv_w // 2]
        kv = jnp.concatenate([k * cos + _swap_halves(k) * sin, kv[:, kv_w // 2:]], axis=1)
        ref[...] = kv
        if bref is not None:
            bref[...] = kv.astype(MXU_DTYPE)
        off += kv_w
    sc_ref[...] = mm(off, off + 3 * d_c)
    off += 3 * d_c
    gate_ref[...] = jax.nn.sigmoid(mm(off, off + LANES))


def _in_proj(x, g1, w_r, cos_t, sin_t, n_prompt, seq, *, d_a, d_b, kv_w, d_c):
    tm = TOKEN_TILE
    n_ptiles = n_prompt // tm
    n_stiles = seq // tm
    x_ops, x_specs = _token_specs(x, tm, n_ptiles)
    n, d = sum(a.shape[0] for a in x_ops), x_ops[0].shape[1]

    def tab_map(i):
        return (jnp.where(i < n_ptiles, i % n_stiles, n_stiles), 0)

    row = lambda w: pl.BlockSpec((tm, w), lambda i: (i, 0))
    widths = (2 * d_a, 3 * d_c, LANES, d_b, kv_w, kv_w, kv_w, kv_w, kv_w)
    dtypes = (F32, F32, F32, F32, F32, F32, F32, MXU_DTYPE, MXU_DTYPE)
    return pl.pallas_call(
        functools.partial(_in_proj_kernel, d_a=d_a, d_b=d_b, kv_w=kv_w, d_c=d_c, n_x=len(x_ops), n_ptiles=n_ptiles),
        grid=(n // tm,),
        in_specs=x_specs + [
                  pl.BlockSpec((1, d), lambda i: (0, 0)),
                  pl.BlockSpec(w_r.shape, lambda i: (0, 0, 0)),
                  pl.BlockSpec((tm, LANES), tab_map),
                  pl.BlockSpec((tm, LANES), tab_map)],
        out_specs=[row(w) for w in widths],
        out_shape=[jax.ShapeDtypeStruct((n, w), dt) for w, dt in zip(widths, dtypes)],
        compiler_params=_cparams("parallel"),
        name="in_proj",
    )(*x_ops, g1, w_r, cos_t, sin_t)


def _scan_rows(a, u):
    t = a.shape[0]
    row = lax.broadcasted_iota(jnp.int32, a.shape, 0)
    d = 1
    while d < t:
        keep = row >= d
        a_sh = jnp.where(keep, pltpu.roll(a, d, axis=0), 1.0)
        u_sh = jnp.where(keep, pltpu.roll(u, d, axis=0), 0.0)
        u = a * u_sh + u
        a = a * a_sh
        d *= 2
    return a, u


def _lru_gates(xc, wa_ref, ba_ref, wx_ref, bx_ref, lam_ref):
    xb = xc.astype(MXU_DTYPE)
    r = jax.nn.sigmoid(jnp.dot(xb, wa_ref[...], preferred_element_type=F32) + ba_ref[...])
    i = jax.nn.sigmoid(jnp.dot(xb, wx_ref[...], preferred_element_type=F32) + bx_ref[...])
    log_a = -LRU_C * r * jax.nn.softplus(-lam_ref[...])
    a = jnp.exp(log_a)
    th = jnp.tanh(log_a)
    u = jnp.sqrt(-2.0 * th / (1.0 - th)) * (i * xc)
    return a, u


def _seq_mix_kernel(lru_ref, sc_ref, h0_ref, lb0_ref, sb0_ref, cw_ref, cb_ref, wa_ref, ba_ref, wx_ref, bx_ref,
                    lam_ref, scw_ref, oa_ref, oc_ref, hn_ref, lbn_ref, sbn_ref, xpad, vpad, hcar, *, d_a, d_c):
    c = pl.program_id(1)
    tc = lru_ref.shape[0]
    nlb = lb0_ref.shape[1]
    nsb = sb0_ref.shape[1]

    @pl.when(c == 0)
    def _():
        xpad[0:8, :] = jnp.zeros((8, d_a), F32)
        xpad[8 - nlb:8, :] = lb0_ref[0]
        vpad[0:8, :] = jnp.zeros((8, d_c), F32)
        vpad[8 - nsb:8, :] = sb0_ref[0]
        hcar[...] = h0_ref[0]

    xa = lru_ref[:, 0:d_a]
    ga = lru_ref[:, d_a:2 * d_a]
    xpad[8:8 + tc, :] = xa
    xc = cw_ref[0:1, :] * xpad[8 - nlb:8 - nlb + tc, :]
    for j in range(1, nlb):
        xc = xc + cw_ref[j:j + 1, :] * xpad[8 - nlb + j:8 - nlb + j + tc, :]
    xc = xc + cw_ref[nlb:nlb + 1, :] * xa + cb_ref[...]
    a, u = _lru_gates(xc, wa_ref, ba_ref, wx_ref, bx_ref, lam_ref)
    a_cum, hs = _scan_rows(a, u)
    hs = hs + a_cum * hcar[...]
    hcar[...] = hs[tc - 1:tc, :]
    oa_ref[...] = hs * jax.nn.gelu(ga)

    v = sc_ref[:, 0:d_c] * sc_ref[:, 2 * d_c:3 * d_c]
    vpad[8:8 + tc, :] = v
    uc = scw_ref[0:1, :] * vpad[8 - nsb:8 - nsb + tc, :]
    for j in range(1, nsb):
        uc = uc + scw_ref[j:j + 1, :] * vpad[8 - nsb + j:8 - nsb + j + tc, :]
    uc = uc + scw_ref[nsb:nsb + 1, :] * v
    oc_ref[...] = sc_ref[:, d_c:2 * d_c] * uc

    hn_ref[0] = hs[tc - 1:tc, :]
    lbn_ref[0] = xpad[8 + tc - nlb:8 + tc, :]
    sbn_ref[0] = vpad[8 + tc - nsb:8 + tc, :]
    xpad[0:8, :] = xpad[tc:tc + 8, :]
    vpad[0:8, :] = vpad[tc:tc + 8, :]


def _seq_mix_prompt(lru_in, sc_in, h0, lb0, sb0, wts, bsz, seq, *, d_a, d_c):
    n = bsz * seq
    tc = TIME_CHUNK
    nch = seq // tc
    row = lambda w: pl.BlockSpec((tc, w), lambda b, c: (b * nch + c, 0))
    full = lambda a: pl.BlockSpec(a.shape, lambda b, c: (0,) * a.ndim)
    state = lambda r, w: pl.BlockSpec((1, r, w), lambda b, c: (b, 0, 0))
    nlb, nsb = lb0.shape[1], sb0.shape[1]
    return pl.pallas_call(
        functools.partial(_seq_mix_kernel, d_a=d_a, d_c=d_c),
        grid=(bsz, nch),
        in_specs=[row(2 * d_a), row(3 * d_c), state(1, d_a), state(nlb, d_a), state(nsb, d_c)]
                 + [full(w) for w in wts],
        out_specs=[row(d_a), row(d_c), state(1, d_a), state(nlb, d_a), state(nsb, d_c)],
        out_shape=[jax.ShapeDtypeStruct((n, d_a), F32), jax.ShapeDtypeStruct((n, d_c), F32),
                   jax.ShapeDtypeStruct((bsz, 1, d_a), F32), jax.ShapeDtypeStruct((bsz, nlb, d_a), F32),
                   jax.ShapeDtypeStruct((bsz, nsb, d_c), F32)],
        scratch_shapes=[pltpu.VMEM((tc + 8, d_a), F32), pltpu.VMEM((tc + 8, d_c), F32), pltpu.VMEM((1, d_a), F32)],
        compiler_params=_cparams("arbitrary", "arbitrary"),
        name="seq_mix_prompt",
    )(lru_in, sc_in, h0, lb0, sb0, *wts)


def _seq_mix_sample_kernel(lru_ref, sc_ref, h0_ref, lb0_ref, sb0_ref, cw_ref, cb_ref, wa_ref, ba_ref, wx_ref,
                           bx_ref, lam_ref, scw_ref, oa_ref, oc_ref, hn_ref, lbn_ref, sbn_ref, *, d_a, d_c):
    t_len = lru_ref.shape[0]
    nlb = lb0_ref.shape[0]
    nsb = sb0_ref.shape[0]
    xs = [lb0_ref[j] for j in range(nlb)] + [lru_ref[t][:, 0:d_a] for t in range(t_len)]
    vs = [sb0_ref[j] for j in range(nsb)] + [sc_ref[t][:, 0:d_c] * sc_ref[t][:, 2 * d_c:3 * d_c] for t in range(t_len)]
    h = h0_ref[...]
    for t in range(t_len):
        xc = cw_ref[0:1, :] * xs[t]
        for j in range(1, nlb + 1):
            xc = xc + cw_ref[j:j + 1, :] * xs[t + j]
        xc = xc + cb_ref[...]
        a, u = _lru_gates(xc, wa_ref, ba_ref, wx_ref, bx_ref, lam_ref)
        h = a * h + u
        oa_ref[t] = h * jax.nn.gelu(lru_ref[t][:, d_a:2 * d_a])
        uc = scw_ref[0:1, :] * vs[t]
        for j in range(1, nsb + 1):
            uc = uc + scw_ref[j:j + 1, :] * vs[t + j]
        oc_ref[t] = sc_ref[t][:, d_c:2 * d_c] * uc
    hn_ref[...] = h
    for j in range(nlb):
        lbn_ref[j] = xs[t_len + j]
    for j in range(nsb):
        sbn_ref[j] = vs[t_len + j]


def _seq_mix_sample(lru_t, sc_t, h0, lb0_t, sb0_t, wts, *, d_a, d_c):
    t_len, bsz, _ = lru_t.shape
    outs = [jax.ShapeDtypeStruct((t_len, bsz, d_a), F32), jax.ShapeDtypeStruct((t_len, bsz, d_c), F32),
            jax.ShapeDtypeStruct(h0.shape, F32), jax.ShapeDtypeStruct(lb0_t.shape, F32),
            jax.ShapeDtypeStruct(sb0_t.shape, F32)]
    return pl.pallas_call(
        functools.partial(_seq_mix_sample_kernel, d_a=d_a, d_c=d_c),
        out_shape=outs,
        compiler_params=pltpu.CompilerParams(vmem_limit_bytes=VMEM_LIMIT),
        name="seq_mix_sample",
    )(lru_t, sc_t, h0, lb0_t, sb0_t, *wts)


def _compress_kernel(kvc_ref, w_ref, out_ref):
    half = out_ref.shape[0] // 2
    x = kvc_ref[...].reshape(half, 2 * CMP_BLOCK, out_ref.shape[1])
    w = w_ref[...][None]
    out_ref[0:half, :] = jnp.sum(x[:, 0:CMP_BLOCK, :] * w, axis=1)
    out_ref[half:2 * half, :] = jnp.sum(x[:, CMP_BLOCK:2 * CMP_BLOCK, :] * w, axis=1)


def _compress_prompt(kvc, w_cmp, bsz, seq):
    kv_w = kvc.shape[1]
    n_cmp = seq // CMP_BLOCK
    return pl.pallas_call(
        _compress_kernel,
        grid=(bsz,),
        in_specs=[pl.BlockSpec((seq, kv_w), lambda b: (b, 0)), pl.BlockSpec(w_cmp.shape, lambda b: (0, 0))],
        out_specs=pl.BlockSpec((n_cmp, kv_w), lambda b: (b, 0)),
        out_shape=jax.ShapeDtypeStruct((bsz * n_cmp, kv_w), F32),
        compiler_params=_cparams("parallel"),
        name="nsa_compress",
    )(kvc, w_cmp)


def _pad_heads(q, lhs_ref, rows):
    lane = lax.broadcasted_iota(jnp.int32, (rows, LANES), 1)
    for h in range(N_HEADS):
        k = h // GROUP
        slab = q[:, (h // 2) * LANES:(h // 2 + 1) * LANES]
        if h % 2 != k:
            slab = pltpu.roll(slab, HEAD_DIM, axis=1)
        keep = (lane >= k * HEAD_DIM) & (lane < (k + 1) * HEAD_DIM)
        lhs_ref[h * rows:(h + 1) * rows, 0:LANES] = jnp.where(keep, slab, 0.0).astype(MXU_DTYPE)


def _gated_output(gate, o_c, o_s, o_w, rows):
    lane = lax.broadcasted_iota(jnp.int32, (rows, LANES), 1)
    slabs = []
    for m in range(N_HEADS // 2):
        parts = []
        for h in (2 * m, 2 * m + 1):
            k = h // GROUP
            sl = slice(h * rows, (h + 1) * rows)
            r = (gate[:, 3 * h:3 * h + 1] * o_c[sl] + gate[:, 3 * h + 1:3 * h + 2] * o_s[sl]
                 + gate[:, 3 * h + 2:3 * h + 3] * o_w[sl])
            if h % 2 != k:
                r = pltpu.roll(r, HEAD_DIM, axis=1)
            parts.append(r)
        slabs.append(jnp.where(lane < HEAD_DIM, parts[0], parts[1]))
    return slabs


def _compressed_branch(qpad, kcv, pos0, rows, n_sel):
    n_cmp = kcv.shape[0]
    half = n_cmp // 2
    pos_q = pos0 + lax.broadcasted_iota(jnp.int32, (rows, 1), 0)
    kc = kcv[:, 0:LANES].astype(MXU_DTYPE)
    vc = kcv[:, LANES:2 * LANES].astype(MXU_DTYPE)
    s_c = lax.dot_general(qpad, kc, _NT, preferred_element_type=F32)
    col = lax.broadcasted_iota(jnp.int32, (rows, n_cmp), 1)
    blk = jnp.where(col < half, 2 * col, 2 * (col - half) + 1)
    m_c = (blk + 1) * CMP_BLOCK - 1 <= pos_q
    ps = []
    imp = [jnp.zeros((rows, n_cmp), F32) for _ in range(N_KV)]
    for h in range(N_HEADS):
        p = _masked_softmax(s_c[h * rows:(h + 1) * rows], m_c)
        imp[h // GROUP] = imp[h // GROUP] + p
        ps.append(p.astype(MXU_DTYPE))
    o_c = jnp.dot(jnp.concatenate(ps, axis=0), vc, preferred_element_type=F32)
    imps = []
    for k in range(N_KV):
        imp_s = imp[k][:, 0:half] + imp[k][:, half:n_cmp]
        if half < LANES:
            imp_s = jnp.concatenate([imp_s, jnp.zeros((rows, LANES - half), F32)], axis=1)
        imps.append(imp_s)
    imp_t = jnp.concatenate(imps + [jnp.zeros((LANES - N_KV * rows, LANES), F32)], axis=0).T
    n_rows = -(-n_sel // 8) * 8
    pos_l = pos0 + lax.broadcasted_iota(jnp.int32, (1, LANES), 1) % rows
    sel_t = _select_blocks_keys(imp_t[0:n_rows], pos_l, n_sel)
    sel = jnp.concatenate([sel_t, jnp.zeros((LANES - n_rows, LANES), F32)], axis=0).T
    return o_c, [sel[k * rows:(k + 1) * rows] for k in range(N_KV)]


def _store_selection(lhs_ref, sels, rows):
    for k in range(N_KV):
        neg = jnp.where(sels[k] > 0.0, 0.0, NEG_INF).astype(MXU_DTYPE)
        for g in range(GROUP):
            h = k * GROUP + g
            lhs_ref[h * rows:(h + 1) * rows, LANES:2 * LANES] = neg


def _block_onehot(n_rows):
    row = lax.broadcasted_iota(jnp.int32, (n_rows, LANES), 0)
    lane = lax.broadcasted_iota(jnp.int32, (n_rows, LANES), 1)
    return jnp.where(row // SEL_BLOCK == lane, 1.0, 0.0).astype(MXU_DTYPE)


def _masked_softmax_keys(s, mask):
    s = jnp.where(mask, s, NEG_INF)
    m = jnp.max(s, axis=0, keepdims=True)
    p = jnp.where(mask, jnp.exp(s - m), 0.0)
    return p / jnp.maximum(jnp.sum(p, axis=0, keepdims=True), TINY)


def _nsa_prompt_kernel(q_ref, gate_ref, kcv_ref, kvs_ref, kvw_ref, out_ref,
                       kaug, vst, vwt, vct, lhs, m_sc, acc_sc, s_a, s_b, *, seq):
    qb = pl.program_id(1)
    rows = Q_BLOCK
    tk = KEY_TILE
    s0 = qb * rows
    n_sel = seq // SEL_BLOCK
    n_cmp = seq // CMP_BLOCK
    cols = N_HEADS * rows

    @pl.when(qb == 0)
    def _():
        kaug[:, 0:LANES] = kvs_ref[:, 0:LANES]
        kaug[:, LANES:2 * LANES] = _block_onehot(seq)
        vct[...] = kcv_ref[:, LANES:2 * LANES].T.astype(MXU_DTYPE)

        def tr_sel(i, c):
            r0 = pl.multiple_of(i * tk, tk)
            ones_row = jnp.where(lax.broadcasted_iota(jnp.int32, (SUM_ROWS, tk), 0) == 0, 1.0, 0.0)
            v_t = kvs_ref[pl.ds(r0, tk), LANES:2 * LANES].astype(F32).T
            vst[i] = jnp.concatenate([v_t, ones_row], axis=0).astype(MXU_DTYPE)
            return c

        def tr_win(i, c):
            r0 = pl.multiple_of(i * rows, rows)
            vwt[i] = kvw_ref[pl.ds(r0, rows), LANES:2 * LANES].astype(F32).T.astype(MXU_DTYPE)
            return c

        lax.fori_loop(0, seq // tk, tr_sel, 0)
        lax.fori_loop(0, seq // rows, tr_win, 0)

    _pad_heads(q_ref[...], lhs, rows)
    qpad = lhs[:, 0:LANES]
    lane = lax.broadcasted_iota(jnp.int32, (1, cols), 1)
    pos_q = s0 + lane % rows

    half = n_cmp // 2
    s_c = lax.dot_general(kcv_ref[:, 0:LANES].astype(MXU_DTYPE), qpad, _NT, preferred_element_type=F32)
    r_c = lax.broadcasted_iota(jnp.int32, (n_cmp, cols), 0)
    blk_c = jnp.where(r_c < half, 2 * r_c, 2 * (r_c - half) + 1)
    p_c = _masked_softmax_keys(s_c, (blk_c + 1) * CMP_BLOCK - 1 <= pos_q)
    o_c = jnp.dot(vct[...], p_c.astype(MXU_DTYPE), preferred_element_type=F32)

    imp = []
    for k in range(N_KV):
        acc = p_c[:, k * GROUP * rows:(k * GROUP + 1) * rows]
        for g in range(1, GROUP):
            acc = acc + p_c[:, (k * GROUP + g) * rows:(k * GROUP + g + 1) * rows]
        imp.append(acc[0:half] + acc[half:n_cmp])
    sel = _select_blocks_keys(jnp.concatenate(imp, axis=1), pos_q[:, 0:N_KV * rows], n_sel)
    for k in range(N_KV):
        sel_k = sel[:, k * rows:(k + 1) * rows]
        if n_sel < LANES:
            sel_k = jnp.concatenate([sel_k, jnp.zeros((LANES - n_sel, rows), F32)], axis=0)
        neg = jnp.where(sel_k.T > 0.0, 0.0, NEG_INF).astype(MXU_DTYPE)
        for g in range(GROUP):
            h = k * GROUP + g
            lhs[h * rows:(h + 1) * rows, LANES:2 * LANES] = neg

    m_sc[...] = jnp.full(m_sc.shape, NEG_INF, F32)
    acc_sc[...] = jnp.zeros(acc_sc.shape, F32)

    last_tile = seq // tk - 1

    def scores(s_ref, kt):
        k0 = pl.multiple_of(jnp.minimum(kt, last_tile) * tk, tk)
        s_ref[...] = lax.dot_general(kaug[pl.ds(k0, tk), :], lhs[...], _NT, preferred_element_type=F32)

    def update(s_ref, kt, causal):
        s = s_ref[...]
        if causal:
            s = jnp.where(kt * tk + lax.broadcasted_iota(jnp.int32, (tk, cols), 0) <= pos_q, s, NEG_INF)
        m_old = m_sc[...]
        m_new = jnp.maximum(m_old, jnp.max(s, axis=0, keepdims=True))
        alpha = jnp.exp(m_old - m_new)
        p = jnp.exp(s - m_new)
        m_sc[...] = m_new
        vt = vst[jnp.minimum(kt, last_tile)]
        acc_sc[...] = alpha * acc_sc[...] + jnp.dot(vt, p.astype(MXU_DTYPE), preferred_element_type=F32)

    n_pairs = (s0 // tk) // 2
    scores(s_a, 0)

    def pair_body(i, carry):
        scores(s_b, 2 * i + 1)
        update(s_a, 2 * i, False)
        scores(s_a, 2 * i + 2)
        update(s_b, 2 * i + 1, False)
        return carry

    lax.fori_loop(0, n_pairs, pair_body, 0)
    t0 = 2 * n_pairs
    scores(s_b, t0 + 1)
    update(s_a, t0, True)

    @pl.when((t0 + 1) * tk <= s0 + rows - 1)
    def _():
        update(s_b, t0 + 1, True)

    o_s = acc_sc[0:LANES, :] / jnp.maximum(acc_sc[LANES:LANES + 1, :], TINY)

    band = WINDOW + rows
    start = pl.multiple_of(jnp.maximum(s0 - WINDOW, 0), rows)
    s_w = lax.dot_general(kvw_ref[pl.ds(start, band), 0:LANES], qpad, _NT, preferred_element_type=F32)
    dlt = pos_q - (start + lax.broadcasted_iota(jnp.int32, (band, cols), 0))
    p_w = _masked_softmax_keys(s_w, (dlt >= 0) & (dlt <= WINDOW))
    t0 = start // rows
    vw = jnp.concatenate([vwt[t0 + j] for j in range(band // rows)], axis=1)
    o_w = jnp.dot(vw, p_w.astype(MXU_DTYPE), preferred_element_type=F32)

    g_t = gate_ref[...].T
    for m in range(N_HEADS // 2):
        parts = []
        for h in (2 * m, 2 * m + 1):
            k = h // GROUP
            rs = slice(k * HEAD_DIM, (k + 1) * HEAD_DIM)
            ls = slice(h * rows, (h + 1) * rows)
            parts.append(g_t[3 * h:3 * h + 1] * o_c[rs, ls] + g_t[3 * h + 1:3 * h + 2] * o_s[rs, ls]
                         + g_t[3 * h + 2:3 * h + 3] * o_w[rs, ls])
        out_ref[:, m * LANES:(m + 1) * LANES] = jnp.concatenate(parts, axis=0).T


def _nsa_prompt(q, gate, kcv, kvs_b, kvw_b, bsz, seq):
    n, d_b = bsz * seq, q.shape[1]
    kv_w = kvs_b.shape[1]
    nq = seq // Q_BLOCK
    n_cmp = seq // CMP_BLOCK
    rows = N_HEADS * Q_BLOCK
    return pl.pallas_call(
        functools.partial(_nsa_prompt_kernel, seq=seq),
        grid=(bsz, nq),
        in_specs=[pl.BlockSpec((Q_BLOCK, d_b), lambda b, i: (b * nq + i, 0)),
                  pl.BlockSpec((Q_BLOCK, LANES), lambda b, i: (b * nq + i, 0)),
                  pl.BlockSpec((n_cmp, kv_w), lambda b, i: (b, 0)),
                  pl.BlockSpec((seq, kv_w), lambda b, i: (b, 0)),
                  pl.BlockSpec((seq, kv_w), lambda b, i: (b, 0))],
        out_specs=pl.BlockSpec((Q_BLOCK, d_b), lambda b, i: (b * nq + i, 0)),
        out_shape=jax.ShapeDtypeStruct((n, d_b), F32),
        scratch_shapes=[pltpu.VMEM((seq, 2 * LANES), MXU_DTYPE),
                        pltpu.VMEM((seq // KEY_TILE, LANES + SUM_ROWS, KEY_TILE), MXU_DTYPE),
                        pltpu.VMEM((seq // Q_BLOCK, LANES, Q_BLOCK), MXU_DTYPE),
                        pltpu.VMEM((LANES, n_cmp), MXU_DTYPE),
                        pltpu.VMEM((rows, 2 * LANES), MXU_DTYPE),
                        pltpu.VMEM((1, rows), F32), pltpu.VMEM((LANES + SUM_ROWS, rows), F32),
                        pltpu.VMEM((KEY_TILE, rows), F32), pltpu.VMEM((KEY_TILE, rows), F32)],
        compiler_params=_cparams("arbitrary", "arbitrary"),
        name="nsa_prompt",
    )(q, gate, kcv, kvs_b, kvw_b)


def _nsa_sample_kernel(pt_ref, q_ref, gate_ref, ksn_ref, kwn_ref, wcmp_ref, win_ref, *rest,
                       n_pages, page, past, t_len):
    cmp_pages = rest[:n_pages]
    sel_pages = rest[n_pages:2 * n_pages]
    out_ref, nwin_ref, kaug, vsel, kcv, lhs = rest[2 * n_pages:]
    rows = 8
    n_cmp = past // CMP_BLOCK
    n_sel = pl.cdiv(past + t_len, SEL_BLOCK)
    per_page = page // CMP_BLOCK

    @pl.when(pl.program_id(0) == 0)
    def _():
        kaug[:, LANES:2 * LANES] = _block_onehot(past)

    q = jnp.concatenate([q_ref[0], jnp.zeros((rows - t_len, q_ref.shape[2]), F32)], axis=0)
    _pad_heads(q, lhs, rows)
    qpad = lhs[:, 0:LANES]
    tq = lax.broadcasted_iota(jnp.int32, (rows, 1), 0)
    pos_q = past + tq

    wrep = jnp.concatenate([wcmp_ref[...]] * per_page, axis=0)
    for p in range(n_pages):
        x = cmp_pages[p][...] * wrep
        for j in range(per_page):
            blk = p * per_page + j
            dst = (blk % 2) * (n_cmp // 2) + blk // 2
            kcv[dst:dst + 1, :] = jnp.sum(x[j * CMP_BLOCK:(j + 1) * CMP_BLOCK], axis=0, keepdims=True)
        kaug[p * page:(p + 1) * page, 0:LANES] = sel_pages[p][:, 0:LANES].astype(MXU_DTYPE)
        vsel[p * page:(p + 1) * page, :] = sel_pages[p][:, LANES:2 * LANES].astype(MXU_DTYPE)

    o_c, sels = _compressed_branch(qpad, kcv[...], past, rows, n_sel)
    _store_selection(lhs, sels, rows)

    def new_rows(ref):
        kv = jnp.concatenate([ref[0], jnp.zeros((LANES - t_len, ref.shape[2]), F32)], axis=0)
        return kv[:, 0:LANES].astype(MXU_DTYPE), kv[:, LANES:2 * LANES].astype(MXU_DTYPE)

    tk_new = lax.broadcasted_iota(jnp.int32, (rows, LANES), 1)
    m_new = (tk_new < t_len) & (tk_new <= tq)

    def joint_attention(s_past, mask_past, v_past, s_new, v_new):
        outs = []
        p_past, p_new = [], []
        for h in range(N_HEADS):
            sl = slice(h * rows, (h + 1) * rows)
            sp = s_past[sl] if mask_past is None else jnp.where(mask_past, s_past[sl], NEG_INF)
            sn = jnp.where(m_new, s_new[sl], NEG_INF)
            m = jnp.maximum(jnp.max(sp, axis=-1, keepdims=True), jnp.max(sn, axis=-1, keepdims=True))
            pp = jnp.exp(sp - m) if mask_past is None else jnp.where(mask_past, jnp.exp(sp - m), 0.0)
            pn = jnp.where(m_new, jnp.exp(sn - m), 0.0)
            den = jnp.maximum(jnp.sum(pp, axis=-1, keepdims=True) + jnp.sum(pn, axis=-1, keepdims=True), TINY)
            p_past.append((pp / den).astype(MXU_DTYPE))
            p_new.append((pn / den).astype(MXU_DTYPE))
        return (jnp.dot(jnp.concatenate(p_past, axis=0), v_past, preferred_element_type=F32)
                + jnp.dot(jnp.concatenate(p_new, axis=0), v_new, preferred_element_type=F32))

    kn, vn = new_rows(ksn_ref)
    s_past = lax.dot_general(lhs[...], kaug[...], _NT, preferred_element_type=F32)
    s_new = lax.dot_general(qpad, kn, _NT, preferred_element_type=F32)
    o_s = joint_attention(s_past, None, vsel[...], s_new, vn)

    n_buf = win_ref.shape[0]
    kwn, vwn = new_rows(kwn_ref)
    kwb = win_ref[:, 0:LANES].astype(MXU_DTYPE)
    vwb = win_ref[:, LANES:2 * LANES].astype(MXU_DTYPE)
    s_wb = lax.dot_general(qpad, kwb, _NT, preferred_element_type=F32)
    s_wn = lax.dot_general(qpad, kwn, _NT, preferred_element_type=F32)
    pos_w = past - n_buf + lax.broadcasted_iota(jnp.int32, (rows, n_buf), 1)
    dlt = pos_q - pos_w
    m_wb = (dlt >= 0) & (dlt <= WINDOW) & (pos_w >= 0)
    o_w = joint_attention(s_wb, m_wb, vwb, s_wn, vwn)

    for m, slab in enumerate(_gated_output(
            jnp.concatenate([gate_ref[0], jnp.zeros((rows - t_len, LANES), F32)], axis=0), o_c, o_s, o_w, rows)):
        out_ref[0, :, m * LANES:(m + 1) * LANES] = slab[0:t_len]

    nwin_ref[0:n_buf - t_len, :] = win_ref[t_len:n_buf, :]
    nwin_ref[n_buf - t_len:n_buf, :] = kwn_ref[0]


def _nsa_sample(page_table, q, gate, kvs_new, kvw_new, w_cmp, pool_c, pool_s, win, layer):
    dbs, t_len, d_b = q.shape
    n_pages = page_table.shape[1]
    page, kv_w = pool_c.shape[2], pool_c.shape[3]
    n_buf = win.shape[2]
    past = n_pages * page
    tok = lambda w: pl.BlockSpec((1, t_len, w), lambda b, pt: (b, 0, 0))

    def page_spec(p):
        return pl.BlockSpec((None, None, page, kv_w), lambda b, pt, p=p: (layer, pt[b * n_pages + p], 0, 0))

    grid_spec = pltpu.PrefetchScalarGridSpec(
        num_scalar_prefetch=1,
        grid=(dbs,),
        in_specs=[tok(d_b), tok(LANES), tok(kv_w), tok(kv_w),
                  pl.BlockSpec(w_cmp.shape, lambda b, pt: (0, 0)),
                  pl.BlockSpec((None, None, n_buf, kv_w), lambda b, pt: (layer, b, 0, 0))]
                 + [page_spec(p) for p in range(n_pages)] * 2,
        out_specs=[tok(d_b), pl.BlockSpec((None, n_buf, kv_w), lambda b, pt: (b, 0, 0))],
        scratch_shapes=[pltpu.VMEM((past, 2 * LANES), MXU_DTYPE), pltpu.VMEM((past, LANES), MXU_DTYPE),
                        pltpu.VMEM((past // CMP_BLOCK, kv_w), F32), pltpu.VMEM((N_HEADS * 8, 2 * LANES), MXU_DTYPE)],
    )
    return pl.pallas_call(
        functools.partial(_nsa_sample_kernel, n_pages=n_pages, page=page, past=past, t_len=t_len),
        grid_spec=grid_spec,
        out_shape=[jax.ShapeDtypeStruct((dbs, t_len, d_b), F32), jax.ShapeDtypeStruct((dbs, n_buf, kv_w), F32)],
        compiler_params=_cparams("arbitrary"),
        name="nsa_sample",
    )(page_table.reshape(-1), q, gate, kvs_new, kvw_new, w_cmp, win, *([pool_c] * n_pages), *([pool_s] * n_pages))


def _out_proj_router_kernel(*refs, n_x, n_ptiles):
    x_refs = refs[:n_x]
    (oa_ref, ob_ref, oc_ref, oas_ref, obs_ref, ocs_ref, wo_ref, g2_ref, wr_ref, br_ref,
     xn_ref, h2_ref, ri_ref, rw_ref, cnt_ref, run_ref) = refs[n_x:]
    tm = oa_ref.shape[0]

    @pl.when(pl.program_id(0) == 0)
    def _():
        run_ref[...] = jnp.zeros(run_ref.shape, F32)

    is_sample = pl.program_id(0) >= n_ptiles
    mix = jnp.concatenate([jnp.where(is_sample, oas_ref[...], oa_ref[...]),
                           jnp.where(is_sample, obs_ref[...], ob_ref[...]),
                           jnp.where(is_sample, ocs_ref[...], oc_ref[...])], axis=1)
    xn = _token_tile(x_refs, n_ptiles) + _split_dot(_split_act(mix, wo_ref.shape[0]), wo_ref)
    xn_ref[...] = xn
    h2 = _rmsnorm(xn, g2_ref[...])
    _store_token_tiles(h2_ref, h2)
    logits = _split_dot(_split_act(h2, wr_ref.shape[0]), wr_ref) + br_ref[...]

    lane = lax.broadcasted_iota(jnp.int32, (tm, LANES), 1)
    lanef = lane.astype(F32)

    def softmax_over(mask):
        m = jnp.max(jnp.where(mask, logits, -jnp.inf), axis=-1, keepdims=True)
        e = jnp.where(mask, jnp.exp(logits - m), 0.0)
        return e / jnp.sum(e, axis=-1, keepdims=True)

    def first_max(p, mask):
        pm = jnp.max(jnp.where(mask, p, -1.0), axis=-1, keepdims=True)
        idx = jnp.min(jnp.where(mask & (p == pm), lanef, 1e9), axis=-1, keepdims=True)
        return pm, idx

    is_g = lane < N_GROUPS
    g_wt, g_sel = first_max(softmax_over(is_g), is_g)
    lo = N_GROUPS + EXP_PER_GROUP * g_sel
    in_e = (lanef >= lo) & (lanef < lo + EXP_PER_GROUP)
    p_e = softmax_over(in_e)
    p0, i0 = first_max(p_e, in_e)
    p1, i1 = first_max(p_e, in_e & (lanef != i0))
    den = p0 + p1
    e0 = i0 - N_GROUPS
    e1 = i1 - N_GROUPS

    hit0 = lanef == e0
    hit1 = lanef == e1
    onehot = jnp.where(hit0 | hit1, 1.0, 0.0)
    r_i = lax.broadcasted_iota(jnp.int32, (tm, tm), 0)
    c_i = lax.broadcasted_iota(jnp.int32, (tm, tm), 1)
    ltri = jnp.where(c_i < r_i, 1.0, 0.0).astype(jnp.bfloat16)
    before = jnp.dot(ltri, onehot.astype(jnp.bfloat16), preferred_element_type=F32) + run_ref[0:1, :]
    r0 = jnp.sum(jnp.where(hit0, before, 0.0), axis=-1, keepdims=True)
    r1 = jnp.sum(jnp.where(hit1, before, 0.0), axis=-1, keepdims=True)
    run_ref[...] = run_ref[...] + jnp.sum(onehot, axis=0, keepdims=True)
    cnt_ref[...] = run_ref[...]

    ri = jnp.where(lane == 0, e0, jnp.where(lane == 1, e1, jnp.where(lane == 2, r0, jnp.where(lane == 3, r1, 0.0))))
    ri_ref[...] = ri.T[0:8].astype(jnp.int32)
    rw_ref[...] = jnp.where(lane == 0, g_wt * p0 / den, jnp.where(lane == 1, g_wt * p1 / den, 0.0))


def _out_proj_router(x, mix_p, mix_s, w_out, g2, w_route, b_route):
    tm = TOKEN_TILE
    n_ptiles = mix_p[0].shape[0] // tm
    x_ops, x_specs = _token_specs(x, tm, n_ptiles)
    n, d = sum(a.shape[0] for a in x_ops), x_ops[0].shape[1]
    assert all(a.shape[0] == tm for a in mix_s) and n == (n_ptiles + 1) * tm
    row = lambda w: pl.BlockSpec((tm, w), lambda i: (i, 0))
    prow = lambda a: pl.BlockSpec((tm, a.shape[1]), lambda i: (jnp.minimum(i, n_ptiles - 1), 0))
    full = lambda a: pl.BlockSpec(a.shape, lambda i: (0,) * a.ndim)
    return pl.pallas_call(
        functools.partial(_out_proj_router_kernel, n_x=len(x_ops), n_ptiles=n_ptiles),
        grid=(n // tm,),
        in_specs=x_specs + [prow(a) for a in mix_p] + [full(a) for a in mix_s]
                 + [full(w_out), full(g2), full(w_route), full(b_route)],
        out_specs=[row(d), pl.BlockSpec((tm * SUBLANES, LANES), lambda i: (i, 0)),
                   pl.BlockSpec((8, tm), lambda i: (0, i)), row(LANES), pl.BlockSpec((8, LANES), lambda i: (0, 0))],
        out_shape=[jax.ShapeDtypeStruct((n, d), F32), jax.ShapeDtypeStruct((n * SUBLANES, LANES), F32),
                   jax.ShapeDtypeStruct((8, n), jnp.int32), jax.ShapeDtypeStruct((n, LANES), F32),
                   jax.ShapeDtypeStruct((8, LANES), F32)],
        scratch_shapes=[pltpu.VMEM((8, LANES), F32)],
        compiler_params=_cparams("arbitrary"),
        name="out_proj_router",
    )(*x_ops, *mix_p, *mix_s, w_out, g2, w_route, b_route)


def _store_token_tiles(ref, x):
    t = x.shape[0]
    for c in range(SUBLANES):
        ref[pl.ds(c, t, stride=SUBLANES), :] = x[:, c * LANES:(c + 1) * LANES]


def _load_token_tiles(ref, first_row, t):
    return jnp.concatenate([ref[pl.ds(first_row + c, t, stride=SUBLANES), :] for c in range(SUBLANES)], axis=1)


def _row_copy(src_hbm, row, dst, slot, r, sem):
    return pltpu.make_async_copy(src_hbm.at[pl.ds(pl.multiple_of(row * SUBLANES, SUBLANES), SUBLANES)],
                                 dst.at[slot, pl.ds(pl.multiple_of(r * SUBLANES, SUBLANES), SUBLANES)],
                                 sem.at[slot])


def _expert_kernel(te_ref, nv_ref, src_ref, h_hbm, wg_ref, wu_ref, wd_ref, ys_ref, xbuf, sem):
    t = pl.program_id(0)
    nv = nv_ref[0]
    te = xbuf.shape[1] // SUBLANES

    def issue(tile, slot):
        def body(i, carry):
            for j in range(2):
                r = 2 * i + j
                _row_copy(h_hbm, src_ref[tile * te + r], xbuf, slot, r, sem).start(priority=j)
            return carry
        lax.fori_loop(0, te // 2, body, 0, unroll=4)

    def wait_all(slot):
        pltpu.make_async_copy(h_hbm.at[pl.ds(0, te * SUBLANES)], xbuf.at[slot], sem.at[slot]).wait()

    @pl.when((t == 0) & (nv > 0))
    def _():
        issue(0, 0)

    @pl.when(t + 1 < nv)
    def _():
        issue(t + 1, (t + 1) % 2)

    @pl.when(t < nv)
    def _():
        slot = t % 2
        wait_all(slot)
        x = _load_token_tiles(xbuf.at[slot], 0, te).astype(MXU_DTYPE)
        hg = jnp.dot(x, wg_ref[...].astype(MXU_DTYPE), preferred_element_type=F32)
        hu = jnp.dot(x, wu_ref[...].astype(MXU_DTYPE), preferred_element_type=F32)
        act = (jax.nn.silu(hg) * hu).astype(MXU_DTYPE)
        _store_token_tiles(ys_ref, jnp.dot(act, wd_ref[...].astype(MXU_DTYPE), preferred_element_type=F32))

    @pl.when(t >= nv)
    def _():
        ys_ref[...] = jnp.zeros(ys_ref.shape, F32)


def _experts(tile_expert, n_valid, src_tok, h2, w_gate, w_up, w_down, layer):
    n_tiles = tile_expert.shape[0]
    te = EXPERT_TILE
    d = SUBLANES * LANES
    d_e = w_gate.shape[3]
    assert w_gate.shape[2] == d and h2.shape[1] == LANES
    wspec = lambda r, c: pl.BlockSpec((None, None, r, c), lambda t, te_r, nv_r, src_r: (layer, te_r[t], 0, 0))
    grid_spec = pltpu.PrefetchScalarGridSpec(
        num_scalar_prefetch=3,
        grid=(n_tiles,),
        in_specs=[pl.BlockSpec(memory_space=pl.ANY), wspec(d, d_e), wspec(d, d_e), wspec(d_e, d)],
        out_specs=pl.BlockSpec((te * SUBLANES, LANES), lambda t, te_r, nv_r, src_r: (t, 0)),
        scratch_shapes=[pltpu.VMEM((2, te * SUBLANES, LANES), F32), pltpu.SemaphoreType.DMA((2,))],
    )
    return pl.pallas_call(
        _expert_kernel,
        grid_spec=grid_spec,
        out_shape=jax.ShapeDtypeStruct((n_tiles * te * SUBLANES, LANES), F32),
        compiler_params=_cparams("arbitrary"),
        name="moe_experts",
    )(tile_expert, n_valid, src_tok, h2, w_gate, w_up, w_down)


def _combine_kernel(d0_ref, d1_ref, x_ref, rw_ref, gf_ref, ys_hbm, *rest, final):
    if final:
        xo_ref, y_ref, buf, sem = rest
    else:
        xo_ref, buf, sem = rest
    t = pl.program_id(0)
    nt = pl.num_programs(0)
    tm = x_ref.shape[0]

    def issue(tile, slot):
        def body(r, carry):
            _row_copy(ys_hbm, d0_ref[tile * tm + r], buf, slot, r, sem).start(priority=0)
            _row_copy(ys_hbm, d1_ref[tile * tm + r], buf, slot, tm + r, sem).start(priority=1)
            return carry
        lax.fori_loop(0, tm, body, 0, unroll=8)

    def wait_all(slot):
        pltpu.make_async_copy(ys_hbm.at[pl.ds(0, 2 * tm * SUBLANES)], buf.at[slot], sem.at[slot]).wait()

    @pl.when(t == 0)
    def _():
        issue(0, 0)

    @pl.when(t + 1 < nt)
    def _():
        issue(t + 1, (t + 1) % 2)

    slot = t % 2
    wait_all(slot)
    rw = rw_ref[...]
    xo = (x_ref[...] + rw[:, 0:1] * _load_token_tiles(buf.at[slot], 0, tm)
          + rw[:, 1:2] * _load_token_tiles(buf.at[slot], tm * SUBLANES, tm))
    xo_ref[...] = xo
    if final:
        y_ref[...] = _rmsnorm(xo, gf_ref[...])


def _combine(d0, d1, x, rw, gf, ys, final):
    n, d = x.shape
    tm = COMBINE_TILE
    row = lambda w: pl.BlockSpec((tm, w), lambda t, a, b: (t, 0))
    n_out = 2 if final else 1
    grid_spec = pltpu.PrefetchScalarGridSpec(
        num_scalar_prefetch=2,
        grid=(n // tm,),
        in_specs=[row(d), row(LANES), pl.BlockSpec((1, d), lambda t, a, b: (0, 0)), pl.BlockSpec(memory_space=pl.ANY)],
        out_specs=[row(d)] * n_out,
        scratch_shapes=[pltpu.VMEM((2, 2 * tm * SUBLANES, LANES), F32), pltpu.SemaphoreType.DMA((2,))],
    )
    return pl.pallas_call(
        functools.partial(_combine_kernel, final=final),
        grid_spec=grid_spec,
        out_shape=[jax.ShapeDtypeStruct((n, d), F32)] * n_out,
        compiler_params=_cparams("arbitrary"),
        name="moe_combine_final" if final else "moe_combine",
    )(d0, d1, x, rw, gf, ys)


def _route_plan(ri, cnt, n_tiles):
    te = EXPERT_TILE
    n = ri.shape[1]
    counts = cnt[0, :N_EXPERTS].astype(jnp.int32)
    padded = ((counts + te - 1) // te) * te
    ends = jnp.cumsum(padded)
    offs = ends - padded
    d0 = offs[ri[0]] + ri[2]
    d1 = offs[ri[1]] + ri[3]
    tok = jnp.arange(n, dtype=jnp.int32)
    src = jnp.zeros((n_tiles * te,), jnp.int32).at[jnp.concatenate([d0, d1])].set(jnp.concatenate([tok, tok]))
    tile_start = jnp.arange(n_tiles, dtype=jnp.int32) * te
    tile_expert = jnp.minimum(jnp.sum((ends[None, :] <= tile_start[:, None]).astype(jnp.int32), axis=1),
                              N_EXPERTS - 1)
    n_valid = (ends[-1:] // te).astype(jnp.int32)
    return d0, d1, src, tile_expert, n_valid


def _rope_tables(pos):
    half = HEAD_DIM // 2
    inv = ROPE_THETA ** (-jnp.arange(half, dtype=F32) / half)
    ang = pos.astype(F32)[:, None] * inv[None, :]
    cos = jnp.cos(ang)
    sin = jnp.sin(ang)
    cos_t = jnp.concatenate([cos, cos] * (LANES // HEAD_DIM), axis=1)
    sin_t = jnp.concatenate([-sin, sin] * (LANES // HEAD_DIM), axis=1)
    return cos_t, sin_t


def _block_diag(w):
    nb, bw, _ = w.shape
    out = jnp.zeros((nb * bw, nb * bw), w.dtype)
    for i in range(nb):
        out = out.at[i * bw:(i + 1) * bw, i * bw:(i + 1) * bw].set(w[i])
    return out


def kernel(x_prompt, x_sample, cache_kv_cmp, cache_kv_sel, cache_kv_win, state_lru_h, state_lru_conv, state_sconv, page_table, norm1_g, w_in, lru_conv_w, lru_conv_b, lru_wa, lru_ba, lru_wx, lru_bx, lru_lambda, nsa_cmp_wk, nsa_cmp_wv, sc_conv_w, w_out, norm2_g, router_group_w, router_group_b, router_exp_w, router_exp_b, exp_w_gate, exp_w_up, exp_w_down, norm_f_g):
    bsz, seq, d = x_prompt.shape
    dbs, t_len, _ = x_sample.shape
    depth = w_in.shape[0]
    d_a = lru_conv_w.shape[2]
    d_c = sc_conv_w.shape[2]
    kv_w = 2 * N_KV * HEAD_DIM
    d_b = N_HEADS * HEAD_DIM
    n_gate = 3 * N_HEADS
    n_p = bsz * seq
    n_s = dbs * t_len
    n = n_p + n_s
    assert n_s == TOKEN_TILE and n_p % TOKEN_TILE == 0 and seq % TOKEN_TILE == 0
    page = cache_kv_cmp.shape[2]
    past = page_table.shape[1] * page
    n_buf = cache_kv_win.shape[2]

    pos = jnp.concatenate([jnp.arange(seq, dtype=jnp.int32),
                           jnp.tile(past + jnp.arange(t_len, dtype=jnp.int32), dbs)])
    cos_t, sin_t = _rope_tables(pos)

    pool_c = cache_kv_cmp.reshape(depth, -1, page, kv_w).astype(MXU_DTYPE)
    pool_s = cache_kv_sel.reshape(depth, -1, page, kv_w).astype(MXU_DTYPE)
    win = cache_kv_win.reshape(depth, dbs, n_buf, kv_w)

    g_off = 2 * d_a + d_b + 3 * kv_w
    x = (x_prompt.reshape(n_p, d), x_sample.reshape(n_s, d))
    h0_p = jnp.zeros((bsz, 1, d_a), F32)
    lb0_p = jnp.zeros((bsz, lru_conv_w.shape[1] - 1, d_a), F32)
    sb0_p = jnp.zeros((bsz, sc_conv_w.shape[1] - 1, d_c), F32)

    n_tiles = (2 * n) // EXPERT_TILE + N_EXPERTS
    proj_parts = lambda l: 2 if l < depth - 1 else 1
    states_p, states_s = [], []
    y = None
    for l in range(depth):
        w_l = w_in[l]
        w_r = _split_weight(jnp.concatenate([w_l[:, :g_off], w_l[:, g_off + n_gate:], w_l[:, g_off:g_off + n_gate],
                                             jnp.zeros((d, LANES - n_gate), F32)], axis=1), proj_parts(l))
        lru_in, sc_in, gate, q, kvc, kvs, kvw, kvs_b, kvw_b = _in_proj(
            x, norm1_g[l][None], w_r, cos_t, sin_t, n_p, seq, d_a=d_a, d_b=d_b, kv_w=kv_w, d_c=d_c)

        seq_w = (lru_conv_w[l], lru_conv_b[l][None], _block_diag(lru_wa[l]).astype(MXU_DTYPE), lru_ba[l][None],
                 _block_diag(lru_wx[l]).astype(MXU_DTYPE), lru_bx[l][None], lru_lambda[l][None], sc_conv_w[l])
        out_a, out_c, hn_p, lbn_p, sbn_p = _seq_mix_prompt(lru_in, sc_in, h0_p, lb0_p, sb0_p, seq_w, bsz, seq,
                                                           d_a=d_a, d_c=d_c)
        tmaj = lambda a: jnp.swapaxes(a.reshape(dbs, t_len, -1), 0, 1)
        oa_s, oc_s, hn_s, lbn_s, sbn_s = _seq_mix_sample(
            tmaj(lru_in[n_p:]), tmaj(sc_in[n_p:]), state_lru_h[l], jnp.swapaxes(state_lru_conv[l], 0, 1),
            jnp.swapaxes(state_sconv[l], 0, 1), seq_w, d_a=d_a, d_c=d_c)
        oa_s = jnp.swapaxes(oa_s, 0, 1).reshape(n_s, d_a)
        oc_s = jnp.swapaxes(oc_s, 0, 1).reshape(n_s, d_c)

        w_cmp = jnp.concatenate([jnp.broadcast_to(nsa_cmp_wk[l][:, None], (CMP_BLOCK, kv_w // 2)),
                                 jnp.broadcast_to(nsa_cmp_wv[l][:, None], (CMP_BLOCK, kv_w // 2))], axis=1)
        kcv = _compress_prompt(kvc, w_cmp, bsz, seq)
        out_b = _nsa_prompt(q, gate, kcv, kvs_b, kvw_b, bsz, seq)
        s3 = lambda a: a[n_p:].reshape(dbs, t_len, -1)
        ob_s, nwin_s = _nsa_sample(page_table, s3(q), s3(gate), s3(kvs), s3(kvw), w_cmp, pool_c, pool_s, win, l)
        ob_s = ob_s.reshape(n_s, d_b)

        w_route = _split_weight(jnp.concatenate([router_group_w[l], router_exp_w[l],
                                                 jnp.zeros((d, LANES - N_GROUPS - N_EXPERTS), F32)], axis=1), 2)
        b_route = jnp.concatenate([router_group_b[l], router_exp_b[l],
                                   jnp.zeros((LANES - N_GROUPS - N_EXPERTS,), F32)])[None]
        xn, h2, ri, rw, cnt = _out_proj_router(x, (out_a, out_b, out_c), (oa_s, ob_s, oc_s),
                                               _split_weight(w_out[l], proj_parts(l)), norm2_g[l][None],
                                               w_route, b_route)
        d0, d1, src, tile_expert, n_valid = _route_plan(ri, cnt, n_tiles)
        ys = _experts(tile_expert, n_valid, src, h2, exp_w_gate, exp_w_up, exp_w_down, l)
        final = l == depth - 1
        outs = _combine(d0, d1, xn, rw, norm_f_g[None], ys, final)
        x = outs[0]
        if final:
            y = outs[1]

        kv6 = lambda a, lead: a.reshape(lead + (2, N_KV, HEAD_DIM))
        states_p.append((kv6(kvc[:n_p], (bsz, seq)), kv6(kvs[:n_p], (bsz, seq)),
                         kv6(kvw[:n_p].reshape(bsz, seq, kv_w)[:, seq - min(WINDOW, seq):], (bsz, min(WINDOW, seq))),
                         hn_p[:, 0], lbn_p, sbn_p))
        states_s.append((kv6(kvc[n_p:], (dbs, t_len)), kv6(kvs[n_p:], (dbs, t_len)), kv6(nwin_s, (dbs, n_buf)),
                         hn_s, jnp.swapaxes(lbn_s, 0, 1), jnp.swapaxes(sbn_s, 0, 1)))

    stack = lambda sts, i: jnp.stack([s[i] for s in sts])
    res = [y[:n_p].reshape(bsz, seq, d), y[n_p:].reshape(dbs, t_len, d)]
    for i in range(6):
        res += [stack(states_p, i), stack(states_s, i)]
    return tuple(res)
```

```python
import functools

import jax
import jax.numpy as jnp
from jax import lax
from jax.experimental import pallas as pl
from jax.experimental.pallas import tpu as pltpu

F32 = jnp.float32
MXU_DTYPE = jnp.bfloat16

HEAD_DIM = 64
N_HEADS = 8
N_KV = 2
GROUP = N_HEADS // N_KV
CMP_BLOCK = 32
SEL_BLOCK = 64
TOP_N = 16
WINDOW = 512
Q_BLOCK = 128
ROPE_THETA = 10000.0
LRU_C = 8.0
N_GROUPS = 4
EXP_PER_GROUP = 8
N_EXPERTS = N_GROUPS * EXP_PER_GROUP
RMS_EPS = 1e-6
NEG_INF = -1e30
TINY = 1e-30
FORCE = 1e6

LANES = 128
SUBLANES = 8
SUM_ROWS = 2 * SUBLANES
VMEM_LIMIT = 56 * 2 ** 20
TOKEN_TILE = 512
TIME_CHUNK = 256
KEY_TILE = 512
EXPERT_TILE = 256
COMBINE_TILE = 256

_NT = (((1,), (1,)), ((), ()))


def _cparams(*sem):
    return pltpu.CompilerParams(dimension_semantics=sem, vmem_limit_bytes=VMEM_LIMIT)


def _split_weight(w, parts):
    if parts == 1:
        return w.astype(MXU_DTYPE)[None]
    hi = lax.bitcast_convert_type(lax.bitcast_convert_type(w, jnp.uint32) & jnp.uint32(0xFFFF0000), F32)
    return jnp.stack([hi.astype(MXU_DTYPE), (w - hi).astype(MXU_DTYPE)])


def _split_act(a, parts):
    if parts == 1:
        return (a.astype(MXU_DTYPE),)
    hi = lax.bitcast_convert_type(lax.bitcast_convert_type(a, jnp.uint32) & jnp.uint32(0xFFFF0000), F32)
    return hi.astype(MXU_DTYPE), (a - hi).astype(MXU_DTYPE)


def _split_dot(a_parts, w_ref, cols=slice(None)):
    dot = lambda a, p: jnp.dot(a, w_ref[p, :, cols], preferred_element_type=F32)
    if len(a_parts) == 1:
        return dot(a_parts[0], 0)
    return (dot(a_parts[1], 1) + (dot(a_parts[0], 1) + dot(a_parts[1], 0))) + dot(a_parts[0], 0)


def _rmsnorm(x, g):
    return x * lax.rsqrt(jnp.mean(x * x, axis=-1, keepdims=True) + RMS_EPS) * g


def _swap_halves(x):
    w = x.shape[-1]
    lane = lax.broadcasted_iota(jnp.int32, x.shape, x.ndim - 1)
    first = (lane % HEAD_DIM) < HEAD_DIM // 2
    return jnp.where(first, pltpu.roll(x, w - HEAD_DIM // 2, axis=1), pltpu.roll(x, HEAD_DIM // 2, axis=1))


def _masked_softmax(s, mask):
    s = jnp.where(mask, s, NEG_INF)
    m = jnp.max(s, axis=-1, keepdims=True)
    p = jnp.where(mask, jnp.exp(s - m), 0.0)
    return p / jnp.maximum(jnp.sum(p, axis=-1, keepdims=True), TINY)


def _select_blocks_keys(imp, pos_q, n_sel):
    blk = lax.broadcasted_iota(jnp.int32, imp.shape, 0)
    cur = pos_q // SEL_BLOCK
    forced = (blk == 0) | (blk == cur) | (blk == cur - 1)
    w = jnp.where(forced, FORCE, imp)
    w = jnp.where(blk <= cur, w, NEG_INF)
    w = jnp.where(blk < n_sel, w, -jnp.inf)
    blkf = blk.astype(F32)

    def body(_, carry):
        w, sel = carry
        m = jnp.max(w, axis=0, keepdims=True)
        first = jnp.min(jnp.where(w == m, blkf, 1e9), axis=0, keepdims=True)
        pick = blkf == first
        return jnp.where(pick, -jnp.inf, w), jnp.where(pick, 1.0, sel)

    _, sel = lax.fori_loop(0, min(TOP_N, n_sel), body, (w, jnp.zeros_like(w)))
    return sel


def _token_specs(x, tm, n_ptiles):
    if not isinstance(x, tuple):
        return [x], [pl.BlockSpec((tm, x.shape[1]), lambda i: (i, 0))]
    x_p, x_s = x
    assert x_p.shape[0] == n_ptiles * tm and x_s.shape[0] == tm
    return [x_p, x_s], [pl.BlockSpec((tm, x_p.shape[1]), lambda i: (jnp.minimum(i, n_ptiles - 1), 0)),
                        pl.BlockSpec((tm, x_s.shape[1]), lambda i: (0, 0))]


def _token_tile(x_refs, n_ptiles):
    if len(x_refs) == 1:
        return x_refs[0][...]
    return jnp.where(pl.program_id(0) >= n_ptiles, x_refs[1][...], x_refs[0][...])


def _in_proj_kernel(*refs, d_a, d_b, kv_w, d_c, n_x, n_ptiles):
    x_refs = refs[:n_x]
    (g_ref, w_ref, cos_ref, sin_ref,
     lru_ref, sc_ref, gate_ref, q_ref, kvc_ref, kvs_ref, kvw_ref, kvsb_ref, kvwb_ref) = refs[n_x:]
    h_parts = _split_act(_rmsnorm(_token_tile(x_refs, n_ptiles), g_ref[...]), w_ref.shape[0])

    def mm(a, b):
        return _split_dot(h_parts, w_ref, slice(a, b))

    cos = cos_ref[...]
    sin = sin_ref[...]
    off = 2 * d_a
    lru_ref[...] = mm(0, off)
    q = mm(off, off + d_b)
    reps = d_b // LANES
    cos_q = jnp.concatenate([cos] * reps, axis=1)
    sin_q = jnp.concatenate([sin] * reps, axis=1)
    q_ref[...] = (q * cos_q + _swap_halves(q) * sin_q) * (HEAD_DIM ** -0.5)
    off += d_b
    for ref, bref in ((kvc_ref, None), (kvs_ref, kvsb_ref), (kvw_ref, kvwb_ref)):
        kv = mm(off, off + kv_w)
        k = kv[:, :kv_w // 2]
        kv = jnp.concatenate([k * cos + _swap_halves(k) * sin, kv[:, kv_w // 2:]], axis=1)
        ref[...] = kv
        if bref is not None:
            bref[...] = kv.astype(MXU_DTYPE)
        off += kv_w
    sc_ref[...] = mm(off, off + 3 * d_c)
    off += 3 * d_c
    gate_ref[...] = jax.nn.sigmoid(mm(off, off + LANES))


def _in_proj(x, g1, w_r, cos_t, sin_t, n_prompt, seq, *, d_a, d_b, kv_w, d_c):
    tm = TOKEN_TILE
    n_ptiles = n_prompt // tm
    n_stiles = seq // tm
    x_ops, x_specs = _token_specs(x, tm, n_ptiles)
    n, d = sum(a.shape[0] for a in x_ops), x_ops[0].shape[1]

    def tab_map(i):
        return (jnp.where(i < n_ptiles, i % n_stiles, n_stiles), 0)

    row = lambda w: pl.BlockSpec((tm, w), lambda i: (i, 0))
    widths = (2 * d_a, 3 * d_c, LANES, d_b, kv_w, kv_w, kv_w, kv_w, kv_w)
    dtypes = (F32, F32, F32, F32, F32, F32, F32, MXU_DTYPE, MXU_DTYPE)
    return pl.pallas_call(
        functools.partial(_in_proj_kernel, d_a=d_a, d_b=d_b, kv_w=kv_w, d_c=d_c, n_x=len(x_ops), n_ptiles=n_ptiles),
        grid=(n // tm,),
        in_specs=x_specs + [
                  pl.BlockSpec((1, d), lambda i: (0, 0)),
                  pl.BlockSpec(w_r.shape, lambda i: (0, 0, 0)),
                  pl.BlockSpec((tm, LANES), tab_map),
                  pl.BlockSpec((tm, LANES), tab_map)],
        out_specs=[row(w) for w in widths],
        out_shape=[jax.ShapeDtypeStruct((n, w), dt) for w, dt in zip(widths, dtypes)],
        compiler_params=_cparams("parallel"),
        name="in_proj",
    )(*x_ops, g1, w_r, cos_t, sin_t)


def _scan_rows(a, u):
    t = a.shape[0]
    row = lax.broadcasted_iota(jnp.int32, a.shape, 0)
    d = 1
    while d < t:
        keep = row >= d
        a_sh = jnp.where(keep, pltpu.roll(a, d, axis=0), 1.0)
        u_sh = jnp.where(keep, pltpu.roll(u, d, axis=0), 0.0)
        u = a * u_sh + u
        a = a * a_sh
        d *= 2
    return a, u


def _lru_gates(xc, wa_ref, ba_ref, wx_ref, bx_ref, lam_ref):
    xb = xc.astype(MXU_DTYPE)
    r = jax.nn.sigmoid(jnp.dot(xb, wa_ref[...], preferred_element_type=F32) + ba_ref[...])
    i = jax.nn.sigmoid(jnp.dot(xb, wx_ref[...], preferred_element_type=F32) + bx_ref[...])
    log_a = -LRU_C * r * jax.nn.softplus(-lam_ref[...])
    a = jnp.exp(log_a)
    th = jnp.tanh(log_a)
    u = jnp.sqrt(-2.0 * th / (1.0 - th)) * (i * xc)
    return a, u


def _seq_mix_kernel(lru_ref, sc_ref, h0_ref, lb0_ref, sb0_ref, cw_ref, cb_ref, wa_ref, ba_ref, wx_ref, bx_ref,
                    lam_ref, scw_ref, oa_ref, oc_ref, hn_ref, lbn_ref, sbn_ref, xpad, vpad, hcar, *, d_a, d_c):
    c = pl.program_id(1)
    tc = lru_ref.shape[0]
    nlb = lb0_ref.shape[1]
    nsb = sb0_ref.shape[1]

    @pl.when(c == 0)
    def _():
        xpad[0:8, :] = jnp.zeros((8, d_a), F32)
        xpad[8 - nlb:8, :] = lb0_ref[0]
        vpad[0:8, :] = jnp.zeros((8, d_c), F32)
        vpad[8 - nsb:8, :] = sb0_ref[0]
        hcar[...] = h0_ref[0]

    xa = lru_ref[:, 0:d_a]
    ga = lru_ref[:, d_a:2 * d_a]
    xpad[8:8 + tc, :] = xa
    xc = cw_ref[0:1, :] * xpad[8 - nlb:8 - nlb + tc, :]
    for j in range(1, nlb):
        xc = xc + cw_ref[j:j + 1, :] * xpad[8 - nlb + j:8 - nlb + j + tc, :]
    xc = xc + cw_ref[nlb:nlb + 1, :] * xa + cb_ref[...]
    a, u = _lru_gates(xc, wa_ref, ba_ref, wx_ref, bx_ref, lam_ref)
    a_cum, hs = _scan_rows(a, u)
    hs = hs + a_cum * hcar[...]
    hcar[...] = hs[tc - 1:tc, :]
    oa_ref[...] = hs * jax.nn.gelu(ga)

    v = sc_ref[:, 0:d_c] * sc_ref[:, 2 * d_c:3 * d_c]
    vpad[8:8 + tc, :] = v
    uc = scw_ref[0:1, :] * vpad[8 - nsb:8 - nsb + tc, :]
    for j in range(1, nsb):
        uc = uc + scw_ref[j:j + 1, :] * vpad[8 - nsb + j:8 - nsb + j + tc, :]
    uc = uc + scw_ref[nsb:nsb + 1, :] * v
    oc_ref[...] = sc_ref[:, d_c:2 * d_c] * uc

    hn_ref[0] = hs[tc - 1:tc, :]
    lbn_ref[0] = xpad[8 + tc - nlb:8 + tc, :]
    sbn_ref[0] = vpad[8 + tc - nsb:8 + tc, :]
    xpad[0:8, :] = xpad[tc:tc + 8, :]
    vpad[0:8, :] = vpad[tc:tc + 8, :]


def _seq_mix_prompt(lru_in, sc_in, h0, lb0, sb0, wts, bsz, seq, *, d_a, d_c):
    n = bsz * seq
    tc = TIME_CHUNK
    nch = seq // tc
    row = lambda w: pl.BlockSpec((tc, w), lambda b, c: (b * nch + c, 0))
    full = lambda a: pl.BlockSpec(a.shape, lambda b, c: (0,) * a.ndim)
    state = lambda r, w: pl.BlockSpec((1, r, w), lambda b, c: (b, 0, 0))
    nlb, nsb = lb0.shape[1], sb0.shape[1]
    return pl.pallas_call(
        functools.partial(_seq_mix_kernel, d_a=d_a, d_c=d_c),
        grid=(bsz, nch),
        in_specs=[row(2 * d_a), row(3 * d_c), state(1, d_a), state(nlb, d_a), state(nsb, d_c)]
                 + [full(w) for w in wts],
        out_specs=[row(d_a), row(d_c), state(1, d_a), state(nlb, d_a), state(nsb, d_c)],
        out_shape=[jax.ShapeDtypeStruct((n, d_a), F32), jax.ShapeDtypeStruct((n, d_c), F32),
                   jax.ShapeDtypeStruct((bsz, 1, d_a), F32), jax.ShapeDtypeStruct((bsz, nlb, d_a), F32),
                   jax.ShapeDtypeStruct((bsz, nsb, d_c), F32)],
        scratch_shapes=[pltpu.VMEM((tc + 8, d_a), F32), pltpu.VMEM((tc + 8, d_c), F32), pltpu.VMEM((1, d_a), F32)],
        compiler_params=_cparams("arbitrary", "arbitrary"),
        name="seq_mix_prompt",
    )(lru_in, sc_in, h0, lb0, sb0, *wts)


def _seq_mix_sample_kernel(lru_ref, sc_ref, h0_ref, lb0_ref, sb0_ref, cw_ref, cb_ref, wa_ref, ba_ref, wx_ref,
                           bx_ref, lam_ref, scw_ref, oa_ref, oc_ref, hn_ref, lbn_ref, sbn_ref, *, d_a, d_c):
    t_len = lru_ref.shape[0]
    nlb = lb0_ref.shape[0]
    nsb = sb0_ref.shape[0]
    xs = [lb0_ref[j] for j in range(nlb)] + [lru_ref[t][:, 0:d_a] for t in range(t_len)]
    vs = [sb0_ref[j] for j in range(nsb)] + [sc_ref[t][:, 0:d_c] * sc_ref[t][:, 2 * d_c:3 * d_c] for t in range(t_len)]
    h = h0_ref[...]
    for t in range(t_len):
        xc = cw_ref[0:1, :] * xs[t]
        for j in range(1, nlb + 1):
            xc = xc + cw_ref[j:j + 1, :] * xs[t + j]
        xc = xc + cb_ref[...]
        a, u = _lru_gates(xc, wa_ref, ba_ref, wx_ref, bx_ref, lam_ref)
        h = a * h + u
        oa_ref[t] = h * jax.nn.gelu(lru_ref[t][:, d_a:2 * d_a])
        uc = scw_ref[0:1, :] * vs[t]
        for j in range(1, nsb + 1):
            uc = uc + scw_ref[j:j + 1, :] * vs[t + j]
        oc_ref[t] = sc_ref[t][:, d_c:2 * d_c] * uc
    hn_ref[...] = h
    for j in range(nlb):
        lbn_ref[j] = xs[t_len + j]
    for j in range(nsb):
        sbn_ref[j] = vs[t_len + j]


def _seq_mix_sample(lru_t, sc_t, h0, lb0_t, sb0_t, wts, *, d_a, d_c):
    t_len, bsz, _ = lru_t.shape
    outs = [jax.ShapeDtypeStruct((t_len, bsz, d_a), F32), jax.ShapeDtypeStruct((t_len, bsz, d_c), F32),
            jax.ShapeDtypeStruct(h0.shape, F32), jax.ShapeDtypeStruct(lb0_t.shape, F32),
            jax.ShapeDtypeStruct(sb0_t.shape, F32)]
    return pl.pallas_call(
        functools.partial(_seq_mix_sample_kernel, d_a=d_a, d_c=d_c),
        out_shape=outs,
        compiler_params=pltpu.CompilerParams(vmem_limit_bytes=VMEM_LIMIT),
        name="seq_mix_sample",
    )(lru_t, sc_t, h0, lb0_t, sb0_t, *wts)


def _compress_kernel(kvc_ref, w_ref, out_ref):
    half = out_ref.shape[0] // 2
    x = kvc_ref[...].reshape(half, 2 * CMP_BLOCK, out_ref.shape[1])
    w = w_ref[...][None]
    out_ref[0:half, :] = jnp.sum(x[:, 0:CMP_BLOCK, :] * w, axis=1)
    out_ref[half:2 * half, :] = jnp.sum(x[:, CMP_BLOCK:2 * CMP_BLOCK, :] * w, axis=1)


def _compress_prompt(kvc, w_cmp, bsz, seq):
    kv_w = kvc.shape[1]
    n_cmp = seq // CMP_BLOCK
    return pl.pallas_call(
        _compress_kernel,
        grid=(bsz,),
        in_specs=[pl.BlockSpec((seq, kv_w), lambda b: (b, 0)), pl.BlockSpec(w_cmp.shape, lambda b: (0, 0))],
        out_specs=pl.BlockSpec((n_cmp, kv_w), lambda b: (b, 0)),
        out_shape=jax.ShapeDtypeStruct((bsz * n_cmp, kv_w), F32),
        compiler_params=_cparams("parallel"),
        name="nsa_compress",
    )(kvc, w_cmp)


def _pad_heads(q, lhs_ref, rows):
    lane = lax.broadcasted_iota(jnp.int32, (rows, LANES), 1)
    for h in range(N_HEADS):
        k = h // GROUP
        slab = q[:, (h // 2) * LANES:(h // 2 + 1) * LANES]
        if h % 2 != k:
            slab = pltpu.roll(slab, HEAD_DIM, axis=1)
        keep = (lane >= k * HEAD_DIM) & (lane < (k + 1) * HEAD_DIM)
        lhs_ref[h * rows:(h + 1) * rows, 0:LANES] = jnp.where(keep, slab, 0.0).astype(MXU_DTYPE)


def _gated_output(gate, o_c, o_s, o_w, rows):
    lane = lax.broadcasted_iota(jnp.int32, (rows, LANES), 1)
    slabs = []
    for m in range(N_HEADS // 2):
        parts = []
        for h in (2 * m, 2 * m + 1):
            k = h // GROUP
            sl = slice(h * rows, (h + 1) * rows)
            r = (gate[:, 3 * h:3 * h + 1] * o_c[sl] + gate[:, 3 * h + 1:3 * h + 2] * o_s[sl]
                 + gate[:, 3 * h + 2:3 * h + 3] * o_w[sl])
            if h % 2 != k:
                r = pltpu.roll(r, HEAD_DIM, axis=1)
            parts.append(r)
        slabs.append(jnp.where(lane < HEAD_DIM, parts[0], parts[1]))
    return slabs


def _compressed_branch(qpad, kcv, pos0, rows, n_sel):
    n_cmp = kcv.shape[0]
    half = n_cmp // 2
    pos_q = pos0 + lax.broadcasted_iota(jnp.int32, (rows, 1), 0)
    kc = kcv[:, 0:LANES].astype(MXU_DTYPE)
    vc = kcv[:, LANES:2 * LANES].astype(MXU_DTYPE)
    s_c = lax.dot_general(qpad, kc, _NT, preferred_element_type=F32)
    col = lax.broadcasted_iota(jnp.int32, (rows, n_cmp), 1)
    blk = jnp.where(col < half, 2 * col, 2 * (col - half) + 1)
    m_c = (blk + 1) * CMP_BLOCK - 1 <= pos_q
    ps = []
    imp = [jnp.zeros((rows, n_cmp), F32) for _ in range(N_KV)]
    for h in range(N_HEADS):
        p = _masked_softmax(s_c[h * rows:(h + 1) * rows], m_c)
        imp[h // GROUP] = imp[h // GROUP] + p
        ps.append(p.astype(MXU_DTYPE))
    o_c = jnp.dot(jnp.concatenate(ps, axis=0), vc, preferred_element_type=F32)
    imps = []
    for k in range(N_KV):
        imp_s = imp[k][:, 0:half] + imp[k][:, half:n_cmp]
        if half < LANES:
            imp_s = jnp.concatenate([imp_s, jnp.zeros((rows, LANES - half), F32)], axis=1)
        imps.append(imp_s)
    imp_t = jnp.concatenate(imps + [jnp.zeros((LANES - N_KV * rows, LANES), F32)], axis=0).T
    n_rows = -(-n_sel // 8) * 8
    pos_l = pos0 + lax.broadcasted_iota(jnp.int32, (1, LANES), 1) % rows
    sel_t = _select_blocks_keys(imp_t[0:n_rows], pos_l, n_sel)
    sel = jnp.concatenate([sel_t, jnp.zeros((LANES - n_rows, LANES), F32)], axis=0).T
    return o_c, [sel[k * rows:(k + 1) * rows] for k in range(N_KV)]


def _store_selection(lhs_ref, sels, rows):
    for k in range(N_KV):
        neg = jnp.where(sels[k] > 0.0, 0.0, NEG_INF).astype(MXU_DTYPE)
        for g in range(GROUP):
            h = k * GROUP + g
            lhs_ref[h * rows:(h + 1) * rows, LANES:2 * LANES] = neg


def _block_onehot(n_rows):
    row = lax.broadcasted_iota(jnp.int32, (n_rows, LANES), 0)
    lane = lax.broadcasted_iota(jnp.int32, (n_rows, LANES), 1)
    return jnp.where(row // SEL_BLOCK == lane, 1.0, 0.0).astype(MXU_DTYPE)


def _masked_softmax_keys(s, mask):
    s = jnp.where(mask, s, NEG_INF)
    m = jnp.max(s, axis=0, keepdims=True)
    p = jnp.where(mask, jnp.exp(s - m), 0.0)
    return p / jnp.maximum(jnp.sum(p, axis=0, keepdims=True), TINY)


def _nsa_prompt_kernel(q_ref, gate_ref, kcv_ref, kvs_ref, kvw_ref, out_ref,
                       kaug, vst, vwt, vct, lhs, m_sc, acc_sc, s_a, s_b, *, seq):
    qb = pl.program_id(1)
    rows = Q_BLOCK
    tk = KEY_TILE
    s0 = qb * rows
    n_sel = seq // SEL_BLOCK
    n_cmp = seq // CMP_BLOCK
    cols = N_HEADS * rows

    @pl.when(qb == 0)
    def _():
        kaug[:, 0:LANES] = kvs_ref[:, 0:LANES]
        kaug[:, LANES:2 * LANES] = _block_onehot(seq)
        vct[...] = kcv_ref[:, LANES:2 * LANES].T.astype(MXU_DTYPE)

        def tr_sel(i, c):
            r0 = pl.multiple_of(i * tk, tk)
            ones_row = jnp.where(lax.broadcasted_iota(jnp.int32, (SUM_ROWS, tk), 0) == 0, 1.0, 0.0)
            v_t = kvs_ref[pl.ds(r0, tk), LANES:2 * LANES].astype(F32).T
            vst[i] = jnp.concatenate([v_t, ones_row], axis=0).astype(MXU_DTYPE)
            return c

        def tr_win(i, c):
            r0 = pl.multiple_of(i * rows, rows)
            vwt[i] = kvw_ref[pl.ds(r0, rows), LANES:2 * LANES].astype(F32).T.astype(MXU_DTYPE)
            return c

        lax.fori_loop(0, seq // tk, tr_sel, 0)
        lax.fori_loop(0, seq // rows, tr_win, 0)

    _pad_heads(q_ref[...], lhs, rows)
    qpad = lhs[:, 0:LANES]
    lane = lax.broadcasted_iota(jnp.int32, (1, cols), 1)
    pos_q = s0 + lane % rows

    half = n_cmp // 2
    s_c = lax.dot_general(kcv_ref[:, 0:LANES].astype(MXU_DTYPE), qpad, _NT, preferred_element_type=F32)
    r_c = lax.broadcasted_iota(jnp.int32, (n_cmp, cols), 0)
    blk_c = jnp.where(r_c < half, 2 * r_c, 2 * (r_c - half) + 1)
    p_c = _masked_softmax_keys(s_c, (blk_c + 1) * CMP_BLOCK - 1 <= pos_q)
    o_c = jnp.dot(vct[...], p_c.astype(MXU_DTYPE), preferred_element_type=F32)

    imp = []
    for k in range(N_KV):
        acc = p_c[:, k * GROUP * rows:(k * GROUP + 1) * rows]
        for g in range(1, GROUP):
            acc = acc + p_c[:, (k * GROUP + g) * rows:(k * GROUP + g + 1) * rows]
        imp.append(acc[0:half] + acc[half:n_cmp])
    sel = _select_blocks_keys(jnp.concatenate(imp, axis=1), pos_q[:, 0:N_KV * rows], n_sel)
    for k in range(N_KV):
        sel_k = sel[:, k * rows:(k + 1) * rows]
        if n_sel < LANES:
            sel_k = jnp.concatenate([sel_k, jnp.zeros((LANES - n_sel, rows), F32)], axis=0)
        neg = jnp.where(sel_k.T > 0.0, 0.0, NEG_INF).astype(MXU_DTYPE)
        for g in range(GROUP):
            h = k * GROUP + g
            lhs[h * rows:(h + 1) * rows, LANES:2 * LANES] = neg

    m_sc[...] = jnp.full(m_sc.shape, NEG_INF, F32)
    acc_sc[...] = jnp.zeros(acc_sc.shape, F32)

    last_tile = seq // tk - 1

    def scores(s_ref, kt):
        k0 = pl.multiple_of(jnp.minimum(kt, last_tile) * tk, tk)
        s_ref[...] = lax.dot_general(kaug[pl.ds(k0, tk), :], lhs[...], _NT, preferred_element_type=F32)

    def update(s_ref, kt, causal):
        s = s_ref[...]
        if causal:
            s = jnp.where(kt * tk + lax.broadcasted_iota(jnp.int32, (tk, cols), 0) <= pos_q, s, NEG_INF)
        m_old = m_sc[...]
        m_new = jnp.maximum(m_old, jnp.max(s, axis=0, keepdims=True))
        alpha = jnp.exp(m_old - m_new)
        p = jnp.exp(s - m_new)
        m_sc[...] = m_new
        vt = vst[jnp.minimum(kt, last_tile)]
        acc_sc[...] = alpha * acc_sc[...] + jnp.dot(vt, p.astype(MXU_DTYPE), preferred_element_type=F32)

    n_pairs = (s0 // tk) // 2
    scores(s_a, 0)

    def pair_body(i, carry):
        scores(s_b, 2 * i + 1)
        update(s_a, 2 * i, False)
        scores(s_a, 2 * i + 2)
        update(s_b, 2 * i + 1, False)
        return carry

    lax.fori_loop(0, n_pairs, pair_body, 0)
    t0 = 2 * n_pairs
    scores(s_b, t0 + 1)
    update(s_a, t0, True)

    @pl.when((t0 + 1) * tk <= s0 + rows - 1)
    def _():
        update(s_b, t0 + 1, True)

    o_s = acc_sc[0:LANES, :] / jnp.maximum(acc_sc[LANES:LANES + 1, :], TINY)

    band = WINDOW + rows
    start = pl.multiple_of(jnp.maximum(s0 - WINDOW, 0), rows)
    s_w = lax.dot_general(kvw_ref[pl.ds(start, band), 0:LANES], qpad, _NT, preferred_element_type=F32)
    dlt = pos_q - (start + lax.broadcasted_iota(jnp.int32, (band, cols), 0))
    p_w = _masked_softmax_keys(s_w, (dlt >= 0) & (dlt <= WINDOW))
    t0 = start // rows
    vw = jnp.concatenate([vwt[t0 + j] for j in range(band // rows)], axis=1)
    o_w = jnp.dot(vw, p_w.astype(MXU_DTYPE), preferred_element_type=F32)

    g_t = gate_ref[...].T
    for m in range(N_HEADS // 2):
        parts = []
        for h in (2 * m, 2 * m + 1):
            k = h // GROUP
            rs = slice(k * HEAD_DIM, (k + 1) * HEAD_DIM)
            ls = slice(h * rows, (h + 1) * rows)
            parts.append(g_t[3 * h:3 * h + 1] * o_c[rs, ls] + g_t[3 * h + 1:3 * h + 2] * o_s[rs, ls]
                         + g_t[3 * h + 2:3 * h + 3] * o_w[rs, ls])
        out_ref[:, m * LANES:(m + 1) * LANES] = jnp.concatenate(parts, axis=0).T


def _nsa_prompt(q, gate, kcv, kvs_b, kvw_b, bsz, seq):
    n, d_b = bsz * seq, q.shape[1]
    kv_w = kvs_b.shape[1]
    nq = seq // Q_BLOCK
    n_cmp = seq // CMP_BLOCK
    rows = N_HEADS * Q_BLOCK
    return pl.pallas_call(
        functools.partial(_nsa_prompt_kernel, seq=seq),
        grid=(bsz, nq),
        in_specs=[pl.BlockSpec((Q_BLOCK, d_b), lambda b, i: (b * nq + i, 0)),
                  pl.BlockSpec((Q_BLOCK, LANES), lambda b, i: (b * nq + i, 0)),
                  pl.BlockSpec((n_cmp, kv_w), lambda b, i: (b, 0)),
                  pl.BlockSpec((seq, kv_w), lambda b, i: (b, 0)),
                  pl.BlockSpec((seq, kv_w), lambda b, i: (b, 0))],
        out_specs=pl.BlockSpec((Q_BLOCK, d_b), lambda b, i: (b * nq + i, 0)),
        out_shape=jax.ShapeDtypeStruct((n, d_b), F32),
        scratch_shapes=[pltpu.VMEM((seq, 2 * LANES), MXU_DTYPE),
                        pltpu.VMEM((seq // KEY_TILE, LANES + SUM_ROWS, KEY_TILE), MXU_DTYPE),
                        pltpu.VMEM((seq // Q_BLOCK, LANES, Q_BLOCK), MXU_DTYPE),
                        pltpu.VMEM((LANES, n_cmp), MXU_DTYPE),
                        pltpu.VMEM((rows, 2 * LANES), MXU_DTYPE),
                        pltpu.VMEM((1, rows), F32), pltpu.VMEM((LANES + SUM_ROWS, rows), F32),
                        pltpu.VMEM((KEY_TILE, rows), F32), pltpu.VMEM((KEY_TILE, rows), F32)],
        compiler_params=_cparams("arbitrary", "arbitrary"),
        name="nsa_prompt",
    )(q, gate, kcv, kvs_b, kvw_b)


def _nsa_sample_kernel(pt_ref, q_ref, gate_ref, ksn_ref, kwn_ref, wcmp_ref, win_ref, *rest,
                       n_pages, page, past, t_len):
    cmp_pages = rest[:n_pages]
    sel_pages = rest[n_pages:2 * n_pages]
    out_ref, nwin_ref, kaug, vsel, kcv, lhs = rest[2 * n_pages:]
    rows = 8
    n_cmp = past // CMP_BLOCK
    n_sel = pl.cdiv(past + t_len, SEL_BLOCK)
    per_page = page // CMP_BLOCK

    @pl.when(pl.program_id(0) == 0)
    def _():
        kaug[:, LANES:2 * LANES] = _block_onehot(past)

    q = jnp.concatenate([q_ref[0], jnp.zeros((rows - t_len, q_ref.shape[2]), F32)], axis=0)
    _pad_heads(q, lhs, rows)
    qpad = lhs[:, 0:LANES]
    tq = lax.broadcasted_iota(jnp.int32, (rows, 1), 0)
    pos_q = past + tq

    wrep = jnp.concatenate([wcmp_ref[...]] * per_page, axis=0)
    for p in range(n_pages):
        x = cmp_pages[p][...] * wrep
        for j in range(per_page):
            blk = p * per_page + j
            dst = (blk % 2) * (n_cmp // 2) + blk // 2
            kcv[dst:dst + 1, :] = jnp.sum(x[j * CMP_BLOCK:(j + 1) * CMP_BLOCK], axis=0, keepdims=True)
        kaug[p * page:(p + 1) * page, 0:LANES] = sel_pages[p][:, 0:LANES].astype(MXU_DTYPE)
        vsel[p * page:(p + 1) * page, :] = sel_pages[p][:, LANES:2 * LANES].astype(MXU_DTYPE)

    o_c, sels = _compressed_branch(qpad, kcv[...], past, rows, n_sel)
    _store_selection(lhs, sels, rows)

    def new_rows(ref):
        kv = jnp.concatenate([ref[0], jnp.zeros((LANES - t_len, ref.shape[2]), F32)], axis=0)
        return kv[:, 0:LANES].astype(MXU_DTYPE), kv[:, LANES:2 * LANES].astype(MXU_DTYPE)

    tk_new = lax.broadcasted_iota(jnp.int32, (rows, LANES), 1)
    m_new = (tk_new < t_len) & (tk_new <= tq)

    def joint_attention(s_past, mask_past, v_past, s_new, v_new):
        outs = []
        p_past, p_new = [], []
        for h in range(N_HEADS):
            sl = slice(h * rows, (h + 1) * rows)
            sp = s_past[sl] if mask_past is None else jnp.where(mask_past, s_past[sl], NEG_INF)
            sn = jnp.where(m_new, s_new[sl], NEG_INF)
            m = jnp.maximum(jnp.max(sp, axis=-1, keepdims=True), jnp.max(sn, axis=-1, keepdims=True))
            pp = jnp.exp(sp - m) if mask_past is None else jnp.where(mask_past, jnp.exp(sp - m), 0.0)
            pn = jnp.where(m_new, jnp.exp(sn - m), 0.0)
            den = jnp.maximum(jnp.sum(pp, axis=-1, keepdims=True) + jnp.sum(pn, axis=-1, keepdims=True), TINY)
            p_past.append((pp / den).astype(MXU_DTYPE))
            p_new.append((pn / den).astype(MXU_DTYPE))
        return (jnp.dot(jnp.concatenate(p_past, axis=0), v_past, preferred_element_type=F32)
                + jnp.dot(jnp.concatenate(p_new, axis=0), v_new, preferred_element_type=F32))

    kn, vn = new_rows(ksn_ref)
    s_past = lax.dot_general(lhs[...], kaug[...], _NT, preferred_element_type=F32)
    s_new = lax.dot_general(qpad, kn, _NT, preferred_element_type=F32)
    o_s = joint_attention(s_past, None, vsel[...], s_new, vn)

    n_buf = win_ref.shape[0]
    kwn, vwn = new_rows(kwn_ref)
    kwb = win_ref[:, 0:LANES].astype(MXU_DTYPE)
    vwb = win_ref[:, LANES:2 * LANES].astype(MXU_DTYPE)
    s_wb = lax.dot_general(qpad, kwb, _NT, preferred_element_type=F32)
    s_wn = lax.dot_general(qpad, kwn, _NT, preferred_element_type=F32)
    pos_w = past - n_buf + lax.broadcasted_iota(jnp.int32, (rows, n_buf), 1)
    dlt = pos_q - pos_w
    m_wb = (dlt >= 0) & (dlt <= WINDOW) & (pos_w >= 0)
    o_w = joint_attention(s_wb, m_wb, vwb, s_wn, vwn)

    for m, slab in enumerate(_gated_output(
            jnp.concatenate([gate_ref[0], jnp.zeros((rows - t_len, LANES), F32)], axis=0), o_c, o_s, o_w, rows)):
        out_ref[0, :, m * LANES:(m + 1) * LANES] = slab[0:t_len]

    nwin_ref[0:n_buf - t_len, :] = win_ref[t_len:n_buf, :]
    nwin_ref[n_buf - t_len:n_buf, :] = kwn_ref[0]


def _nsa_sample(page_table, q, gate, kvs_new, kvw_new, w_cmp, pool_c, pool_s, win, layer):
    dbs, t_len, d_b = q.shape
    n_pages = page_table.shape[1]
    page, kv_w = pool_c.shape[2], pool_c.shape[3]
    n_buf = win.shape[2]
    past = n_pages * page
    tok = lambda w: pl.BlockSpec((1, t_len, w), lambda b, pt: (b, 0, 0))

    def page_spec(p):
        return pl.BlockSpec((None, None, page, kv_w), lambda b, pt, p=p: (layer, pt[b * n_pages + p], 0, 0))

    grid_spec = pltpu.PrefetchScalarGridSpec(
        num_scalar_prefetch=1,
        grid=(dbs,),
        in_specs=[tok(d_b), tok(LANES), tok(kv_w), tok(kv_w),
                  pl.BlockSpec(w_cmp.shape, lambda b, pt: (0, 0)),
                  pl.BlockSpec((None, None, n_buf, kv_w), lambda b, pt: (layer, b, 0, 0))]
                 + [page_spec(p) for p in range(n_pages)] * 2,
        out_specs=[tok(d_b), pl.BlockSpec((None, n_buf, kv_w), lambda b, pt: (b, 0, 0))],
        scratch_shapes=[pltpu.VMEM((past, 2 * LANES), MXU_DTYPE), pltpu.VMEM((past, LANES), MXU_DTYPE),
                        pltpu.VMEM((past // CMP_BLOCK, kv_w), F32), pltpu.VMEM((N_HEADS * 8, 2 * LANES), MXU_DTYPE)],
    )
    return pl.pallas_call(
        functools.partial(_nsa_sample_kernel, n_pages=n_pages, page=page, past=past, t_len=t_len),
        grid_spec=grid_spec,
        out_shape=[jax.ShapeDtypeStruct((dbs, t_len, d_b), F32), jax.ShapeDtypeStruct((dbs, n_buf, kv_w), F32)],
        compiler_params=_cparams("arbitrary"),
        name="nsa_sample",
    )(page_table.reshape(-1), q, gate, kvs_new, kvw_new, w_cmp, win, *([pool_c] * n_pages), *([pool_s] * n_pages))


def _out_proj_router_kernel(*refs, n_x, n_ptiles):
    x_refs = refs[:n_x]
    (oa_ref, ob_ref, oc_ref, oas_ref, obs_ref, ocs_ref, wo_ref, g2_ref, wr_ref, br_ref,
     xn_ref, h2_ref, ri_ref, rw_ref, cnt_ref, run_ref) = refs[n_x:]
    tm = oa_ref.shape[0]

    @pl.when(pl.program_id(0) == 0)
    def _():
        run_ref[...] = jnp.zeros(run_ref.shape, F32)

    is_sample = pl.program_id(0) >= n_ptiles
    mix = jnp.concatenate([jnp.where(is_sample, oas_ref[...], oa_ref[...]),
                           jnp.where(is_sample, obs_ref[...], ob_ref[...]),
                           jnp.where(is_sample, ocs_ref[...], oc_ref[...])], axis=1)
    xn = _token_tile(x_refs, n_ptiles) + _split_dot(_split_act(mix, wo_ref.shape[0]), wo_ref)
    xn_ref[...] = xn
    h2 = _rmsnorm(xn, g2_ref[...])
    _store_token_tiles(h2_ref, h2)
    logits = _split_dot(_split_act(h2, wr_ref.shape[0]), wr_ref) + br_ref[...]

    lane = lax.broadcasted_iota(jnp.int32, (tm, LANES), 1)
    lanef = lane.astype(F32)

    def softmax_over(mask):
        m = jnp.max(jnp.where(mask, logits, -jnp.inf), axis=-1, keepdims=True)
        e = jnp.where(mask, jnp.exp(logits - m), 0.0)
        return e / jnp.sum(e, axis=-1, keepdims=True)

    def first_max(p, mask):
        pm = jnp.max(jnp.where(mask, p, -1.0), axis=-1, keepdims=True)
        idx = jnp.min(jnp.where(mask & (p == pm), lanef, 1e9), axis=-1, keepdims=True)
        return pm, idx

    is_g = lane < N_GROUPS
    g_wt, g_sel = first_max(softmax_over(is_g), is_g)
    lo = N_GROUPS + EXP_PER_GROUP * g_sel
    in_e = (lanef >= lo) & (lanef < lo + EXP_PER_GROUP)
    p_e = softmax_over(in_e)
    p0, i0 = first_max(p_e, in_e)
    p1, i1 = first_max(p_e, in_e & (lanef != i0))
    den = p0 + p1
    e0 = i0 - N_GROUPS
    e1 = i1 - N_GROUPS

    hit0 = lanef == e0
    hit1 = lanef == e1
    onehot = jnp.where(hit0 | hit1, 1.0, 0.0)
    r_i = lax.broadcasted_iota(jnp.int32, (tm, tm), 0)
    c_i = lax.broadcasted_iota(jnp.int32, (tm, tm), 1)
    ltri = jnp.where(c_i < r_i, 1.0, 0.0).astype(jnp.bfloat16)
    before = jnp.dot(ltri, onehot.astype(jnp.bfloat16), preferred_element_type=F32) + run_ref[0:1, :]
    r0 = jnp.sum(jnp.where(hit0, before, 0.0), axis=-1, keepdims=True)
    r1 = jnp.sum(jnp.where(hit1, before, 0.0), axis=-1, keepdims=True)
    run_ref[...] = run_ref[...] + jnp.sum(onehot, axis=0, keepdims=True)
    cnt_ref[...] = run_ref[...]

    ri = jnp.where(lane == 0, e0, jnp.where(lane == 1, e1, jnp.where(lane == 2, r0, jnp.where(lane == 3, r1, 0.0))))
    ri_ref[...] = ri.T[0:8].astype(jnp.int32)
    rw_ref[...] = jnp.where(lane == 0, g_wt * p0 / den, jnp.where(lane == 1, g_wt * p1 / den, 0.0))


def _out_proj_router(x, mix_p, mix_s, w_out, g2, w_route, b_route):
    tm = TOKEN_TILE
    n_ptiles = mix_p[0].shape[0] // tm
    x_ops, x_specs = _token_specs(x, tm, n_ptiles)
    n, d = sum(a.shape[0] for a in x_ops), x_ops[0].shape[1]
    assert all(a.shape[0] == tm for a in mix_s) and n == (n_ptiles + 1) * tm
    row = lambda w: pl.BlockSpec((tm, w), lambda i: (i, 0))
    prow = lambda a: pl.BlockSpec((tm, a.shape[1]), lambda i: (jnp.minimum(i, n_ptiles - 1), 0))
    full = lambda a: pl.BlockSpec(a.shape, lambda i: (0,) * a.ndim)
    return pl.pallas_call(
        functools.partial(_out_proj_router_kernel, n_x=len(x_ops), n_ptiles=n_ptiles),
        grid=(n // tm,),
        in_specs=x_specs + [prow(a) for a in mix_p] + [full(a) for a in mix_s]
                 + [full(w_out), full(g2), full(w_route), full(b_route)],
        out_specs=[row(d), pl.BlockSpec((tm * SUBLANES, LANES), lambda i: (i, 0)),
                   pl.BlockSpec((8, tm), lambda i: (0, i)), row(LANES), pl.BlockSpec((8, LANES), lambda i: (0, 0))],
        out_shape=[jax.ShapeDtypeStruct((n, d), F32), jax.ShapeDtypeStruct((n * SUBLANES, LANES), F32),
                   jax.ShapeDtypeStruct((8, n), jnp.int32), jax.ShapeDtypeStruct((n, LANES), F32),
                   jax.ShapeDtypeStruct((8, LANES), F32)],
        scratch_shapes=[pltpu.VMEM((8, LANES), F32)],
        compiler_params=_cparams("arbitrary"),
        name="out_proj_router",
    )(*x_ops, *mix_p, *mix_s, w_out, g2, w_route, b_route)


def _store_token_tiles(ref, x):
    t = x.shape[0]
    for c in range(SUBLANES):
        ref[pl.ds(c, t, stride=SUBLANES), :] = x[:, c * LANES:(c + 1) * LANES]


def _load_token_tiles(ref, first_row, t):
    return jnp.concatenate([ref[pl.ds(first_row + c, t, stride=SUBLANES), :] for c in range(SUBLANES)], axis=1)


def _row_copy(src_hbm, row, dst, slot, r, sem):
    return pltpu.make_async_copy(src_hbm.at[pl.ds(pl.multiple_of(row * SUBLANES, SUBLANES), SUBLANES)],
                                 dst.at[slot, pl.ds(pl.multiple_of(r * SUBLANES, SUBLANES), SUBLANES)],
                                 sem.at[slot])


def _expert_kernel(te_ref, nv_ref, src_ref, h_hbm, wg_ref, wu_ref, wd_ref, ys_ref, xbuf, sem):
    t = pl.program_id(0)
    nv = nv_ref[0]
    te = xbuf.shape[1] // SUBLANES

    def issue(tile, slot):
        def body(i, carry):
            for j in range(2):
                r = 2 * i + j
                _row_copy(h_hbm, src_ref[tile * te + r], xbuf, slot, r, sem).start(priority=j)
            return carry
        lax.fori_loop(0, te // 2, body, 0, unroll=4)

    def wait_all(slot):
        pltpu.make_async_copy(h_hbm.at[pl.ds(0, te * SUBLANES)], xbuf.at[slot], sem.at[slot]).wait()

    @pl.when((t == 0) & (nv > 0))
    def _():
        issue(0, 0)

    @pl.when(t + 1 < nv)
    def _():
        issue(t + 1, (t + 1) % 2)

    @pl.when(t < nv)
    def _():
        slot = t % 2
        wait_all(slot)
        x = _load_token_tiles(xbuf.at[slot], 0, te).astype(MXU_DTYPE)
        hg = jnp.dot(x, wg_ref[...].astype(MXU_DTYPE), preferred_element_type=F32)
        hu = jnp.dot(x, wu_ref[...].astype(MXU_DTYPE), preferred_element_type=F32)
        act = (jax.nn.silu(hg) * hu).astype(MXU_DTYPE)
        _store_token_tiles(ys_ref, jnp.dot(act, wd_ref[...].astype(MXU_DTYPE), preferred_element_type=F32))

    @pl.when(t >= nv)
    def _():
        ys_ref[...] = jnp.zeros(ys_ref.shape, F32)


def _experts(tile_expert, n_valid, src_tok, h2, w_gate, w_up, w_down, layer):
    n_tiles = tile_expert.shape[0]
    te = EXPERT_TILE
    d = SUBLANES * LANES
    d_e = w_gate.shape[3]
    assert w_gate.shape[2] == d and h2.shape[1] == LANES
    wspec = lambda r, c: pl.BlockSpec((None, None, r, c), lambda t, te_r, nv_r, src_r: (layer, te_r[t], 0, 0))
    grid_spec = pltpu.PrefetchScalarGridSpec(
        num_scalar_prefetch=3,
        grid=(n_tiles,),
        in_specs=[pl.BlockSpec(memory_space=pl.ANY), wspec(d, d_e), wspec(d, d_e), wspec(d_e, d)],
        out_specs=pl.BlockSpec((te * SUBLANES, LANES), lambda t, te_r, nv_r, src_r: (t, 0)),
        scratch_shapes=[pltpu.VMEM((2, te * SUBLANES, LANES), F32), pltpu.SemaphoreType.DMA((2,))],
    )
    return pl.pallas_call(
        _expert_kernel,
        grid_spec=grid_spec,
        out_shape=jax.ShapeDtypeStruct((n_tiles * te * SUBLANES, LANES), F32),
        compiler_params=_cparams("arbitrary"),
        name="moe_experts",
    )(tile_expert, n_valid, src_tok, h2, w_gate, w_up, w_down)


def _combine_kernel(d0_ref, d1_ref, x_ref, rw_ref, gf_ref, ys_hbm, *rest, final):
    if final:
        xo_ref, y_ref, buf, sem = rest
    else:
        xo_ref, buf, sem = rest
    t = pl.program_id(0)
    nt = pl.num_programs(0)
    tm = x_ref.shape[0]

    def issue(tile, slot):
        def body(r, carry):
            _row_copy(ys_hbm, d0_ref[tile * tm + r], buf, slot, r, sem).start(priority=0)
            _row_copy(ys_hbm, d1_ref[tile * tm + r], buf, slot, tm + r, sem).start(priority=1)
            return carry
        lax.fori_loop(0, tm, body, 0, unroll=8)

    def wait_all(slot):
        pltpu.make_async_copy(ys_hbm.at[pl.ds(0, 2 * tm * SUBLANES)], buf.at[slot], sem.at[slot]).wait()

    @pl.when(t == 0)
    def _():
        issue(0, 0)

    @pl.when(t + 1 < nt)
    def _():
        issue(t + 1, (t + 1) % 2)

    slot = t % 2
    wait_all(slot)
    rw = rw_ref[...]
    xo = (x_ref[...] + rw[:, 0:1] * _load_token_tiles(buf.at[slot], 0, tm)
          + rw[:, 1:2] * _load_token_tiles(buf.at[slot], tm * SUBLANES, tm))
    xo_ref[...] = xo
    if final:
        y_ref[...] = _rmsnorm(xo, gf_ref[...])


def _combine(d0, d1, x, rw, gf, ys, final):
    n, d = x.shape
    tm = COMBINE_TILE
    row = lambda w: pl.BlockSpec((tm, w), lambda t, a, b: (t, 0))
    n_out = 2 if final else 1
    grid_spec = pltpu.PrefetchScalarGridSpec(
        num_scalar_prefetch=2,
        grid=(n // tm,),
        in_specs=[row(d), row(LANES), pl.BlockSpec((1, d), lambda t, a, b: (0, 0)), pl.BlockSpec(memory_space=pl.ANY)],
        out_specs=[row(d)] * n_out,
        scratch_shapes=[pltpu.VMEM((2, 2 * tm * SUBLANES, LANES), F32), pltpu.SemaphoreType.DMA((2,))],
    )
    return pl.pallas_call(
        functools.partial(_combine_kernel, final=final),
        grid_spec=grid_spec,
        out_shape=[jax.ShapeDtypeStruct((n, d), F32)] * n_out,
        compiler_params=_cparams("arbitrary"),
        name="moe_combine_final" if final else "moe_combine",
    )(d0, d1, x, rw, gf, ys)


def _route_plan(ri, cnt, n_tiles):
    te = EXPERT_TILE
    n = ri.shape[1]
    counts = cnt[0, :N_EXPERTS].astype(jnp.int32)
    padded = ((counts + te - 1) // te) * te
    ends = jnp.cumsum(padded)
    offs = ends - padded
    d0 = offs[ri[0]] + ri[2]
    d1 = offs[ri[1]] + ri[3]
    tok = jnp.arange(n, dtype=jnp.int32)
    src = jnp.zeros((n_tiles * te,), jnp.int32).at[jnp.concatenate([d0, d1])].set(jnp.concatenate([tok, tok]))
    tile_start = jnp.arange(n_tiles, dtype=jnp.int32) * te
    tile_expert = jnp.minimum(jnp.sum((ends[None, :] <= tile_start[:, None]).astype(jnp.int32), axis=1),
                              N_EXPERTS - 1)
    n_valid = (ends[-1:] // te).astype(jnp.int32)
    return d0, d1, src, tile_expert, n_valid


def _rope_tables(pos):
    half = HEAD_DIM // 2
    inv = ROPE_THETA ** (-jnp.arange(half, dtype=F32) / half)
    ang = pos.astype(F32)[:, None] * inv[None, :]
    cos = jnp.cos(ang)
    sin = jnp.sin(ang)
    cos_t = jnp.concatenate([cos, cos] * (LANES // HEAD_DIM), axis=1)
    sin_t = jnp.concatenate([-sin, sin] * (LANES // HEAD_DIM), axis=1)
    return cos_t, sin_t


def _block_diag(w):
    nb, bw, _ = w.shape
    out = jnp.zeros((nb * bw, nb * bw), w.dtype)
    for i in range(nb):
        out = out.at[i * bw:(i + 1) * bw, i * bw:(i + 1) * bw].set(w[i])
    return out


def kernel(x_prompt, x_sample, cache_kv_cmp, cache_kv_sel, cache_kv_win, state_lru_h, state_lru_conv, state_sconv, page_table, norm1_g, w_in, lru_conv_w, lru_conv_b, lru_wa, lru_ba, lru_wx, lru_bx, lru_lambda, nsa_cmp_wk, nsa_cmp_wv, sc_conv_w, w_out, norm2_g, router_group_w, router_group_b, router_exp_w, router_exp_b, exp_w_gate, exp_w_up, exp_w_down, norm_f_g):
    bsz, seq, d = x_prompt.shape
    dbs, t_len, _ = x_sample.shape
    depth = w_in.shape[0]
    d_a = lru_conv_w.shape[2]
    d_c = sc_conv_w.shape[2]
    kv_w = 2 * N_KV * HEAD_DIM
    d_b = N_HEADS * HEAD_DIM
    n_gate = 3 * N_HEADS
    n_p = bsz * seq
    n_s = dbs * t_len
    n = n_p + n_s
    assert n_s == TOKEN_TILE and n_p % TOKEN_TILE == 0 and seq % TOKEN_TILE == 0
    page = cache_kv_cmp.shape[2]
    past = page_table.shape[1] * page
    n_buf = cache_kv_win.shape[2]

    pos = jnp.concatenate([jnp.arange(seq, dtype=jnp.int32),
                           jnp.tile(past + jnp.arange(t_len, dtype=jnp.int32), dbs)])
    cos_t, sin_t = _rope_tables(pos)

    pool_c = cache_kv_cmp.reshape(depth, -1, page, kv_w)
    pool_s = cache_kv_sel.reshape(depth, -1, page, kv_w)
    win = cache_kv_win.reshape(depth, dbs, n_buf, kv_w)

    g_off = 2 * d_a + d_b + 3 * kv_w
    x = (x_prompt.reshape(n_p, d), x_sample.reshape(n_s, d))
    h0_p = jnp.zeros((bsz, 1, d_a), F32)
    lb0_p = jnp.zeros((bsz, lru_conv_w.shape[1] - 1, d_a), F32)
    sb0_p = jnp.zeros((bsz, sc_conv_w.shape[1] - 1, d_c), F32)

    n_tiles = (2 * n) // EXPERT_TILE + N_EXPERTS
    proj_parts = lambda l: 2 if l < depth - 1 else 1
    states_p, states_s = [], []
    y = None
    for l in range(depth):
        w_l = w_in[l]
        w_r = _split_weight(jnp.concatenate([w_l[:, :g_off], w_l[:, g_off + n_gate:], w_l[:, g_off:g_off + n_gate],
                                             jnp.zeros((d, LANES - n_gate), F32)], axis=1), proj_parts(l))
        lru_in, sc_in, gate, q, kvc, kvs, kvw, kvs_b, kvw_b = _in_proj(
            x, norm1_g[l][None], w_r, cos_t, sin_t, n_p, seq, d_a=d_a, d_b=d_b, kv_w=kv_w, d_c=d_c)

        seq_w = (lru_conv_w[l], lru_conv_b[l][None], _block_diag(lru_wa[l]).astype(MXU_DTYPE), lru_ba[l][None],
                 _block_diag(lru_wx[l]).astype(MXU_DTYPE), lru_bx[l][None], lru_lambda[l][None], sc_conv_w[l])
        out_a, out_c, hn_p, lbn_p, sbn_p = _seq_mix_prompt(lru_in, sc_in, h0_p, lb0_p, sb0_p, seq_w, bsz, seq,
                                                           d_a=d_a, d_c=d_c)
        tmaj = lambda a: jnp.swapaxes(a.reshape(dbs, t_len, -1), 0, 1)
        oa_s, oc_s, hn_s, lbn_s, sbn_s = _seq_mix_sample(
            tmaj(lru_in[n_p:]), tmaj(sc_in[n_p:]), state_lru_h[l], jnp.swapaxes(state_lru_conv[l], 0, 1),
            jnp.swapaxes(state_sconv[l], 0, 1), seq_w, d_a=d_a, d_c=d_c)
        oa_s = jnp.swapaxes(oa_s, 0, 1).reshape(n_s, d_a)
        oc_s = jnp.swapaxes(oc_s, 0, 1).reshape(n_s, d_c)

        w_cmp = jnp.concatenate([jnp.broadcast_to(nsa_cmp_wk[l][:, None], (CMP_BLOCK, kv_w // 2)),
                                 jnp.broadcast_to(nsa_cmp_wv[l][:, None], (CMP_BLOCK, kv_w // 2))], axis=1)
        kcv = _compress_prompt(kvc, w_cmp, bsz, seq)
        out_b = _nsa_prompt(q, gate, kcv, kvs_b, kvw_b, bsz, seq)
        s3 = lambda a: a[n_p:].reshape(dbs, t_len, -1)
        ob_s, nwin_s = _nsa_sample(page_table, s3(q), s3(gate), s3(kvs), s3(kvw), w_cmp, pool_c, pool_s, win, l)
        ob_s = ob_s.reshape(n_s, d_b)

        w_route = _split_weight(jnp.concatenate([router_group_w[l], router_exp_w[l],
                                                 jnp.zeros((d, LANES - N_GROUPS - N_EXPERTS), F32)], axis=1), 2)
        b_route = jnp.concatenate([router_group_b[l], router_exp_b[l],
                                   jnp.zeros((LANES - N_GROUPS - N_EXPERTS,), F32)])[None]
        xn, h2, ri, rw, cnt = _out_proj_router(x, (out_a, out_b, out_c), (oa_s, ob_s, oc_s),
                                               _split_weight(w_out[l], proj_parts(l)), norm2_g[l][None],
                                               w_route, b_route)
        d0, d1, src, tile_expert, n_valid = _route_plan(ri, cnt, n_tiles)
        ys = _experts(tile_expert, n_valid, src, h2, exp_w_gate, exp_w_up, exp_w_down, l)
        final = l == depth - 1
        outs = _combine(d0, d1, xn, rw, norm_f_g[None], ys, final)
        x = outs[0]
        if final:
            y = outs[1]

        kv6 = lambda a, lead: a.reshape(lead + (2, N_KV, HEAD_DIM))
        states_p.append((kv6(kvc[:n_p], (bsz, seq)), kv6(kvs[:n_p], (bsz, seq)),
                         kv6(kvw[:n_p].reshape(bsz, seq, kv_w)[:, seq - min(WINDOW, seq):], (bsz, min(WINDOW, seq))),
                         hn_p[:, 0], lbn_p, sbn_p))
        states_s.append((kv6(kvc[n_p:], (dbs, t_len)), kv6(kvs[n_p:], (dbs, t_len)), kv6(nwin_s, (dbs, n_buf)),
                         hn_s, jnp.swapaxes(lbn_s, 0, 1), jnp.swapaxes(sbn_s, 0, 1)))

    stack = lambda sts, i: jnp.stack([s[i] for s in sts])
    res = [y[:n_p].reshape(bsz, seq, d), y[n_p:].reshape(dbs, t_len, d)]
    for i in range(6):
        res += [stack(states_p, i), stack(states_s, i)]
    return tuple(res)
```

```python
import functools

import jax
import jax.numpy as jnp
from jax import lax
from jax.experimental import pallas as pl
from jax.experimental.pallas import tpu as pltpu

F32 = jnp.float32
MXU_DTYPE = jnp.bfloat16

HEAD_DIM = 64
N_HEADS = 8
N_KV = 2
GROUP = N_HEADS // N_KV
CMP_BLOCK = 32
SEL_BLOCK = 64
TOP_N = 16
WINDOW = 512
Q_BLOCK = 128
ROPE_THETA = 10000.0
LRU_C = 8.0
N_GROUPS = 4
EXP_PER_GROUP = 8
N_EXPERTS = N_GROUPS * EXP_PER_GROUP
RMS_EPS = 1e-6
NEG_INF = -1e30
TINY = 1e-30
FORCE = 1e6

LANES = 128
SUBLANES = 8
SUM_ROWS = 2 * SUBLANES
VMEM_LIMIT = 56 * 2 ** 20
TOKEN_TILE = 512
TIME_CHUNK = 256
KEY_TILE = 512
EXPERT_TILE = 512
COMBINE_TILE = 256

_NT = (((1,), (1,)), ((), ()))


def _cparams(*sem):
    return pltpu.CompilerParams(dimension_semantics=sem, vmem_limit_bytes=VMEM_LIMIT)


def _split_weight(w, parts):
    if parts == 1:
        return w.astype(MXU_DTYPE)[None]
    hi = lax.bitcast_convert_type(lax.bitcast_convert_type(w, jnp.uint32) & jnp.uint32(0xFFFF0000), F32)
    return jnp.stack([hi.astype(MXU_DTYPE), (w - hi).astype(MXU_DTYPE)])


def _split_act(a, parts):
    if parts == 1:
        return (a.astype(MXU_DTYPE),)
    hi = lax.bitcast_convert_type(lax.bitcast_convert_type(a, jnp.uint32) & jnp.uint32(0xFFFF0000), F32)
    return hi.astype(MXU_DTYPE), (a - hi).astype(MXU_DTYPE)


def _split_dot(a_parts, w_ref, cols=slice(None)):
    dot = lambda a, p: jnp.dot(a, w_ref[p, :, cols], preferred_element_type=F32)
    if len(a_parts) == 1:
        return dot(a_parts[0], 0)
    return (dot(a_parts[1], 1) + (dot(a_parts[0], 1) + dot(a_parts[1], 0))) + dot(a_parts[0], 0)


def _rmsnorm(x, g):
    return x * lax.rsqrt(jnp.mean(x * x, axis=-1, keepdims=True) + RMS_EPS) * g


def _swap_halves(x):
    w = x.shape[-1]
    lane = lax.broadcasted_iota(jnp.int32, x.shape, x.ndim - 1)
    first = (lane % HEAD_DIM) < HEAD_DIM // 2
    return jnp.where(first, pltpu.roll(x, w - HEAD_DIM // 2, axis=1), pltpu.roll(x, HEAD_DIM // 2, axis=1))


def _masked_softmax(s, mask):
    s = jnp.where(mask, s, NEG_INF)
    m = jnp.max(s, axis=-1, keepdims=True)
    p = jnp.where(mask, jnp.exp(s - m), 0.0)
    return p / jnp.maximum(jnp.sum(p, axis=-1, keepdims=True), TINY)


def _select_blocks_keys(imp, pos_q, n_sel):
    blk = lax.broadcasted_iota(jnp.int32, imp.shape, 0)
    cur = pos_q // SEL_BLOCK
    forced = (blk == 0) | (blk == cur) | (blk == cur - 1)
    w = jnp.where(forced, FORCE, imp)
    w = jnp.where(blk <= cur, w, NEG_INF)
    w = jnp.where(blk < n_sel, w, -jnp.inf)
    blkf = blk.astype(F32)

    def body(_, carry):
        w, sel = carry
        m = jnp.max(w, axis=0, keepdims=True)
        first = jnp.min(jnp.where(w == m, blkf, 1e9), axis=0, keepdims=True)
        pick = blkf == first
        return jnp.where(pick, -jnp.inf, w), jnp.where(pick, 1.0, sel)

    _, sel = lax.fori_loop(0, min(TOP_N, n_sel), body, (w, jnp.zeros_like(w)))
    return sel


def _token_specs(x, tm, n_ptiles):
    if not isinstance(x, tuple):
        return [x], [pl.BlockSpec((tm, x.shape[1]), lambda i: (i, 0))]
    x_p, x_s = x
    assert x_p.shape[0] == n_ptiles * tm and x_s.shape[0] == tm
    return [x_p, x_s], [pl.BlockSpec((tm, x_p.shape[1]), lambda i: (jnp.minimum(i, n_ptiles - 1), 0)),
                        pl.BlockSpec((tm, x_s.shape[1]), lambda i: (0, 0))]


def _token_tile(x_refs, n_ptiles):
    if len(x_refs) == 1:
        return x_refs[0][...]
    return jnp.where(pl.program_id(0) >= n_ptiles, x_refs[1][...], x_refs[0][...])


def _in_proj_kernel(*refs, d_a, d_b, kv_w, d_c, n_x, n_ptiles):
    x_refs = refs[:n_x]
    (g_ref, w_ref, cos_ref, sin_ref,
     lru_ref, sc_ref, gate_ref, q_ref, kvc_ref, kvs_ref, kvw_ref, kvsb_ref, kvwb_ref) = refs[n_x:]
    h_parts = _split_act(_rmsnorm(_token_tile(x_refs, n_ptiles), g_ref[...]), w_ref.shape[0])

    def mm(a, b):
        return _split_dot(h_parts, w_ref, slice(a, b))

    cos = cos_ref[...]
    sin = sin_ref[...]
    off = 2 * d_a
    lru_ref[...] = mm(0, off)
    q = mm(off, off + d_b)
    reps = d_b // LANES
    cos_q = jnp.concatenate([cos] * reps, axis=1)
    sin_q = jnp.concatenate([sin] * reps, axis=1)
    q_ref[...] = (q * cos_q + _swap_halves(q) * sin_q) * (HEAD_DIM ** -0.5)
    off += d_b
    for ref, bref in ((kvc_ref, None), (kvs_ref, kvsb_ref), (kvw_ref, kvwb_ref)):
        kv = mm(off, off + kv_w)
        k = kv[:, :kv_w // 2]
        kv = jnp.concatenate([k * cos + _swap_halves(k) * sin, kv[:, kv_w // 2:]], axis=1)
        ref[...] = kv
        if bref is not None:
            bref[...] = kv.astype(MXU_DTYPE)
        off += kv_w
    sc_ref[...] = mm(off, off + 3 * d_c)
    off += 3 * d_c
    gate_ref[...] = jax.nn.sigmoid(mm(off, off + LANES))


def _in_proj(x, g1, w_r, cos_t, sin_t, n_prompt, seq, *, d_a, d_b, kv_w, d_c):
    tm = TOKEN_TILE
    n_ptiles = n_prompt // tm
    n_stiles = seq // tm
    x_ops, x_specs = _token_specs(x, tm, n_ptiles)
    n, d = sum(a.shape[0] for a in x_ops), x_ops[0].shape[1]

    def tab_map(i):
        return (jnp.where(i < n_ptiles, i % n_stiles, n_stiles), 0)

    row = lambda w: pl.BlockSpec((tm, w), lambda i: (i, 0))
    widths = (2 * d_a, 3 * d_c, LANES, d_b, kv_w, kv_w, kv_w, kv_w, kv_w)
    dtypes = (F32, F32, F32, F32, F32, F32, F32, MXU_DTYPE, MXU_DTYPE)
    return pl.pallas_call(
        functools.partial(_in_proj_kernel, d_a=d_a, d_b=d_b, kv_w=kv_w, d_c=d_c, n_x=len(x_ops), n_ptiles=n_ptiles),
        grid=(n // tm,),
        in_specs=x_specs + [
                  pl.BlockSpec((1, d), lambda i: (0, 0)),
                  pl.BlockSpec(w_r.shape, lambda i: (0, 0, 0)),
                  pl.BlockSpec((tm, LANES), tab_map),
                  pl.BlockSpec((tm, LANES), tab_map)],
        out_specs=[row(w) for w in widths],
        out_shape=[jax.ShapeDtypeStruct((n, w), dt) for w, dt in zip(widths, dtypes)],
        compiler_params=_cparams("parallel"),
        name="in_proj",
    )(*x_ops, g1, w_r, cos_t, sin_t)


def _scan_rows(a, u):
    t = a.shape[0]
    row = lax.broadcasted_iota(jnp.int32, a.shape, 0)
    d = 1
    while d < t:
        keep = row >= d
        a_sh = jnp.where(keep, pltpu.roll(a, d, axis=0), 1.0)
        u_sh = jnp.where(keep, pltpu.roll(u, d, axis=0), 0.0)
        u = a * u_sh + u
        a = a * a_sh
        d *= 2
    return a, u


def _lru_gates(xc, wa_ref, ba_ref, wx_ref, bx_ref, lam_ref):
    xb = xc.astype(MXU_DTYPE)
    r = jax.nn.sigmoid(jnp.dot(xb, wa_ref[...], preferred_element_type=F32) + ba_ref[...])
    i = jax.nn.sigmoid(jnp.dot(xb, wx_ref[...], preferred_element_type=F32) + bx_ref[...])
    log_a = -LRU_C * r * jax.nn.softplus(-lam_ref[...])
    a = jnp.exp(log_a)
    th = jnp.tanh(log_a)
    u = jnp.sqrt(-2.0 * th / (1.0 - th)) * (i * xc)
    return a, u


def _seq_mix_kernel(lru_ref, sc_ref, h0_ref, lb0_ref, sb0_ref, cw_ref, cb_ref, wa_ref, ba_ref, wx_ref, bx_ref,
                    lam_ref, scw_ref, oa_ref, oc_ref, hn_ref, lbn_ref, sbn_ref, xpad, vpad, hcar, *, d_a, d_c):
    c = pl.program_id(1)
    tc = lru_ref.shape[0]
    nlb = lb0_ref.shape[1]
    nsb = sb0_ref.shape[1]

    @pl.when(c == 0)
    def _():
        xpad[0:8, :] = jnp.zeros((8, d_a), F32)
        xpad[8 - nlb:8, :] = lb0_ref[0]
        vpad[0:8, :] = jnp.zeros((8, d_c), F32)
        vpad[8 - nsb:8, :] = sb0_ref[0]
        hcar[...] = h0_ref[0]

    xa = lru_ref[:, 0:d_a]
    ga = lru_ref[:, d_a:2 * d_a]
    xpad[8:8 + tc, :] = xa
    xc = cw_ref[0:1, :] * xpad[8 - nlb:8 - nlb + tc, :]
    for j in range(1, nlb):
        xc = xc + cw_ref[j:j + 1, :] * xpad[8 - nlb + j:8 - nlb + j + tc, :]
    xc = xc + cw_ref[nlb:nlb + 1, :] * xa + cb_ref[...]
    a, u = _lru_gates(xc, wa_ref, ba_ref, wx_ref, bx_ref, lam_ref)
    a_cum, hs = _scan_rows(a, u)
    hs = hs + a_cum * hcar[...]
    hcar[...] = hs[tc - 1:tc, :]
    oa_ref[...] = hs * jax.nn.gelu(ga)

    v = sc_ref[:, 0:d_c] * sc_ref[:, 2 * d_c:3 * d_c]
    vpad[8:8 + tc, :] = v
    uc = scw_ref[0:1, :] * vpad[8 - nsb:8 - nsb + tc, :]
    for j in range(1, nsb):
        uc = uc + scw_ref[j:j + 1, :] * vpad[8 - nsb + j:8 - nsb + j + tc, :]
    uc = uc + scw_ref[nsb:nsb + 1, :] * v
    oc_ref[...] = sc_ref[:, d_c:2 * d_c] * uc

    hn_ref[0] = hs[tc - 1:tc, :]
    lbn_ref[0] = xpad[8 + tc - nlb:8 + tc, :]
    sbn_ref[0] = vpad[8 + tc - nsb:8 + tc, :]
    xpad[0:8, :] = xpad[tc:tc + 8, :]
    vpad[0:8, :] = vpad[tc:tc + 8, :]


def _seq_mix_prompt(lru_in, sc_in, h0, lb0, sb0, wts, bsz, seq, *, d_a, d_c):
    n = bsz * seq
    tc = TIME_CHUNK
    nch = seq // tc
    row = lambda w: pl.BlockSpec((tc, w), lambda b, c: (b * nch + c, 0))
    full = lambda a: pl.BlockSpec(a.shape, lambda b, c: (0,) * a.ndim)
    state = lambda r, w: pl.BlockSpec((1, r, w), lambda b, c: (b, 0, 0))
    nlb, nsb = lb0.shape[1], sb0.shape[1]
    return pl.pallas_call(
        functools.partial(_seq_mix_kernel, d_a=d_a, d_c=d_c),
        grid=(bsz, nch),
        in_specs=[row(2 * d_a), row(3 * d_c), state(1, d_a), state(nlb, d_a), state(nsb, d_c)]
                 + [full(w) for w in wts],
        out_specs=[row(d_a), row(d_c), state(1, d_a), state(nlb, d_a), state(nsb, d_c)],
        out_shape=[jax.ShapeDtypeStruct((n, d_a), F32), jax.ShapeDtypeStruct((n, d_c), F32),
                   jax.ShapeDtypeStruct((bsz, 1, d_a), F32), jax.ShapeDtypeStruct((bsz, nlb, d_a), F32),
                   jax.ShapeDtypeStruct((bsz, nsb, d_c), F32)],
        scratch_shapes=[pltpu.VMEM((tc + 8, d_a), F32), pltpu.VMEM((tc + 8, d_c), F32), pltpu.VMEM((1, d_a), F32)],
        compiler_params=_cparams("arbitrary", "arbitrary"),
        name="seq_mix_prompt",
    )(lru_in, sc_in, h0, lb0, sb0, *wts)


def _seq_mix_sample_kernel(lru_ref, sc_ref, h0_ref, lb0_ref, sb0_ref, cw_ref, cb_ref, wa_ref, ba_ref, wx_ref,
                           bx_ref, lam_ref, scw_ref, oa_ref, oc_ref, hn_ref, lbn_ref, sbn_ref, *, d_a, d_c):
    t_len = lru_ref.shape[0]
    nlb = lb0_ref.shape[0]
    nsb = sb0_ref.shape[0]
    xs = [lb0_ref[j] for j in range(nlb)] + [lru_ref[t][:, 0:d_a] for t in range(t_len)]
    vs = [sb0_ref[j] for j in range(nsb)] + [sc_ref[t][:, 0:d_c] * sc_ref[t][:, 2 * d_c:3 * d_c] for t in range(t_len)]
    h = h0_ref[...]
    for t in range(t_len):
        xc = cw_ref[0:1, :] * xs[t]
        for j in range(1, nlb + 1):
            xc = xc + cw_ref[j:j + 1, :] * xs[t + j]
        xc = xc + cb_ref[...]
        a, u = _lru_gates(xc, wa_ref, ba_ref, wx_ref, bx_ref, lam_ref)
        h = a * h + u
        oa_ref[t] = h * jax.nn.gelu(lru_ref[t][:, d_a:2 * d_a])
        uc = scw_ref[0:1, :] * vs[t]
        for j in range(1, nsb + 1):
            uc = uc + scw_ref[j:j + 1, :] * vs[t + j]
        oc_ref[t] = sc_ref[t][:, d_c:2 * d_c] * uc
    hn_ref[...] = h
    for j in range(nlb):
        lbn_ref[j] = xs[t_len + j]
    for j in range(nsb):
        sbn_ref[j] = vs[t_len + j]


def _seq_mix_sample(lru_t, sc_t, h0, lb0_t, sb0_t, wts, *, d_a, d_c):
    t_len, bsz, _ = lru_t.shape
    outs = [jax.ShapeDtypeStruct((t_len, bsz, d_a), F32), jax.ShapeDtypeStruct((t_len, bsz, d_c), F32),
            jax.ShapeDtypeStruct(h0.shape, F32), jax.ShapeDtypeStruct(lb0_t.shape, F32),
            jax.ShapeDtypeStruct(sb0_t.shape, F32)]
    return pl.pallas_call(
        functools.partial(_seq_mix_sample_kernel, d_a=d_a, d_c=d_c),
        out_shape=outs,
        compiler_params=pltpu.CompilerParams(vmem_limit_bytes=VMEM_LIMIT),
        name="seq_mix_sample",
    )(lru_t, sc_t, h0, lb0_t, sb0_t, *wts)


def _compress_kernel(kvc_ref, w_ref, out_ref):
    half = out_ref.shape[0] // 2
    x = kvc_ref[...].reshape(half, 2 * CMP_BLOCK, out_ref.shape[1])
    w = w_ref[...][None]
    out_ref[0:half, :] = jnp.sum(x[:, 0:CMP_BLOCK, :] * w, axis=1)
    out_ref[half:2 * half, :] = jnp.sum(x[:, CMP_BLOCK:2 * CMP_BLOCK, :] * w, axis=1)


def _compress_prompt(kvc, w_cmp, bsz, seq):
    kv_w = kvc.shape[1]
    n_cmp = seq // CMP_BLOCK
    return pl.pallas_call(
        _compress_kernel,
        grid=(bsz,),
        in_specs=[pl.BlockSpec((seq, kv_w), lambda b: (b, 0)), pl.BlockSpec(w_cmp.shape, lambda b: (0, 0))],
        out_specs=pl.BlockSpec((n_cmp, kv_w), lambda b: (b, 0)),
        out_shape=jax.ShapeDtypeStruct((bsz * n_cmp, kv_w), F32),
        compiler_params=_cparams("parallel"),
        name="nsa_compress",
    )(kvc, w_cmp)


def _pad_heads(q, lhs_ref, rows):
    lane = lax.broadcasted_iota(jnp.int32, (rows, LANES), 1)
    for h in range(N_HEADS):
        k = h // GROUP
        slab = q[:, (h // 2) * LANES:(h // 2 + 1) * LANES]
        if h % 2 != k:
            slab = pltpu.roll(slab, HEAD_DIM, axis=1)
        keep = (lane >= k * HEAD_DIM) & (lane < (k + 1) * HEAD_DIM)
        lhs_ref[h * rows:(h + 1) * rows, 0:LANES] = jnp.where(keep, slab, 0.0).astype(MXU_DTYPE)


def _gated_output(gate, o_c, o_s, o_w, rows):
    lane = lax.broadcasted_iota(jnp.int32, (rows, LANES), 1)
    slabs = []
    for m in range(N_HEADS // 2):
        parts = []
        for h in (2 * m, 2 * m + 1):
            k = h // GROUP
            sl = slice(h * rows, (h + 1) * rows)
            r = (gate[:, 3 * h:3 * h + 1] * o_c[sl] + gate[:, 3 * h + 1:3 * h + 2] * o_s[sl]
                 + gate[:, 3 * h + 2:3 * h + 3] * o_w[sl])
            if h % 2 != k:
                r = pltpu.roll(r, HEAD_DIM, axis=1)
            parts.append(r)
        slabs.append(jnp.where(lane < HEAD_DIM, parts[0], parts[1]))
    return slabs


def _compressed_branch(qpad, kcv, pos0, rows, n_sel):
    n_cmp = kcv.shape[0]
    half = n_cmp // 2
    pos_q = pos0 + lax.broadcasted_iota(jnp.int32, (rows, 1), 0)
    kc = kcv[:, 0:LANES].astype(MXU_DTYPE)
    vc = kcv[:, LANES:2 * LANES].astype(MXU_DTYPE)
    s_c = lax.dot_general(qpad, kc, _NT, preferred_element_type=F32)
    col = lax.broadcasted_iota(jnp.int32, (rows, n_cmp), 1)
    blk = jnp.where(col < half, 2 * col, 2 * (col - half) + 1)
    m_c = (blk + 1) * CMP_BLOCK - 1 <= pos_q
    ps = []
    imp = [jnp.zeros((rows, n_cmp), F32) for _ in range(N_KV)]
    for h in range(N_HEADS):
        p = _masked_softmax(s_c[h * rows:(h + 1) * rows], m_c)
        imp[h // GROUP] = imp[h // GROUP] + p
        ps.append(p.astype(MXU_DTYPE))
    o_c = jnp.dot(jnp.concatenate(ps, axis=0), vc, preferred_element_type=F32)
    imps = []
    for k in range(N_KV):
        imp_s = imp[k][:, 0:half] + imp[k][:, half:n_cmp]
        if half < LANES:
            imp_s = jnp.concatenate([imp_s, jnp.zeros((rows, LANES - half), F32)], axis=1)
        imps.append(imp_s)
    imp_t = jnp.concatenate(imps + [jnp.zeros((LANES - N_KV * rows, LANES), F32)], axis=0).T
    n_rows = -(-n_sel // 8) * 8
    pos_l = pos0 + lax.broadcasted_iota(jnp.int32, (1, LANES), 1) % rows
    sel_t = _select_blocks_keys(imp_t[0:n_rows], pos_l, n_sel)
    sel = jnp.concatenate([sel_t, jnp.zeros((LANES - n_rows, LANES), F32)], axis=0).T
    return o_c, [sel[k * rows:(k + 1) * rows] for k in range(N_KV)]


def _store_selection(lhs_ref, sels, rows):
    for k in range(N_KV):
        neg = jnp.where(sels[k] > 0.0, 0.0, NEG_INF).astype(MXU_DTYPE)
        for g in range(GROUP):
            h = k * GROUP + g
            lhs_ref[h * rows:(h + 1) * rows, LANES:2 * LANES] = neg


def _block_onehot(n_rows):
    row = lax.broadcasted_iota(jnp.int32, (n_rows, LANES), 0)
    lane = lax.broadcasted_iota(jnp.int32, (n_rows, LANES), 1)
    return jnp.where(row // SEL_BLOCK == lane, 1.0, 0.0).astype(MXU_DTYPE)


def _masked_softmax_keys(s, mask):
    s = jnp.where(mask, s, NEG_INF)
    m = jnp.max(s, axis=0, keepdims=True)
    p = jnp.where(mask, jnp.exp(s - m), 0.0)
    return p / jnp.maximum(jnp.sum(p, axis=0, keepdims=True), TINY)


def _nsa_prompt_kernel(q_ref, gate_ref, kcv_ref, kvs_ref, kvw_ref, out_ref,
                       kaug, vst, vwt, vct, lhs, m_sc, acc_sc, s_a, s_b, *, seq):
    qb = pl.program_id(1)
    rows = Q_BLOCK
    tk = KEY_TILE
    s0 = qb * rows
    n_sel = seq // SEL_BLOCK
    n_cmp = seq // CMP_BLOCK
    cols = N_HEADS * rows

    @pl.when(qb == 0)
    def _():
        kaug[:, 0:LANES] = kvs_ref[:, 0:LANES]
        kaug[:, LANES:2 * LANES] = _block_onehot(seq)
        vct[...] = kcv_ref[:, LANES:2 * LANES].T.astype(MXU_DTYPE)

        def tr_sel(i, c):
            r0 = pl.multiple_of(i * tk, tk)
            ones_row = jnp.where(lax.broadcasted_iota(jnp.int32, (SUM_ROWS, tk), 0) == 0, 1.0, 0.0)
            v_t = kvs_ref[pl.ds(r0, tk), LANES:2 * LANES].astype(F32).T
            vst[i] = jnp.concatenate([v_t, ones_row], axis=0).astype(MXU_DTYPE)
            return c

        def tr_win(i, c):
            r0 = pl.multiple_of(i * rows, rows)
            vwt[i] = kvw_ref[pl.ds(r0, rows), LANES:2 * LANES].astype(F32).T.astype(MXU_DTYPE)
            return c

        lax.fori_loop(0, seq // tk, tr_sel, 0)
        lax.fori_loop(0, seq // rows, tr_win, 0)

    _pad_heads(q_ref[...], lhs, rows)
    qpad = lhs[:, 0:LANES]
    lane = lax.broadcasted_iota(jnp.int32, (1, cols), 1)
    pos_q = s0 + lane % rows

    half = n_cmp // 2
    s_c = lax.dot_general(kcv_ref[:, 0:LANES].astype(MXU_DTYPE), qpad, _NT, preferred_element_type=F32)
    r_c = lax.broadcasted_iota(jnp.int32, (n_cmp, cols), 0)
    blk_c = jnp.where(r_c < half, 2 * r_c, 2 * (r_c - half) + 1)
    p_c = _masked_softmax_keys(s_c, (blk_c + 1) * CMP_BLOCK - 1 <= pos_q)
    o_c = jnp.dot(vct[...], p_c.astype(MXU_DTYPE), preferred_element_type=F32)

    imp = []
    for k in range(N_KV):
        acc = p_c[:, k * GROUP * rows:(k * GROUP + 1) * rows]
        for g in range(1, GROUP):
            acc = acc + p_c[:, (k * GROUP + g) * rows:(k * GROUP + g + 1) * rows]
        imp.append(acc[0:half] + acc[half:n_cmp])
    sel = _select_blocks_keys(jnp.concatenate(imp, axis=1), pos_q[:, 0:N_KV * rows], n_sel)
    for k in range(N_KV):
        sel_k = sel[:, k * rows:(k + 1) * rows]
        if n_sel < LANES:
            sel_k = jnp.concatenate([sel_k, jnp.zeros((LANES - n_sel, rows), F32)], axis=0)
        neg = jnp.where(sel_k.T > 0.0, 0.0, NEG_INF).astype(MXU_DTYPE)
        for g in range(GROUP):
            h = k * GROUP + g
            lhs[h * rows:(h + 1) * rows, LANES:2 * LANES] = neg

    m_sc[...] = jnp.full(m_sc.shape, NEG_INF, F32)
    acc_sc[...] = jnp.zeros(acc_sc.shape, F32)

    last_tile = seq // tk - 1

    def scores(s_ref, kt):
        k0 = pl.multiple_of(jnp.minimum(kt, last_tile) * tk, tk)
        s_ref[...] = lax.dot_general(kaug[pl.ds(k0, tk), :], lhs[...], _NT, preferred_element_type=F32)

    def update(s_ref, kt, causal):
        s = s_ref[...]
        if causal:
            s = jnp.where(kt * tk + lax.broadcasted_iota(jnp.int32, (tk, cols), 0) <= pos_q, s, NEG_INF)
        m_old = m_sc[...]
        m_new = jnp.maximum(m_old, jnp.max(s, axis=0, keepdims=True))
        alpha = jnp.exp(m_old - m_new)
        p = jnp.exp(s - m_new)
        m_sc[...] = m_new
        vt = vst[jnp.minimum(kt, last_tile)]
        acc_sc[...] = alpha * acc_sc[...] + jnp.dot(vt, p.astype(MXU_DTYPE), preferred_element_type=F32)

    n_pairs = (s0 // tk) // 2
    scores(s_a, 0)

    def pair_body(i, carry):
        scores(s_b, 2 * i + 1)
        update(s_a, 2 * i, False)
        scores(s_a, 2 * i + 2)
        update(s_b, 2 * i + 1, False)
        return carry

    lax.fori_loop(0, n_pairs, pair_body, 0)
    t0 = 2 * n_pairs
    scores(s_b, t0 + 1)
    update(s_a, t0, True)

    @pl.when((t0 + 1) * tk <= s0 + rows - 1)
    def _():
        update(s_b, t0 + 1, True)

    o_s = acc_sc[0:LANES, :] / jnp.maximum(acc_sc[LANES:LANES + 1, :], TINY)

    band = WINDOW + rows
    start = pl.multiple_of(jnp.maximum(s0 - WINDOW, 0), rows)
    s_w = lax.dot_general(kvw_ref[pl.ds(start, band), 0:LANES], qpad, _NT, preferred_element_type=F32)
    dlt = pos_q - (start + lax.broadcasted_iota(jnp.int32, (band, cols), 0))
    p_w = _masked_softmax_keys(s_w, (dlt >= 0) & (dlt <= WINDOW))
    t0 = start // rows
    vw = jnp.concatenate([vwt[t0 + j] for j in range(band // rows)], axis=1)
    o_w = jnp.dot(vw, p_w.astype(MXU_DTYPE), preferred_element_type=F32)

    g_t = gate_ref[...].T
    for m in range(N_HEADS // 2):
        parts = []
        for h in (2 * m, 2 * m + 1):
            k = h // GROUP
            rs = slice(k * HEAD_DIM, (k + 1) * HEAD_DIM)
            ls = slice(h * rows, (h + 1) * rows)
            parts.append(g_t[3 * h:3 * h + 1] * o_c[rs, ls] + g_t[3 * h + 1:3 * h + 2] * o_s[rs, ls]
                         + g_t[3 * h + 2:3 * h + 3] * o_w[rs, ls])
        out_ref[:, m * LANES:(m + 1) * LANES] = jnp.concatenate(parts, axis=0).T


def _nsa_prompt(q, gate, kcv, kvs_b, kvw_b, bsz, seq):
    n, d_b = bsz * seq, q.shape[1]
    kv_w = kvs_b.shape[1]
    nq = seq // Q_BLOCK
    n_cmp = seq // CMP_BLOCK
    rows = N_HEADS * Q_BLOCK
    return pl.pallas_call(
        functools.partial(_nsa_prompt_kernel, seq=seq),
        grid=(bsz, nq),
        in_specs=[pl.BlockSpec((Q_BLOCK, d_b), lambda b, i: (b * nq + i, 0)),
                  pl.BlockSpec((Q_BLOCK, LANES), lambda b, i: (b * nq + i, 0)),
                  pl.BlockSpec((n_cmp, kv_w), lambda b, i: (b, 0)),
                  pl.BlockSpec((seq, kv_w), lambda b, i: (b, 0)),
                  pl.BlockSpec((seq, kv_w), lambda b, i: (b, 0))],
        out_specs=pl.BlockSpec((Q_BLOCK, d_b), lambda b, i: (b * nq + i, 0)),
        out_shape=jax.ShapeDtypeStruct((n, d_b), F32),
        scratch_shapes=[pltpu.VMEM((seq, 2 * LANES), MXU_DTYPE),
                        pltpu.VMEM((seq // KEY_TILE, LANES + SUM_ROWS, KEY_TILE), MXU_DTYPE),
                        pltpu.VMEM((seq // Q_BLOCK, LANES, Q_BLOCK), MXU_DTYPE),
                        pltpu.VMEM((LANES, n_cmp), MXU_DTYPE),
                        pltpu.VMEM((rows, 2 * LANES), MXU_DTYPE),
                        pltpu.VMEM((1, rows), F32), pltpu.VMEM((LANES + SUM_ROWS, rows), F32),
                        pltpu.VMEM((KEY_TILE, rows), F32), pltpu.VMEM((KEY_TILE, rows), F32)],
        compiler_params=_cparams("arbitrary", "arbitrary"),
        name="nsa_prompt",
    )(q, gate, kcv, kvs_b, kvw_b)


def _nsa_sample_kernel(pt_ref, q_ref, gate_ref, ksn_ref, kwn_ref, wcmp_ref, win_ref, *rest,
                       n_pages, page, past, t_len):
    cmp_pages = rest[:n_pages]
    sel_pages = rest[n_pages:2 * n_pages]
    out_ref, nwin_ref, kaug, vsel, kcv, lhs = rest[2 * n_pages:]
    rows = 8
    n_cmp = past // CMP_BLOCK
    n_sel = pl.cdiv(past + t_len, SEL_BLOCK)
    per_page = page // CMP_BLOCK

    @pl.when(pl.program_id(0) == 0)
    def _():
        kaug[:, LANES:2 * LANES] = _block_onehot(past)

    q = jnp.concatenate([q_ref[0], jnp.zeros((rows - t_len, q_ref.shape[2]), F32)], axis=0)
    _pad_heads(q, lhs, rows)
    qpad = lhs[:, 0:LANES]
    tq = lax.broadcasted_iota(jnp.int32, (rows, 1), 0)
    pos_q = past + tq

    wrep = jnp.concatenate([wcmp_ref[...]] * per_page, axis=0)
    for p in range(n_pages):
        x = cmp_pages[p][...] * wrep
        for j in range(per_page):
            blk = p * per_page + j
            dst = (blk % 2) * (n_cmp // 2) + blk // 2
            kcv[dst:dst + 1, :] = jnp.sum(x[j * CMP_BLOCK:(j + 1) * CMP_BLOCK], axis=0, keepdims=True)
        kaug[p * page:(p + 1) * page, 0:LANES] = sel_pages[p][:, 0:LANES].astype(MXU_DTYPE)
        vsel[p * page:(p + 1) * page, :] = sel_pages[p][:, LANES:2 * LANES].astype(MXU_DTYPE)

    o_c, sels = _compressed_branch(qpad, kcv[...], past, rows, n_sel)
    _store_selection(lhs, sels, rows)

    def new_rows(ref):
        kv = jnp.concatenate([ref[0], jnp.zeros((LANES - t_len, ref.shape[2]), F32)], axis=0)
        return kv[:, 0:LANES].astype(MXU_DTYPE), kv[:, LANES:2 * LANES].astype(MXU_DTYPE)

    tk_new = lax.broadcasted_iota(jnp.int32, (rows, LANES), 1)
    m_new = (tk_new < t_len) & (tk_new <= tq)

    def joint_attention(s_past, mask_past, v_past, s_new, v_new):
        outs = []
        p_past, p_new = [], []
        for h in range(N_HEADS):
            sl = slice(h * rows, (h + 1) * rows)
            sp = s_past[sl] if mask_past is None else jnp.where(mask_past, s_past[sl], NEG_INF)
            sn = jnp.where(m_new, s_new[sl], NEG_INF)
            m = jnp.maximum(jnp.max(sp, axis=-1, keepdims=True), jnp.max(sn, axis=-1, keepdims=True))
            pp = jnp.exp(sp - m) if mask_past is None else jnp.where(mask_past, jnp.exp(sp - m), 0.0)
            pn = jnp.where(m_new, jnp.exp(sn - m), 0.0)
            den = jnp.maximum(jnp.sum(pp, axis=-1, keepdims=True) + jnp.sum(pn, axis=-1, keepdims=True), TINY)
            p_past.append((pp / den).astype(MXU_DTYPE))
            p_new.append((pn / den).astype(MXU_DTYPE))
        return (jnp.dot(jnp.concatenate(p_past, axis=0), v_past, preferred_element_type=F32)
                + jnp.dot(jnp.concatenate(p_new, axis=0), v_new, preferred_element_type=F32))

    kn, vn = new_rows(ksn_ref)
    s_past = lax.dot_general(lhs[...], kaug[...], _NT, preferred_element_type=F32)
    s_new = lax.dot_general(qpad, kn, _NT, preferred_element_type=F32)
    o_s = joint_attention(s_past, None, vsel[...], s_new, vn)

    n_buf = win_ref.shape[0]
    kwn, vwn = new_rows(kwn_ref)
    kwb = win_ref[:, 0:LANES].astype(MXU_DTYPE)
    vwb = win_ref[:, LANES:2 * LANES].astype(MXU_DTYPE)
    s_wb = lax.dot_general(qpad, kwb, _NT, preferred_element_type=F32)
    s_wn = lax.dot_general(qpad, kwn, _NT, preferred_element_type=F32)
    pos_w = past - n_buf + lax.broadcasted_iota(jnp.int32, (rows, n_buf), 1)
    dlt = pos_q - pos_w
    m_wb = (dlt >= 0) & (dlt <= WINDOW) & (pos_w >= 0)
    o_w = joint_attention(s_wb, m_wb, vwb, s_wn, vwn)

    for m, slab in enumerate(_gated_output(
            jnp.concatenate([gate_ref[0], jnp.zeros((rows - t_len, LANES), F32)], axis=0), o_c, o_s, o_w, rows)):
        out_ref[0, :, m * LANES:(m + 1) * LANES] = slab[0:t_len]

    nwin_ref[0:n_buf - t_len, :] = win_ref[t_len:n_buf, :]
    nwin_ref[n_buf - t_len:n_buf, :] = kwn_ref[0]


def _nsa_sample(page_table, q, gate, kvs_new, kvw_new, w_cmp, pool_c, pool_s, win, layer):
    dbs, t_len, d_b = q.shape
    n_pages = page_table.shape[1]
    page, kv_w = pool_c.shape[2], pool_c.shape[3]
    n_buf = win.shape[2]
    past = n_pages * page
    tok = lambda w: pl.BlockSpec((1, t_len, w), lambda b, pt: (b, 0, 0))

    def page_spec(p):
        return pl.BlockSpec((None, None, page, kv_w), lambda b, pt, p=p: (layer, pt[b * n_pages + p], 0, 0))

    grid_spec = pltpu.PrefetchScalarGridSpec(
        num_scalar_prefetch=1,
        grid=(dbs,),
        in_specs=[tok(d_b), tok(LANES), tok(kv_w), tok(kv_w),
                  pl.BlockSpec(w_cmp.shape, lambda b, pt: (0, 0)),
                  pl.BlockSpec((None, None, n_buf, kv_w), lambda b, pt: (layer, b, 0, 0))]
                 + [page_spec(p) for p in range(n_pages)] * 2,
        out_specs=[tok(d_b), pl.BlockSpec((None, n_buf, kv_w), lambda b, pt: (b, 0, 0))],
        scratch_shapes=[pltpu.VMEM((past, 2 * LANES), MXU_DTYPE), pltpu.VMEM((past, LANES), MXU_DTYPE),
                        pltpu.VMEM((past // CMP_BLOCK, kv_w), F32), pltpu.VMEM((N_HEADS * 8, 2 * LANES), MXU_DTYPE)],
    )
    return pl.pallas_call(
        functools.partial(_nsa_sample_kernel, n_pages=n_pages, page=page, past=past, t_len=t_len),
        grid_spec=grid_spec,
        out_shape=[jax.ShapeDtypeStruct((dbs, t_len, d_b), F32), jax.ShapeDtypeStruct((dbs, n_buf, kv_w), F32)],
        compiler_params=_cparams("arbitrary"),
        name="nsa_sample",
    )(page_table.reshape(-1), q, gate, kvs_new, kvw_new, w_cmp, win, *([pool_c] * n_pages), *([pool_s] * n_pages))


def _out_proj_router_kernel(*refs, n_x, n_ptiles):
    x_refs = refs[:n_x]
    (oa_ref, ob_ref, oc_ref, oas_ref, obs_ref, ocs_ref, wo_ref, g2_ref, wr_ref, br_ref,
     xn_ref, h2_ref, ri_ref, rw_ref, cnt_ref, run_ref) = refs[n_x:]
    tm = oa_ref.shape[0]

    @pl.when(pl.program_id(0) == 0)
    def _():
        run_ref[...] = jnp.zeros(run_ref.shape, F32)

    is_sample = pl.program_id(0) >= n_ptiles
    mix = jnp.concatenate([jnp.where(is_sample, oas_ref[...], oa_ref[...]),
                           jnp.where(is_sample, obs_ref[...], ob_ref[...]),
                           jnp.where(is_sample, ocs_ref[...], oc_ref[...])], axis=1)
    xn = _token_tile(x_refs, n_ptiles) + _split_dot(_split_act(mix, wo_ref.shape[0]), wo_ref)
    xn_ref[...] = xn
    h2 = _rmsnorm(xn, g2_ref[...])
    _store_token_tiles(h2_ref, h2)
    logits = _split_dot(_split_act(h2, wr_ref.shape[0]), wr_ref) + br_ref[...]

    lane = lax.broadcasted_iota(jnp.int32, (tm, LANES), 1)
    lanef = lane.astype(F32)

    def softmax_over(mask):
        m = jnp.max(jnp.where(mask, logits, -jnp.inf), axis=-1, keepdims=True)
        e = jnp.where(mask, jnp.exp(logits - m), 0.0)
        return e / jnp.sum(e, axis=-1, keepdims=True)

    def first_max(p, mask):
        pm = jnp.max(jnp.where(mask, p, -1.0), axis=-1, keepdims=True)
        idx = jnp.min(jnp.where(mask & (p == pm), lanef, 1e9), axis=-1, keepdims=True)
        return pm, idx

    is_g = lane < N_GROUPS
    g_wt, g_sel = first_max(softmax_over(is_g), is_g)
    lo = N_GROUPS + EXP_PER_GROUP * g_sel
    in_e = (lanef >= lo) & (lanef < lo + EXP_PER_GROUP)
    p_e = softmax_over(in_e)
    p0, i0 = first_max(p_e, in_e)
    p1, i1 = first_max(p_e, in_e & (lanef != i0))
    den = p0 + p1
    e0 = i0 - N_GROUPS
    e1 = i1 - N_GROUPS

    hit0 = lanef == e0
    hit1 = lanef == e1
    onehot = jnp.where(hit0 | hit1, 1.0, 0.0)
    r_i = lax.broadcasted_iota(jnp.int32, (tm, tm), 0)
    c_i = lax.broadcasted_iota(jnp.int32, (tm, tm), 1)
    ltri = jnp.where(c_i < r_i, 1.0, 0.0).astype(jnp.bfloat16)
    before = jnp.dot(ltri, onehot.astype(jnp.bfloat16), preferred_element_type=F32) + run_ref[0:1, :]
    r0 = jnp.sum(jnp.where(hit0, before, 0.0), axis=-1, keepdims=True)
    r1 = jnp.sum(jnp.where(hit1, before, 0.0), axis=-1, keepdims=True)
    run_ref[...] = run_ref[...] + jnp.sum(onehot, axis=0, keepdims=True)
    cnt_ref[...] = run_ref[...]

    ri = jnp.where(lane == 0, e0, jnp.where(lane == 1, e1, jnp.where(lane == 2, r0, jnp.where(lane == 3, r1, 0.0))))
    ri_ref[...] = ri.T[0:8].astype(jnp.int32)
    rw_ref[...] = jnp.where(lane == 0, g_wt * p0 / den, jnp.where(lane == 1, g_wt * p1 / den, 0.0))


def _out_proj_router(x, mix_p, mix_s, w_out, g2, w_route, b_route):
    tm = TOKEN_TILE
    n_ptiles = mix_p[0].shape[0] // tm
    x_ops, x_specs = _token_specs(x, tm, n_ptiles)
    n, d = sum(a.shape[0] for a in x_ops), x_ops[0].shape[1]
    assert all(a.shape[0] == tm for a in mix_s) and n == (n_ptiles + 1) * tm
    row = lambda w: pl.BlockSpec((tm, w), lambda i: (i, 0))
    prow = lambda a: pl.BlockSpec((tm, a.shape[1]), lambda i: (jnp.minimum(i, n_ptiles - 1), 0))
    full = lambda a: pl.BlockSpec(a.shape, lambda i: (0,) * a.ndim)
    return pl.pallas_call(
        functools.partial(_out_proj_router_kernel, n_x=len(x_ops), n_ptiles=n_ptiles),
        grid=(n // tm,),
        in_specs=x_specs + [prow(a) for a in mix_p] + [full(a) for a in mix_s]
                 + [full(w_out), full(g2), full(w_route), full(b_route)],
        out_specs=[row(d), pl.BlockSpec((tm * SUBLANES, LANES), lambda i: (i, 0)),
                   pl.BlockSpec((8, tm), lambda i: (0, i)), row(LANES), pl.BlockSpec((8, LANES), lambda i: (0, 0))],
        out_shape=[jax.ShapeDtypeStruct((n, d), F32), jax.ShapeDtypeStruct((n * SUBLANES, LANES), F32),
                   jax.ShapeDtypeStruct((8, n), jnp.int32), jax.ShapeDtypeStruct((n, LANES), F32),
                   jax.ShapeDtypeStruct((8, LANES), F32)],
        scratch_shapes=[pltpu.VMEM((8, LANES), F32)],
        compiler_params=_cparams("arbitrary"),
        name="out_proj_router",
    )(*x_ops, *mix_p, *mix_s, w_out, g2, w_route, b_route)


def _store_token_tiles(ref, x):
    t = x.shape[0]
    for c in range(SUBLANES):
        ref[pl.ds(c, t, stride=SUBLANES), :] = x[:, c * LANES:(c + 1) * LANES]


def _load_token_tiles(ref, first_row, t):
    return jnp.concatenate([ref[pl.ds(first_row + c, t, stride=SUBLANES), :] for c in range(SUBLANES)], axis=1)


def _row_copy(src_hbm, row, dst, slot, r, sem):
    return pltpu.make_async_copy(src_hbm.at[pl.ds(pl.multiple_of(row * SUBLANES, SUBLANES), SUBLANES)],
                                 dst.at[slot, pl.ds(pl.multiple_of(r * SUBLANES, SUBLANES), SUBLANES)],
                                 sem.at[slot])


def _expert_kernel(te_ref, nv_ref, src_ref, h_hbm, wg_ref, wu_ref, wd_ref, ys_ref, xbuf, sem):
    t = pl.program_id(0)
    nv = nv_ref[0]
    te = xbuf.shape[1] // SUBLANES

    def issue(tile, slot):
        def body(i, carry):
            for j in range(2):
                r = 2 * i + j
                _row_copy(h_hbm, src_ref[tile * te + r], xbuf, slot, r, sem).start(priority=j)
            return carry
        lax.fori_loop(0, te // 2, body, 0, unroll=4)

    def wait_all(slot):
        pltpu.make_async_copy(h_hbm.at[pl.ds(0, te * SUBLANES)], xbuf.at[slot], sem.at[slot]).wait()

    @pl.when((t == 0) & (nv > 0))
    def _():
        issue(0, 0)

    @pl.when(t + 1 < nv)
    def _():
        issue(t + 1, (t + 1) % 2)

    @pl.when(t < nv)
    def _():
        slot = t % 2
        wait_all(slot)
        x = _load_token_tiles(xbuf.at[slot], 0, te).astype(MXU_DTYPE)
        hg = jnp.dot(x, wg_ref[...].astype(MXU_DTYPE), preferred_element_type=F32)
        hu = jnp.dot(x, wu_ref[...].astype(MXU_DTYPE), preferred_element_type=F32)
        act = (jax.nn.silu(hg) * hu).astype(MXU_DTYPE)
        _store_token_tiles(ys_ref, jnp.dot(act, wd_ref[...].astype(MXU_DTYPE), preferred_element_type=F32))

    @pl.when(t >= nv)
    def _():
        ys_ref[...] = jnp.zeros(ys_ref.shape, F32)


def _experts(tile_expert, n_valid, src_tok, h2, w_gate, w_up, w_down, layer):
    n_tiles = tile_expert.shape[0]
    te = EXPERT_TILE
    d = SUBLANES * LANES
    d_e = w_gate.shape[3]
    assert w_gate.shape[2] == d and h2.shape[1] == LANES
    wspec = lambda r, c: pl.BlockSpec((None, None, r, c), lambda t, te_r, nv_r, src_r: (layer, te_r[t], 0, 0))
    grid_spec = pltpu.PrefetchScalarGridSpec(
        num_scalar_prefetch=3,
        grid=(n_tiles,),
        in_specs=[pl.BlockSpec(memory_space=pl.ANY), wspec(d, d_e), wspec(d, d_e), wspec(d_e, d)],
        out_specs=pl.BlockSpec((te * SUBLANES, LANES), lambda t, te_r, nv_r, src_r: (t, 0)),
        scratch_shapes=[pltpu.VMEM((2, te * SUBLANES, LANES), F32), pltpu.SemaphoreType.DMA((2,))],
    )
    return pl.pallas_call(
        _expert_kernel,
        grid_spec=grid_spec,
        out_shape=jax.ShapeDtypeStruct((n_tiles * te * SUBLANES, LANES), F32),
        compiler_params=_cparams("arbitrary"),
        name="moe_experts",
    )(tile_expert, n_valid, src_tok, h2, w_gate, w_up, w_down)


def _combine_kernel(d0_ref, d1_ref, x_ref, rw_ref, gf_ref, ys_hbm, *rest, final):
    if final:
        xo_ref, y_ref, buf, sem = rest
    else:
        xo_ref, buf, sem = rest
    t = pl.program_id(0)
    nt = pl.num_programs(0)
    tm = x_ref.shape[0]

    def issue(tile, slot):
        def body(r, carry):
            _row_copy(ys_hbm, d0_ref[tile * tm + r], buf, slot, r, sem).start(priority=0)
            _row_copy(ys_hbm, d1_ref[tile * tm + r], buf, slot, tm + r, sem).start(priority=1)
            return carry
        lax.fori_loop(0, tm, body, 0, unroll=8)

    def wait_all(slot):
        pltpu.make_async_copy(ys_hbm.at[pl.ds(0, 2 * tm * SUBLANES)], buf.at[slot], sem.at[slot]).wait()

    @pl.when(t == 0)
    def _():
        issue(0, 0)

    @pl.when(t + 1 < nt)
    def _():
        issue(t + 1, (t + 1) % 2)

    slot = t % 2
    wait_all(slot)
    rw = rw_ref[...]
    xo = (x_ref[...] + rw[:, 0:1] * _load_token_tiles(buf.at[slot], 0, tm)
          + rw[:, 1:2] * _load_token_tiles(buf.at[slot], tm * SUBLANES, tm))
    xo_ref[...] = xo
    if final:
        y_ref[...] = _rmsnorm(xo, gf_ref[...])


def _combine(d0, d1, x, rw, gf, ys, final):
    n, d = x.shape
    tm = COMBINE_TILE
    row = lambda w: pl.BlockSpec((tm, w), lambda t, a, b: (t, 0))
    n_out = 2 if final else 1
    grid_spec = pltpu.PrefetchScalarGridSpec(
        num_scalar_prefetch=2,
        grid=(n // tm,),
        in_specs=[row(d), row(LANES), pl.BlockSpec((1, d), lambda t, a, b: (0, 0)), pl.BlockSpec(memory_space=pl.ANY)],
        out_specs=[row(d)] * n_out,
        scratch_shapes=[pltpu.VMEM((2, 2 * tm * SUBLANES, LANES), F32), pltpu.SemaphoreType.DMA((2,))],
    )
    return pl.pallas_call(
        functools.partial(_combine_kernel, final=final),
        grid_spec=grid_spec,
        out_shape=[jax.ShapeDtypeStruct((n, d), F32)] * n_out,
        compiler_params=_cparams("arbitrary"),
        name="moe_combine_final" if final else "moe_combine",
    )(d0, d1, x, rw, gf, ys)


def _route_plan(ri, cnt, n_tiles):
    te = EXPERT_TILE
    n = ri.shape[1]
    counts = cnt[0, :N_EXPERTS].astype(jnp.int32)
    padded = ((counts + te - 1) // te) * te
    ends = jnp.cumsum(padded)
    offs = ends - padded
    d0 = offs[ri[0]] + ri[2]
    d1 = offs[ri[1]] + ri[3]
    tok = jnp.arange(n, dtype=jnp.int32)
    src = jnp.zeros((n_tiles * te,), jnp.int32).at[jnp.concatenate([d0, d1])].set(jnp.concatenate([tok, tok]))
    tile_start = jnp.arange(n_tiles, dtype=jnp.int32) * te
    tile_expert = jnp.minimum(jnp.sum((ends[None, :] <= tile_start[:, None]).astype(jnp.int32), axis=1),
                              N_EXPERTS - 1)
    n_valid = (ends[-1:] // te).astype(jnp.int32)
    return d0, d1, src, tile_expert, n_valid


def _rope_tables(pos):
    half = HEAD_DIM // 2
    inv = ROPE_THETA ** (-jnp.arange(half, dtype=F32) / half)
    ang = pos.astype(F32)[:, None] * inv[None, :]
    cos = jnp.cos(ang)
    sin = jnp.sin(ang)
    cos_t = jnp.concatenate([cos, cos] * (LANES // HEAD_DIM), axis=1)
    sin_t = jnp.concatenate([-sin, sin] * (LANES // HEAD_DIM), axis=1)
    return cos_t, sin_t


def _block_diag(w):
    nb, bw, _ = w.shape
    out = jnp.zeros((nb * bw, nb * bw), w.dtype)
    for i in range(nb):
        out = out.at[i * bw:(i + 1) * bw, i * bw:(i + 1) * bw].set(w[i])
    return out


def kernel(x_prompt, x_sample, cache_kv_cmp, cache_kv_sel, cache_kv_win, state_lru_h, state_lru_conv, state_sconv, page_table, norm1_g, w_in, lru_conv_w, lru_conv_b, lru_wa, lru_ba, lru_wx, lru_bx, lru_lambda, nsa_cmp_wk, nsa_cmp_wv, sc_conv_w, w_out, norm2_g, router_group_w, router_group_b, router_exp_w, router_exp_b, exp_w_gate, exp_w_up, exp_w_down, norm_f_g):
    bsz, seq, d = x_prompt.shape
    dbs, t_len, _ = x_sample.shape
    depth = w_in.shape[0]
    d_a = lru_conv_w.shape[2]
    d_c = sc_conv_w.shape[2]
    kv_w = 2 * N_KV * HEAD_DIM
    d_b = N_HEADS * HEAD_DIM
    n_gate = 3 * N_HEADS
    n_p = bsz * seq
    n_s = dbs * t_len
    n = n_p + n_s
    assert n_s == TOKEN_TILE and n_p % TOKEN_TILE == 0 and seq % TOKEN_TILE == 0
    page = cache_kv_cmp.shape[2]
    past = page_table.shape[1] * page
    n_buf = cache_kv_win.shape[2]

    pos = jnp.concatenate([jnp.arange(seq, dtype=jnp.int32),
                           jnp.tile(past + jnp.arange(t_len, dtype=jnp.int32), dbs)])
    cos_t, sin_t = _rope_tables(pos)

    pool_c = cache_kv_cmp.reshape(depth, -1, page, kv_w)
    pool_s = cache_kv_sel.reshape(depth, -1, page, kv_w)
    win = cache_kv_win.reshape(depth, dbs, n_buf, kv_w)

    g_off = 2 * d_a + d_b + 3 * kv_w
    x = (x_prompt.reshape(n_p, d), x_sample.reshape(n_s, d))
    h0_p = jnp.zeros((bsz, 1, d_a), F32)
    lb0_p = jnp.zeros((bsz, lru_conv_w.shape[1] - 1, d_a), F32)
    sb0_p = jnp.zeros((bsz, sc_conv_w.shape[1] - 1, d_c), F32)

    n_tiles = (2 * n) // EXPERT_TILE + N_EXPERTS
    proj_parts = lambda l: 2 if l < depth - 1 else 1
    states_p, states_s = [], []
    y = None
    for l in range(depth):
        w_l = w_in[l]
        w_r = _split_weight(jnp.concatenate([w_l[:, :g_off], w_l[:, g_off + n_gate:], w_l[:, g_off:g_off + n_gate],
                                             jnp.zeros((d, LANES - n_gate), F32)], axis=1), proj_parts(l))
        lru_in, sc_in, gate, q, kvc, kvs, kvw, kvs_b, kvw_b = _in_proj(
            x, norm1_g[l][None], w_r, cos_t, sin_t, n_p, seq, d_a=d_a, d_b=d_b, kv_w=kv_w, d_c=d_c)

        seq_w = (lru_conv_w[l], lru_conv_b[l][None], _block_diag(lru_wa[l]).astype(MXU_DTYPE), lru_ba[l][None],
                 _block_diag(lru_wx[l]).astype(MXU_DTYPE), lru_bx[l][None], lru_lambda[l][None], sc_conv_w[l])
        out_a, out_c, hn_p, lbn_p, sbn_p = _seq_mix_prompt(lru_in, sc_in, h0_p, lb0_p, sb0_p, seq_w, bsz, seq,
                                                           d_a=d_a, d_c=d_c)
        tmaj = lambda a: jnp.swapaxes(a.reshape(dbs, t_len, -1), 0, 1)
        oa_s, oc_s, hn_s, lbn_s, sbn_s = _seq_mix_sample(
            tmaj(lru_in[n_p:]), tmaj(sc_in[n_p:]), state_lru_h[l], jnp.swapaxes(state_lru_conv[l], 0, 1),
            jnp.swapaxes(state_sconv[l], 0, 1), seq_w, d_a=d_a, d_c=d_c)
        oa_s = jnp.swapaxes(oa_s, 0, 1).reshape(n_s, d_a)
        oc_s = jnp.swapaxes(oc_s, 0, 1).reshape(n_s, d_c)

        w_cmp = jnp.concatenate([jnp.broadcast_to(nsa_cmp_wk[l][:, None], (CMP_BLOCK, kv_w // 2)),
                                 jnp.broadcast_to(nsa_cmp_wv[l][:, None], (CMP_BLOCK, kv_w // 2))], axis=1)
        kcv = _compress_prompt(kvc, w_cmp, bsz, seq)
        out_b = _nsa_prompt(q, gate, kcv, kvs_b, kvw_b, bsz, seq)
        s3 = lambda a: a[n_p:].reshape(dbs, t_len, -1)
        ob_s, nwin_s = _nsa_sample(page_table, s3(q), s3(gate), s3(kvs), s3(kvw), w_cmp, pool_c, pool_s, win, l)
        ob_s = ob_s.reshape(n_s, d_b)

        w_route = _split_weight(jnp.concatenate([router_group_w[l], router_exp_w[l],
                                                 jnp.zeros((d, LANES - N_GROUPS - N_EXPERTS), F32)], axis=1), 2)
        b_route = jnp.concatenate([router_group_b[l], router_exp_b[l],
                                   jnp.zeros((LANES - N_GROUPS - N_EXPERTS,), F32)])[None]
        xn, h2, ri, rw, cnt = _out_proj_router(x, (out_a, out_b, out_c), (oa_s, ob_s, oc_s),
                                               _split_weight(w_out[l], proj_parts(l)), norm2_g[l][None],
                                               w_route, b_route)
        d0, d1, src, tile_expert, n_valid = _route_plan(ri, cnt, n_tiles)
        ys = _experts(tile_expert, n_valid, src, h2, exp_w_gate, exp_w_up, exp_w_down, l)
        final = l == depth - 1
        outs = _combine(d0, d1, xn, rw, norm_f_g[None], ys, final)
        x = outs[0]
        if final:
            y = outs[1]

        kv6 = lambda a, lead: a.reshape(lead + (2, N_KV, HEAD_DIM))
        states_p.append((kv6(kvc[:n_p], (bsz, seq)), kv6(kvs[:n_p], (bsz, seq)),
                         kv6(kvw[:n_p].reshape(bsz, seq, kv_w)[:, seq - min(WINDOW, seq):], (bsz, min(WINDOW, seq))),
                         hn_p[:, 0], lbn_p, sbn_p))
        states_s.append((kv6(kvc[n_p:], (dbs, t_len)), kv6(kvs[n_p:], (dbs, t_len)), kv6(nwin_s, (dbs, n_buf)),
                         hn_s, jnp.swapaxes(lbn_s, 0, 1), jnp.swapaxes(sbn_s, 0, 1)))

    stack = lambda sts, i: jnp.stack([s[i] for s in sts])
    res = [y[:n_p].reshape(bsz, seq, d), y[n_p:].reshape(dbs, t_len, d)]
    for i in range(6):
        res += [stack(states_p, i), stack(states_s, i)]
    return tuple(res)
```

```python
import functools

import jax
import jax.numpy as jnp
from jax import lax
from jax.experimental import pallas as pl
from jax.experimental.pallas import tpu as pltpu

F32 = jnp.float32
MXU_DTYPE = jnp.bfloat16

HEAD_DIM = 64
N_HEADS = 8
N_KV = 2
GROUP = N_HEADS // N_KV
CMP_BLOCK = 32
SEL_BLOCK = 64
TOP_N = 16
WINDOW = 512
Q_BLOCK = 128
ROPE_THETA = 10000.0
LRU_C = 8.0
N_GROUPS = 4
EXP_PER_GROUP = 8
N_EXPERTS = N_GROUPS * EXP_PER_GROUP
RMS_EPS = 1e-6
NEG_INF = -1e30
TINY = 1e-30
FORCE = 1e6

LANES = 128
SUBLANES = 8
SUM_ROWS = 2 * SUBLANES
VMEM_LIMIT = 56 * 2 ** 20
TOKEN_TILE = 512
TIME_CHUNK = 256
KEY_TILE = 512
EXPERT_TILE = 256
COMBINE_TILE = 256

_NT = (((1,), (1,)), ((), ()))


def _cparams(*sem):
    return pltpu.CompilerParams(dimension_semantics=sem, vmem_limit_bytes=VMEM_LIMIT)


def _split_weight(w, parts):
    if parts == 1:
        return w.astype(MXU_DTYPE)[None]
    hi = lax.bitcast_convert_type(lax.bitcast_convert_type(w, jnp.uint32) & jnp.uint32(0xFFFF0000), F32)
    return jnp.stack([hi.astype(MXU_DTYPE), (w - hi).astype(MXU_DTYPE)])


def _split_act(a, parts):
    if parts == 1:
        return (a.astype(MXU_DTYPE),)
    hi = lax.bitcast_convert_type(lax.bitcast_convert_type(a, jnp.uint32) & jnp.uint32(0xFFFF0000), F32)
    return hi.astype(MXU_DTYPE), (a - hi).astype(MXU_DTYPE)


def _split_dot(a_parts, w_ref, cols=slice(None)):
    dot = lambda a, p: jnp.dot(a, w_ref[p, :, cols], preferred_element_type=F32)
    if len(a_parts) == 1:
        return dot(a_parts[0], 0)
    return (dot(a_parts[1], 1) + (dot(a_parts[0], 1) + dot(a_parts[1], 0))) + dot(a_parts[0], 0)


def _rmsnorm(x, g):
    return x * lax.rsqrt(jnp.mean(x * x, axis=-1, keepdims=True) + RMS_EPS) * g


def _swap_halves(x):
    w = x.shape[-1]
    lane = lax.broadcasted_iota(jnp.int32, x.shape, x.ndim - 1)
    first = (lane % HEAD_DIM) < HEAD_DIM // 2
    return jnp.where(first, pltpu.roll(x, w - HEAD_DIM // 2, axis=1), pltpu.roll(x, HEAD_DIM // 2, axis=1))


def _masked_softmax(s, mask):
    s = jnp.where(mask, s, NEG_INF)
    m = jnp.max(s, axis=-1, keepdims=True)
    p = jnp.where(mask, jnp.exp(s - m), 0.0)
    return p / jnp.maximum(jnp.sum(p, axis=-1, keepdims=True), TINY)


def _select_blocks_keys(imp, pos_q, n_sel):
    blk = lax.broadcasted_iota(jnp.int32, imp.shape, 0)
    cur = pos_q // SEL_BLOCK
    forced = (blk == 0) | (blk == cur) | (blk == cur - 1)
    w = jnp.where(forced, FORCE, imp)
    w = jnp.where(blk <= cur, w, NEG_INF)
    w = jnp.where(blk < n_sel, w, -jnp.inf)
    blkf = blk.astype(F32)

    def body(_, carry):
        w, sel = carry
        m = jnp.max(w, axis=0, keepdims=True)
        first = jnp.min(jnp.where(w == m, blkf, 1e9), axis=0, keepdims=True)
        pick = blkf == first
        return jnp.where(pick, -jnp.inf, w), jnp.where(pick, 1.0, sel)

    _, sel = lax.fori_loop(0, min(TOP_N, n_sel), body, (w, jnp.zeros_like(w)))
    return sel


def _token_specs(x, tm, n_ptiles):
    if not isinstance(x, tuple):
        return [x], [pl.BlockSpec((tm, x.shape[1]), lambda i: (i, 0))]
    x_p, x_s = x
    assert x_p.shape[0] == n_ptiles * tm and x_s.shape[0] == tm
    return [x_p, x_s], [pl.BlockSpec((tm, x_p.shape[1]), lambda i: (jnp.minimum(i, n_ptiles - 1), 0)),
                        pl.BlockSpec((tm, x_s.shape[1]), lambda i: (0, 0))]


def _token_tile(x_refs, n_ptiles):
    if len(x_refs) == 1:
        return x_refs[0][...]
    return jnp.where(pl.program_id(0) >= n_ptiles, x_refs[1][...], x_refs[0][...])


def _in_proj_kernel(*refs, d_a, d_b, kv_w, d_c, n_x, n_ptiles):
    x_refs = refs[:n_x]
    (g_ref, w_ref, cos_ref, sin_ref,
     lru_ref, sc_ref, gate_ref, q_ref, kvc_ref, kvs_ref, kvw_ref, kvsb_ref, kvwb_ref) = refs[n_x:]
    h_parts = _split_act(_rmsnorm(_token_tile(x_refs, n_ptiles), g_ref[...]), w_ref.shape[0])

    def mm(a, b):
        return _split_dot(h_parts, w_ref, slice(a, b))

    cos = cos_ref[...]
    sin = sin_ref[...]
    off = 2 * d_a
    lru_ref[...] = mm(0, off)
    q = mm(off, off + d_b)
    reps = d_b // LANES
    cos_q = jnp.concatenate([cos] * reps, axis=1)
    sin_q = jnp.concatenate([sin] * reps, axis=1)
    q_ref[...] = (q * cos_q + _swap_halves(q) * sin_q) * (HEAD_DIM ** -0.5)
    off += d_b
    for ref, bref in ((kvc_ref, None), (kvs_ref, kvsb_ref), (kvw_ref, kvwb_ref)):
        kv = mm(off, off + kv_w)
        k = kv[:, :kv_w // 2]
        kv = jnp.concatenate([k * cos + _swap_halves(k) * sin, kv[:, kv_w // 2:]], axis=1)
        ref[...] = kv
        if bref is not None:
            bref[...] = kv.astype(MXU_DTYPE)
        off += kv_w
    sc_ref[...] = mm(off, off + 3 * d_c)
    off += 3 * d_c
    gate_ref[...] = jax.nn.sigmoid(mm(off, off + LANES))


def _in_proj(x, g1, w_r, cos_t, sin_t, n_prompt, seq, *, d_a, d_b, kv_w, d_c):
    tm = TOKEN_TILE
    n_ptiles = n_prompt // tm
    n_stiles = seq // tm
    x_ops, x_specs = _token_specs(x, tm, n_ptiles)
    n, d = sum(a.shape[0] for a in x_ops), x_ops[0].shape[1]

    def tab_map(i):
        return (jnp.where(i < n_ptiles, i % n_stiles, n_stiles), 0)

    row = lambda w: pl.BlockSpec((tm, w), lambda i: (i, 0))
    widths = (2 * d_a, 3 * d_c, LANES, d_b, kv_w, kv_w, kv_w, kv_w, kv_w)
    dtypes = (F32, F32, F32, F32, F32, F32, F32, MXU_DTYPE, MXU_DTYPE)
    return pl.pallas_call(
        functools.partial(_in_proj_kernel, d_a=d_a, d_b=d_b, kv_w=kv_w, d_c=d_c, n_x=len(x_ops), n_ptiles=n_ptiles),
        grid=(n // tm,),
        in_specs=x_specs + [
                  pl.BlockSpec((1, d), lambda i: (0, 0)),
                  pl.BlockSpec(w_r.shape, lambda i: (0, 0, 0)),
                  pl.BlockSpec((tm, LANES), tab_map),
                  pl.BlockSpec((tm, LANES), tab_map)],
        out_specs=[row(w) for w in widths],
        out_shape=[jax.ShapeDtypeStruct((n, w), dt) for w, dt in zip(widths, dtypes)],
        compiler_params=_cparams("parallel"),
        name="in_proj",
    )(*x_ops, g1, w_r, cos_t, sin_t)


def _scan_rows(a, u):
    t = a.shape[0]
    row = lax.broadcasted_iota(jnp.int32, a.shape, 0)
    d = 1
    while d < t:
        keep = row >= d
        a_sh = jnp.where(keep, pltpu.roll(a, d, axis=0), 1.0)
        u_sh = jnp.where(keep, pltpu.roll(u, d, axis=0), 0.0)
        u = a * u_sh + u
        a = a * a_sh
        d *= 2
    return a, u


def _lru_gates(xc, wa_ref, ba_ref, wx_ref, bx_ref, lam_ref):
    xb = xc.astype(MXU_DTYPE)
    r = jax.nn.sigmoid(jnp.dot(xb, wa_ref[...], preferred_element_type=F32) + ba_ref[...])
    i = jax.nn.sigmoid(jnp.dot(xb, wx_ref[...], preferred_element_type=F32) + bx_ref[...])
    log_a = -LRU_C * r * jax.nn.softplus(-lam_ref[...])
    a = jnp.exp(log_a)
    th = jnp.tanh(log_a)
    u = jnp.sqrt(-2.0 * th / (1.0 - th)) * (i * xc)
    return a, u


def _seq_mix_kernel(lru_ref, sc_ref, h0_ref, lb0_ref, sb0_ref, cw_ref, cb_ref, wa_ref, ba_ref, wx_ref, bx_ref,
                    lam_ref, scw_ref, oa_ref, oc_ref, hn_ref, lbn_ref, sbn_ref, xpad, vpad, hcar, *, d_a, d_c):
    c = pl.program_id(1)
    tc = lru_ref.shape[0]
    nlb = lb0_ref.shape[1]
    nsb = sb0_ref.shape[1]

    @pl.when(c == 0)
    def _():
        xpad[0:8, :] = jnp.zeros((8, d_a), F32)
        xpad[8 - nlb:8, :] = lb0_ref[0]
        vpad[0:8, :] = jnp.zeros((8, d_c), F32)
        vpad[8 - nsb:8, :] = sb0_ref[0]
        hcar[...] = h0_ref[0]

    xa = lru_ref[:, 0:d_a]
    ga = lru_ref[:, d_a:2 * d_a]
    xpad[8:8 + tc, :] = xa
    xc = cw_ref[0:1, :] * xpad[8 - nlb:8 - nlb + tc, :]
    for j in range(1, nlb):
        xc = xc + cw_ref[j:j + 1, :] * xpad[8 - nlb + j:8 - nlb + j + tc, :]
    xc = xc + cw_ref[nlb:nlb + 1, :] * xa + cb_ref[...]
    a, u = _lru_gates(xc, wa_ref, ba_ref, wx_ref, bx_ref, lam_ref)
    a_cum, hs = _scan_rows(a, u)
    hs = hs + a_cum * hcar[...]
    hcar[...] = hs[tc - 1:tc, :]
    oa_ref[...] = hs * jax.nn.gelu(ga)

    v = sc_ref[:, 0:d_c] * sc_ref[:, 2 * d_c:3 * d_c]
    vpad[8:8 + tc, :] = v
    uc = scw_ref[0:1, :] * vpad[8 - nsb:8 - nsb + tc, :]
    for j in range(1, nsb):
        uc = uc + scw_ref[j:j + 1, :] * vpad[8 - nsb + j:8 - nsb + j + tc, :]
    uc = uc + scw_ref[nsb:nsb + 1, :] * v
    oc_ref[...] = sc_ref[:, d_c:2 * d_c] * uc

    hn_ref[0] = hs[tc - 1:tc, :]
    lbn_ref[0] = xpad[8 + tc - nlb:8 + tc, :]
    sbn_ref[0] = vpad[8 + tc - nsb:8 + tc, :]
    xpad[0:8, :] = xpad[tc:tc + 8, :]
    vpad[0:8, :] = vpad[tc:tc + 8, :]


def _seq_mix_prompt(lru_in, sc_in, h0, lb0, sb0, wts, bsz, seq, *, d_a, d_c):
    n = bsz * seq
    tc = TIME_CHUNK
    nch = seq // tc
    row = lambda w: pl.BlockSpec((tc, w), lambda b, c: (b * nch + c, 0))
    full = lambda a: pl.BlockSpec(a.shape, lambda b, c: (0,) * a.ndim)
    state = lambda r, w: pl.BlockSpec((1, r, w), lambda b, c: (b, 0, 0))
    nlb, nsb = lb0.shape[1], sb0.shape[1]
    return pl.pallas_call(
        functools.partial(_seq_mix_kernel, d_a=d_a, d_c=d_c),
        grid=(bsz, nch),
        in_specs=[row(2 * d_a), row(3 * d_c), state(1, d_a), state(nlb, d_a), state(nsb, d_c)]
                 + [full(w) for w in wts],
        out_specs=[row(d_a), row(d_c), state(1, d_a), state(nlb, d_a), state(nsb, d_c)],
        out_shape=[jax.ShapeDtypeStruct((n, d_a), F32), jax.ShapeDtypeStruct((n, d_c), F32),
                   jax.ShapeDtypeStruct((bsz, 1, d_a), F32), jax.ShapeDtypeStruct((bsz, nlb, d_a), F32),
                   jax.ShapeDtypeStruct((bsz, nsb, d_c), F32)],
        scratch_shapes=[pltpu.VMEM((tc + 8, d_a), F32), pltpu.VMEM((tc + 8, d_c), F32), pltpu.VMEM((1, d_a), F32)],
        compiler_params=_cparams("arbitrary", "arbitrary"),
        name="seq_mix_prompt",
    )(lru_in, sc_in, h0, lb0, sb0, *wts)


def _seq_mix_sample_kernel(lru_ref, sc_ref, h0_ref, lb0_ref, sb0_ref, cw_ref, cb_ref, wa_ref, ba_ref, wx_ref,
                           bx_ref, lam_ref, scw_ref, oa_ref, oc_ref, hn_ref, lbn_ref, sbn_ref, *, d_a, d_c):
    t_len = lru_ref.shape[0]
    nlb = lb0_ref.shape[0]
    nsb = sb0_ref.shape[0]
    xs = [lb0_ref[j] for j in range(nlb)] + [lru_ref[t][:, 0:d_a] for t in range(t_len)]
    vs = [sb0_ref[j] for j in range(nsb)] + [sc_ref[t][:, 0:d_c] * sc_ref[t][:, 2 * d_c:3 * d_c] for t in range(t_len)]
    h = h0_ref[...]
    for t in range(t_len):
        xc = cw_ref[0:1, :] * xs[t]
        for j in range(1, nlb + 1):
            xc = xc + cw_ref[j:j + 1, :] * xs[t + j]
        xc = xc + cb_ref[...]
        a, u = _lru_gates(xc, wa_ref, ba_ref, wx_ref, bx_ref, lam_ref)
        h = a * h + u
        oa_ref[t] = h * jax.nn.gelu(lru_ref[t][:, d_a:2 * d_a])
        uc = scw_ref[0:1, :] * vs[t]
        for j in range(1, nsb + 1):
            uc = uc + scw_ref[j:j + 1, :] * vs[t + j]
        oc_ref[t] = sc_ref[t][:, d_c:2 * d_c] * uc
    hn_ref[...] = h
    for j in range(nlb):
        lbn_ref[j] = xs[t_len + j]
    for j in range(nsb):
        sbn_ref[j] = vs[t_len + j]


def _seq_mix_sample(lru_t, sc_t, h0, lb0_t, sb0_t, wts, *, d_a, d_c):
    t_len, bsz, _ = lru_t.shape
    outs = [jax.ShapeDtypeStruct((t_len, bsz, d_a), F32), jax.ShapeDtypeStruct((t_len, bsz, d_c), F32),
            jax.ShapeDtypeStruct(h0.shape, F32), jax.ShapeDtypeStruct(lb0_t.shape, F32),
            jax.ShapeDtypeStruct(sb0_t.shape, F32)]
    return pl.pallas_call(
        functools.partial(_seq_mix_sample_kernel, d_a=d_a, d_c=d_c),
        out_shape=outs,
        compiler_params=pltpu.CompilerParams(vmem_limit_bytes=VMEM_LIMIT),
        name="seq_mix_sample",
    )(lru_t, sc_t, h0, lb0_t, sb0_t, *wts)


def _compress_kernel(kvc_ref, w_ref, out_ref):
    half = out_ref.shape[0] // 2
    x = kvc_ref[...].reshape(half, 2 * CMP_BLOCK, out_ref.shape[1])
    w = w_ref[...][None]
    out_ref[0:half, :] = jnp.sum(x[:, 0:CMP_BLOCK, :] * w, axis=1)
    out_ref[half:2 * half, :] = jnp.sum(x[:, CMP_BLOCK:2 * CMP_BLOCK, :] * w, axis=1)


def _compress_prompt(kvc, w_cmp, bsz, seq):
    kv_w = kvc.shape[1]
    n_cmp = seq // CMP_BLOCK
    return pl.pallas_call(
        _compress_kernel,
        grid=(bsz,),
        in_specs=[pl.BlockSpec((seq, kv_w), lambda b: (b, 0)), pl.BlockSpec(w_cmp.shape, lambda b: (0, 0))],
        out_specs=pl.BlockSpec((n_cmp, kv_w), lambda b: (b, 0)),
        out_shape=jax.ShapeDtypeStruct((bsz * n_cmp, kv_w), F32),
        compiler_params=_cparams("parallel"),
        name="nsa_compress",
    )(kvc, w_cmp)


def _pad_heads(q, lhs_ref, rows):
    lane = lax.broadcasted_iota(jnp.int32, (rows, LANES), 1)
    for h in range(N_HEADS):
        k = h // GROUP
        slab = q[:, (h // 2) * LANES:(h // 2 + 1) * LANES]
        if h % 2 != k:
            slab = pltpu.roll(slab, HEAD_DIM, axis=1)
        keep = (lane >= k * HEAD_DIM) & (lane < (k + 1) * HEAD_DIM)
        lhs_ref[h * rows:(h + 1) * rows, 0:LANES] = jnp.where(keep, slab, 0.0).astype(MXU_DTYPE)


def _gated_output(gate, o_c, o_s, o_w, rows):
    lane = lax.broadcasted_iota(jnp.int32, (rows, LANES), 1)
    slabs = []
    for m in range(N_HEADS // 2):
        parts = []
        for h in (2 * m, 2 * m + 1):
            k = h // GROUP
            sl = slice(h * rows, (h + 1) * rows)
            r = (gate[:, 3 * h:3 * h + 1] * o_c[sl] + gate[:, 3 * h + 1:3 * h + 2] * o_s[sl]
                 + gate[:, 3 * h + 2:3 * h + 3] * o_w[sl])
            if h % 2 != k:
                r = pltpu.roll(r, HEAD_DIM, axis=1)
            parts.append(r)
        slabs.append(jnp.where(lane < HEAD_DIM, parts[0], parts[1]))
    return slabs


def _compressed_branch(qpad, kcv, pos0, rows, n_sel):
    n_cmp = kcv.shape[0]
    half = n_cmp // 2
    pos_q = pos0 + lax.broadcasted_iota(jnp.int32, (rows, 1), 0)
    kc = kcv[:, 0:LANES].astype(MXU_DTYPE)
    vc = kcv[:, LANES:2 * LANES].astype(MXU_DTYPE)
    s_c = lax.dot_general(qpad, kc, _NT, preferred_element_type=F32)
    col = lax.broadcasted_iota(jnp.int32, (rows, n_cmp), 1)
    blk = jnp.where(col < half, 2 * col, 2 * (col - half) + 1)
    m_c = (blk + 1) * CMP_BLOCK - 1 <= pos_q
    ps = []
    imp = [jnp.zeros((rows, n_cmp), F32) for _ in range(N_KV)]
    for h in range(N_HEADS):
        p = _masked_softmax(s_c[h * rows:(h + 1) * rows], m_c)
        imp[h // GROUP] = imp[h // GROUP] + p
        ps.append(p.astype(MXU_DTYPE))
    o_c = jnp.dot(jnp.concatenate(ps, axis=0), vc, preferred_element_type=F32)
    imps = []
    for k in range(N_KV):
        imp_s = imp[k][:, 0:half] + imp[k][:, half:n_cmp]
        if half < LANES:
            imp_s = jnp.concatenate([imp_s, jnp.zeros((rows, LANES - half), F32)], axis=1)
        imps.append(imp_s)
    imp_t = jnp.concatenate(imps + [jnp.zeros((LANES - N_KV * rows, LANES), F32)], axis=0).T
    n_rows = -(-n_sel // 8) * 8
    pos_l = pos0 + lax.broadcasted_iota(jnp.int32, (1, LANES), 1) % rows
    sel_t = _select_blocks_keys(imp_t[0:n_rows], pos_l, n_sel)
    sel = jnp.concatenate([sel_t, jnp.zeros((LANES - n_rows, LANES), F32)], axis=0).T
    return o_c, [sel[k * rows:(k + 1) * rows] for k in range(N_KV)]


def _store_selection(lhs_ref, sels, rows):
    for k in range(N_KV):
        neg = jnp.where(sels[k] > 0.0, 0.0, NEG_INF).astype(MXU_DTYPE)
        for g in range(GROUP):
            h = k * GROUP + g
            lhs_ref[h * rows:(h + 1) * rows, LANES:2 * LANES] = neg


def _block_onehot(n_rows):
    row = lax.broadcasted_iota(jnp.int32, (n_rows, LANES), 0)
    lane = lax.broadcasted_iota(jnp.int32, (n_rows, LANES), 1)
    return jnp.where(row // SEL_BLOCK == lane, 1.0, 0.0).astype(MXU_DTYPE)


def _masked_softmax_keys(s, mask):
    s = jnp.where(mask, s, NEG_INF)
    m = jnp.max(s, axis=0, keepdims=True)
    p = jnp.where(mask, jnp.exp(s - m), 0.0)
    return p / jnp.maximum(jnp.sum(p, axis=0, keepdims=True), TINY)


def _nsa_prompt_kernel(q_ref, gate_ref, kcv_ref, kvs_ref, kvw_ref, out_ref,
                       kaug, vst, vwt, vct, lhs, m_sc, acc_sc, s_a, s_b, *, seq):
    qb = pl.program_id(1)
    rows = Q_BLOCK
    tk = KEY_TILE
    s0 = qb * rows
    n_sel = seq // SEL_BLOCK
    n_cmp = seq // CMP_BLOCK
    cols = N_HEADS * rows

    @pl.when(qb == 0)
    def _():
        kaug[:, 0:LANES] = kvs_ref[:, 0:LANES]
        kaug[:, LANES:2 * LANES] = _block_onehot(seq)
        vct[...] = kcv_ref[:, LANES:2 * LANES].T.astype(MXU_DTYPE)

        def tr_sel(i, c):
            r0 = pl.multiple_of(i * tk, tk)
            ones_row = jnp.where(lax.broadcasted_iota(jnp.int32, (SUM_ROWS, tk), 0) == 0, 1.0, 0.0)
            v_t = kvs_ref[pl.ds(r0, tk), LANES:2 * LANES].astype(F32).T
            vst[i] = jnp.concatenate([v_t, ones_row], axis=0).astype(MXU_DTYPE)
            return c

        def tr_win(i, c):
            r0 = pl.multiple_of(i * rows, rows)
            vwt[i] = kvw_ref[pl.ds(r0, rows), LANES:2 * LANES].astype(F32).T.astype(MXU_DTYPE)
            return c

        lax.fori_loop(0, seq // tk, tr_sel, 0)
        lax.fori_loop(0, seq // rows, tr_win, 0)

    _pad_heads(q_ref[...], lhs, rows)
    qpad = lhs[:, 0:LANES]
    lane = lax.broadcasted_iota(jnp.int32, (1, cols), 1)
    pos_q = s0 + lane % rows

    half = n_cmp // 2
    s_c = lax.dot_general(kcv_ref[:, 0:LANES].astype(MXU_DTYPE), qpad, _NT, preferred_element_type=F32)
    r_c = lax.broadcasted_iota(jnp.int32, (n_cmp, cols), 0)
    blk_c = jnp.where(r_c < half, 2 * r_c, 2 * (r_c - half) + 1)
    p_c = _masked_softmax_keys(s_c, (blk_c + 1) * CMP_BLOCK - 1 <= pos_q)
    o_c = jnp.dot(vct[...], p_c.astype(MXU_DTYPE), preferred_element_type=F32)

    imp = []
    for k in range(N_KV):
        acc = p_c[:, k * GROUP * rows:(k * GROUP + 1) * rows]
        for g in range(1, GROUP):
            acc = acc + p_c[:, (k * GROUP + g) * rows:(k * GROUP + g + 1) * rows]
        imp.append(acc[0:half] + acc[half:n_cmp])
    sel = _select_blocks_keys(jnp.concatenate(imp, axis=1), pos_q[:, 0:N_KV * rows], n_sel)
    for k in range(N_KV):
        sel_k = sel[:, k * rows:(k + 1) * rows]
        if n_sel < LANES:
            sel_k = jnp.concatenate([sel_k, jnp.zeros((LANES - n_sel, rows), F32)], axis=0)
        neg = jnp.where(sel_k.T > 0.0, 0.0, NEG_INF).astype(MXU_DTYPE)
        for g in range(GROUP):
            h = k * GROUP + g
            lhs[h * rows:(h + 1) * rows, LANES:2 * LANES] = neg

    m_sc[...] = jnp.full(m_sc.shape, NEG_INF, F32)
    acc_sc[...] = jnp.zeros(acc_sc.shape, F32)

    last_tile = seq // tk - 1

    def scores(s_ref, kt):
        k0 = pl.multiple_of(jnp.minimum(kt, last_tile) * tk, tk)
        s_ref[...] = lax.dot_general(kaug[pl.ds(k0, tk), :], lhs[...], _NT, preferred_element_type=F32)

    def update(s_ref, kt, causal):
        s = s_ref[...]
        if causal:
            s = jnp.where(kt * tk + lax.broadcasted_iota(jnp.int32, (tk, cols), 0) <= pos_q, s, NEG_INF)
        m_old = m_sc[...]
        m_new = jnp.maximum(m_old, jnp.max(s, axis=0, keepdims=True))
        alpha = jnp.exp(m_old - m_new)
        p = jnp.exp(s - m_new)
        m_sc[...] = m_new
        vt = vst[jnp.minimum(kt, last_tile)]
        acc_sc[...] = alpha * acc_sc[...] + jnp.dot(vt, p.astype(MXU_DTYPE), preferred_element_type=F32)

    n_pairs = (s0 // tk) // 2
    scores(s_a, 0)

    def pair_body(i, carry):
        scores(s_b, 2 * i + 1)
        update(s_a, 2 * i, False)
        scores(s_a, 2 * i + 2)
        update(s_b, 2 * i + 1, False)
        return carry

    lax.fori_loop(0, n_pairs, pair_body, 0)
    t0 = 2 * n_pairs
    scores(s_b, t0 + 1)
    update(s_a, t0, True)

    @pl.when((t0 + 1) * tk <= s0 + rows - 1)
    def _():
        update(s_b, t0 + 1, True)

    o_s = acc_sc[0:LANES, :] / jnp.maximum(acc_sc[LANES:LANES + 1, :], TINY)

    band = WINDOW + rows
    start = pl.multiple_of(jnp.maximum(s0 - WINDOW, 0), rows)
    s_w = lax.dot_general(kvw_ref[pl.ds(start, band), 0:LANES], qpad, _NT, preferred_element_type=F32)
    dlt = pos_q - (start + lax.broadcasted_iota(jnp.int32, (band, cols), 0))
    p_w = _masked_softmax_keys(s_w, (dlt >= 0) & (dlt <= WINDOW))
    t0 = start // rows
    vw = jnp.concatenate([vwt[t0 + j] for j in range(band // rows)], axis=1)
    o_w = jnp.dot(vw, p_w.astype(MXU_DTYPE), preferred_element_type=F32)

    g_t = gate_ref[...].T
    for m in range(N_HEADS // 2):
        parts = []
        for h in (2 * m, 2 * m + 1):
            k = h // GROUP
            rs = slice(k * HEAD_DIM, (k + 1) * HEAD_DIM)
            ls = slice(h * rows, (h + 1) * rows)
            parts.append(g_t[3 * h:3 * h + 1] * o_c[rs, ls] + g_t[3 * h + 1:3 * h + 2] * o_s[rs, ls]
                         + g_t[3 * h + 2:3 * h + 3] * o_w[rs, ls])
        out_ref[:, m * LANES:(m + 1) * LANES] = jnp.concatenate(parts, axis=0).T


def _nsa_prompt(q, gate, kcv, kvs_b, kvw_b, bsz, seq):
    n, d_b = bsz * seq, q.shape[1]
    kv_w = kvs_b.shape[1]
    nq = seq // Q_BLOCK
    n_cmp = seq // CMP_BLOCK
    rows = N_HEADS * Q_BLOCK
    return pl.pallas_call(
        functools.partial(_nsa_prompt_kernel, seq=seq),
        grid=(bsz, nq),
        in_specs=[pl.BlockSpec((Q_BLOCK, d_b), lambda b, i: (b * nq + i, 0)),
                  pl.BlockSpec((Q_BLOCK, LANES), lambda b, i: (b * nq + i, 0)),
                  pl.BlockSpec((n_cmp, kv_w), lambda b, i: (b, 0)),
                  pl.BlockSpec((seq, kv_w), lambda b, i: (b, 0)),
                  pl.BlockSpec((seq, kv_w), lambda b, i: (b, 0))],
        out_specs=pl.BlockSpec((Q_BLOCK, d_b), lambda b, i: (b * nq + i, 0)),
        out_shape=jax.ShapeDtypeStruct((n, d_b), F32),
        scratch_shapes=[pltpu.VMEM((seq, 2 * LANES), MXU_DTYPE),
                        pltpu.VMEM((seq // KEY_TILE, LANES + SUM_ROWS, KEY_TILE), MXU_DTYPE),
                        pltpu.VMEM((seq // Q_BLOCK, LANES, Q_BLOCK), MXU_DTYPE),
                        pltpu.VMEM((LANES, n_cmp), MXU_DTYPE),
                        pltpu.VMEM((rows, 2 * LANES), MXU_DTYPE),
                        pltpu.VMEM((1, rows), F32), pltpu.VMEM((LANES + SUM_ROWS, rows), F32),
                        pltpu.VMEM((KEY_TILE, rows), F32), pltpu.VMEM((KEY_TILE, rows), F32)],
        compiler_params=_cparams("arbitrary", "arbitrary"),
        name="nsa_prompt",
    )(q, gate, kcv, kvs_b, kvw_b)


def _nsa_sample_kernel(pt_ref, q_ref, gate_ref, ksn_ref, kwn_ref, wcmp_ref, win_ref, *rest,
                       n_pages, page, past, t_len):
    cmp_pages = rest[:n_pages]
    sel_pages = rest[n_pages:2 * n_pages]
    out_ref, nwin_ref, kaug, vsel, kcv, lhs = rest[2 * n_pages:]
    rows = 8
    n_cmp = past // CMP_BLOCK
    n_sel = pl.cdiv(past + t_len, SEL_BLOCK)
    per_page = page // CMP_BLOCK

    @pl.when(pl.program_id(0) == 0)
    def _():
        kaug[:, LANES:2 * LANES] = _block_onehot(past)

    q = jnp.concatenate([q_ref[0], jnp.zeros((rows - t_len, q_ref.shape[2]), F32)], axis=0)
    _pad_heads(q, lhs, rows)
    qpad = lhs[:, 0:LANES]
    tq = lax.broadcasted_iota(jnp.int32, (rows, 1), 0)
    pos_q = past + tq

    wrep = jnp.concatenate([wcmp_ref[...]] * per_page, axis=0)
    for p in range(n_pages):
        x = cmp_pages[p][...] * wrep
        for j in range(per_page):
            blk = p * per_page + j
            dst = (blk % 2) * (n_cmp // 2) + blk // 2
            kcv[dst:dst + 1, :] = jnp.sum(x[j * CMP_BLOCK:(j + 1) * CMP_BLOCK], axis=0, keepdims=True)
        kaug[p * page:(p + 1) * page, 0:LANES] = sel_pages[p][:, 0:LANES].astype(MXU_DTYPE)
        vsel[p * page:(p + 1) * page, :] = sel_pages[p][:, LANES:2 * LANES].astype(MXU_DTYPE)

    o_c, sels = _compressed_branch(qpad, kcv[...], past, rows, n_sel)
    _store_selection(lhs, sels, rows)

    def new_rows(ref):
        kv = jnp.concatenate([ref[0], jnp.zeros((LANES - t_len, ref.shape[2]), F32)], axis=0)
        return kv[:, 0:LANES].astype(MXU_DTYPE), kv[:, LANES:2 * LANES].astype(MXU_DTYPE)

    tk_new = lax.broadcasted_iota(jnp.int32, (rows, LANES), 1)
    m_new = (tk_new < t_len) & (tk_new <= tq)

    def joint_attention(s_past, mask_past, v_past, s_new, v_new):
        outs = []
        p_past, p_new = [], []
        for h in range(N_HEADS):
            sl = slice(h * rows, (h + 1) * rows)
            sp = s_past[sl] if mask_past is None else jnp.where(mask_past, s_past[sl], NEG_INF)
            sn = jnp.where(m_new, s_new[sl], NEG_INF)
            m = jnp.maximum(jnp.max(sp, axis=-1, keepdims=True), jnp.max(sn, axis=-1, keepdims=True))
            pp = jnp.exp(sp - m) if mask_past is None else jnp.where(mask_past, jnp.exp(sp - m), 0.0)
            pn = jnp.where(m_new, jnp.exp(sn - m), 0.0)
            den = jnp.maximum(jnp.sum(pp, axis=-1, keepdims=True) + jnp.sum(pn, axis=-1, keepdims=True), TINY)
            p_past.append((pp / den).astype(MXU_DTYPE))
            p_new.append((pn / den).astype(MXU_DTYPE))
        return (jnp.dot(jnp.concatenate(p_past, axis=0), v_past, preferred_element_type=F32)
                + jnp.dot(jnp.concatenate(p_new, axis=0), v_new, preferred_element_type=F32))

    kn, vn = new_rows(ksn_ref)
    s_past = lax.dot_general(lhs[...], kaug[...], _NT, preferred_element_type=F32)
    s_new = lax.dot_general(qpad, kn, _NT, preferred_element_type=F32)
    o_s = joint_attention(s_past, None, vsel[...], s_new, vn)

    n_buf = win_ref.shape[0]
    kwn, vwn = new_rows(kwn_ref)
    kwb = win_ref[:, 0:LANES].astype(MXU_DTYPE)
    vwb = win_ref[:, LANES:2 * LANES].astype(MXU_DTYPE)
    s_wb = lax.dot_general(qpad, kwb, _NT, preferred_element_type=F32)
    s_wn = lax.dot_general(qpad, kwn, _NT, preferred_element_type=F32)
    pos_w = past - n_buf + lax.broadcasted_iota(jnp.int32, (rows, n_buf), 1)
    dlt = pos_q - pos_w
    m_wb = (dlt >= 0) & (dlt <= WINDOW) & (pos_w >= 0)
    o_w = joint_attention(s_wb, m_wb, vwb, s_wn, vwn)

    for m, slab in enumerate(_gated_output(
            jnp.concatenate([gate_ref[0], jnp.zeros((rows - t_len, LANES), F32)], axis=0), o_c, o_s, o_w, rows)):
        out_ref[0, :, m * LANES:(m + 1) * LANES] = slab[0:t_len]

    nwin_ref[0:n_buf - t_len, :] = win_ref[t_len:n_buf, :]
    nwin_ref[n_buf - t_len:n_buf, :] = kwn_ref[0]


def _nsa_sample(page_table, q, gate, kvs_new, kvw_new, w_cmp, pool_c, pool_s, win, layer):
    dbs, t_len, d_b = q.shape
    n_pages = page_table.shape[1]
    page, kv_w = pool_c.shape[2], pool_c.shape[3]
    n_buf = win.shape[2]
    past = n_pages * page
    tok = lambda w: pl.BlockSpec((1, t_len, w), lambda b, pt: (b, 0, 0))

    def page_spec(p):
        return pl.BlockSpec((None, None, page, kv_w), lambda b, pt, p=p: (layer, pt[b * n_pages + p], 0, 0))

    grid_spec = pltpu.PrefetchScalarGridSpec(
        num_scalar_prefetch=1,
        grid=(dbs,),
        in_specs=[tok(d_b), tok(LANES), tok(kv_w), tok(kv_w),
                  pl.BlockSpec(w_cmp.shape, lambda b, pt: (0, 0)),
                  pl.BlockSpec((None, None, n_buf, kv_w), lambda b, pt: (layer, b, 0, 0))]
                 + [page_spec(p) for p in range(n_pages)] * 2,
        out_specs=[tok(d_b), pl.BlockSpec((None, n_buf, kv_w), lambda b, pt: (b, 0, 0))],
        scratch_shapes=[pltpu.VMEM((past, 2 * LANES), MXU_DTYPE), pltpu.VMEM((past, LANES), MXU_DTYPE),
                        pltpu.VMEM((past // CMP_BLOCK, kv_w), F32), pltpu.VMEM((N_HEADS * 8, 2 * LANES), MXU_DTYPE)],
    )
    return pl.pallas_call(
        functools.partial(_nsa_sample_kernel, n_pages=n_pages, page=page, past=past, t_len=t_len),
        grid_spec=grid_spec,
        out_shape=[jax.ShapeDtypeStruct((dbs, t_len, d_b), F32), jax.ShapeDtypeStruct((dbs, n_buf, kv_w), F32)],
        compiler_params=_cparams("arbitrary"),
        name="nsa_sample",
    )(page_table.reshape(-1), q, gate, kvs_new, kvw_new, w_cmp, win, *([pool_c] * n_pages), *([pool_s] * n_pages))


def _out_proj_router_kernel(*refs, n_x, n_ptiles):
    x_refs = refs[:n_x]
    (oa_ref, ob_ref, oc_ref, oas_ref, obs_ref, ocs_ref, wo_ref, g2_ref, wr_ref, br_ref,
     xn_ref, h2_ref, ri_ref, rw_ref, cnt_ref, run_ref) = refs[n_x:]
    tm = oa_ref.shape[0]

    @pl.when(pl.program_id(0) == 0)
    def _():
        run_ref[...] = jnp.zeros(run_ref.shape, F32)

    is_sample = pl.program_id(0) >= n_ptiles
    mix = jnp.concatenate([jnp.where(is_sample, oas_ref[...], oa_ref[...]),
                           jnp.where(is_sample, obs_ref[...], ob_ref[...]),
                           jnp.where(is_sample, ocs_ref[...], oc_ref[...])], axis=1)
    xn = _token_tile(x_refs, n_ptiles) + _split_dot(_split_act(mix, wo_ref.shape[0]), wo_ref)
    xn_ref[...] = xn
    h2 = _rmsnorm(xn, g2_ref[...])
    _store_token_tiles(h2_ref, h2)
    logits = _split_dot(_split_act(h2, wr_ref.shape[0]), wr_ref) + br_ref[...]

    lane = lax.broadcasted_iota(jnp.int32, (tm, LANES), 1)
    lanef = lane.astype(F32)

    def softmax_over(mask):
        m = jnp.max(jnp.where(mask, logits, -jnp.inf), axis=-1, keepdims=True)
        e = jnp.where(mask, jnp.exp(logits - m), 0.0)
        return e / jnp.sum(e, axis=-1, keepdims=True)

    def first_max(p, mask):
        pm = jnp.max(jnp.where(mask, p, -1.0), axis=-1, keepdims=True)
        idx = jnp.min(jnp.where(mask & (p == pm), lanef, 1e9), axis=-1, keepdims=True)
        return pm, idx

    is_g = lane < N_GROUPS
    g_wt, g_sel = first_max(softmax_over(is_g), is_g)
    lo = N_GROUPS + EXP_PER_GROUP * g_sel
    in_e = (lanef >= lo) & (lanef < lo + EXP_PER_GROUP)
    p_e = softmax_over(in_e)
    p0, i0 = first_max(p_e, in_e)
    p1, i1 = first_max(p_e, in_e & (lanef != i0))
    den = p0 + p1
    e0 = i0 - N_GROUPS
    e1 = i1 - N_GROUPS

    hit0 = lanef == e0
    hit1 = lanef == e1
    onehot = jnp.where(hit0 | hit1, 1.0, 0.0)
    r_i = lax.broadcasted_iota(jnp.int32, (tm, tm), 0)
    c_i = lax.broadcasted_iota(jnp.int32, (tm, tm), 1)
    ltri = jnp.where(c_i < r_i, 1.0, 0.0).astype(jnp.bfloat16)
    before = jnp.dot(ltri, onehot.astype(jnp.bfloat16), preferred_element_type=F32) + run_ref[0:1, :]
    r0 = jnp.sum(jnp.where(hit0, before, 0.0), axis=-1, keepdims=True)
    r1 = jnp.sum(jnp.where(hit1, before, 0.0), axis=-1, keepdims=True)
    run_ref[...] = run_ref[...] + jnp.sum(onehot, axis=0, keepdims=True)
    cnt_ref[...] = run_ref[...]

    ri = jnp.where(lane == 0, e0, jnp.where(lane == 1, e1, jnp.where(lane == 2, r0, jnp.where(lane == 3, r1, 0.0))))
    ri_ref[...] = ri.T[0:8].astype(jnp.int32)
    rw_ref[...] = jnp.where(lane == 0, g_wt * p0 / den, jnp.where(lane == 1, g_wt * p1 / den, 0.0))


def _out_proj_router(x, mix_p, mix_s, w_out, g2, w_route, b_route):
    tm = TOKEN_TILE
    n_ptiles = mix_p[0].shape[0] // tm
    x_ops, x_specs = _token_specs(x, tm, n_ptiles)
    n, d = sum(a.shape[0] for a in x_ops), x_ops[0].shape[1]
    assert all(a.shape[0] == tm for a in mix_s) and n == (n_ptiles + 1) * tm
    row = lambda w: pl.BlockSpec((tm, w), lambda i: (i, 0))
    prow = lambda a: pl.BlockSpec((tm, a.shape[1]), lambda i: (jnp.minimum(i, n_ptiles - 1), 0))
    full = lambda a: pl.BlockSpec(a.shape, lambda i: (0,) * a.ndim)
    return pl.pallas_call(
        functools.partial(_out_proj_router_kernel, n_x=len(x_ops), n_ptiles=n_ptiles),
        grid=(n // tm,),
        in_specs=x_specs + [prow(a) for a in mix_p] + [full(a) for a in mix_s]
                 + [full(w_out), full(g2), full(w_route), full(b_route)],
        out_specs=[row(d), pl.BlockSpec((tm * SUBLANES, LANES), lambda i: (i, 0)),
                   pl.BlockSpec((8, tm), lambda i: (0, i)), row(LANES), pl.BlockSpec((8, LANES), lambda i: (0, 0))],
        out_shape=[jax.ShapeDtypeStruct((n, d), F32), jax.ShapeDtypeStruct((n * SUBLANES, LANES), F32),
                   jax.ShapeDtypeStruct((8, n), jnp.int32), jax.ShapeDtypeStruct((n, LANES), F32),
                   jax.ShapeDtypeStruct((8, LANES), F32)],
        scratch_shapes=[pltpu.VMEM((8, LANES), F32)],
        compiler_params=_cparams("arbitrary"),
        name="out_proj_router",
    )(*x_ops, *mix_p, *mix_s, w_out, g2, w_route, b_route)


def _store_token_tiles(ref, x):
    t = x.shape[0]
    for c in range(SUBLANES):
        ref[pl.ds(c, t, stride=SUBLANES), :] = x[:, c * LANES:(c + 1) * LANES]


def _load_token_tiles(ref, first_row, t):
    return jnp.concatenate([ref[pl.ds(first_row + c, t, stride=SUBLANES), :] for c in range(SUBLANES)], axis=1)


def _row_copy(src_hbm, row, dst, slot, r, sem):
    return pltpu.make_async_copy(src_hbm.at[pl.ds(pl.multiple_of(row * SUBLANES, SUBLANES), SUBLANES)],
                                 dst.at[slot, pl.ds(pl.multiple_of(r * SUBLANES, SUBLANES), SUBLANES)],
                                 sem.at[slot])


def _expert_kernel(te_ref, nv_ref, src_ref, h_hbm, wg_ref, wu_ref, wd_ref, ys_ref, xbuf, sem):
    t = pl.program_id(0)
    nv = nv_ref[0]
    te = xbuf.shape[1] // SUBLANES

    def issue(tile, slot):
        def body(i, carry):
            for j in range(2):
                r = 2 * i + j
                _row_copy(h_hbm, src_ref[tile * te + r], xbuf, slot, r, sem).start(priority=j)
            return carry
        lax.fori_loop(0, te // 2, body, 0, unroll=4)

    def wait_all(slot):
        pltpu.make_async_copy(h_hbm.at[pl.ds(0, te * SUBLANES)], xbuf.at[slot], sem.at[slot]).wait()

    @pl.when((t == 0) & (nv > 0))
    def _():
        issue(0, 0)

    @pl.when(t + 1 < nv)
    def _():
        issue(t + 1, (t + 1) % 2)

    @pl.when(t < nv)
    def _():
        slot = t % 2
        wait_all(slot)
        x = _load_token_tiles(xbuf.at[slot], 0, te).astype(MXU_DTYPE)
        hg = jnp.dot(x, wg_ref[...].astype(MXU_DTYPE), preferred_element_type=F32)
        hu = jnp.dot(x, wu_ref[...].astype(MXU_DTYPE), preferred_element_type=F32)
        act = (jax.nn.silu(hg) * hu).astype(MXU_DTYPE)
        _store_token_tiles(ys_ref, jnp.dot(act, wd_ref[...].astype(MXU_DTYPE), preferred_element_type=F32))

    @pl.when(t >= nv)
    def _():
        ys_ref[...] = jnp.zeros(ys_ref.shape, F32)


def _experts(tile_expert, n_valid, src_tok, h2, w_gate, w_up, w_down, layer):
    n_tiles = tile_expert.shape[0]
    te = EXPERT_TILE
    d = SUBLANES * LANES
    d_e = w_gate.shape[3]
    assert w_gate.shape[2] == d and h2.shape[1] == LANES
    wspec = lambda r, c: pl.BlockSpec((None, None, r, c), lambda t, te_r, nv_r, src_r: (layer, te_r[t], 0, 0))
    grid_spec = pltpu.PrefetchScalarGridSpec(
        num_scalar_prefetch=3,
        grid=(n_tiles,),
        in_specs=[pl.BlockSpec(memory_space=pl.ANY), wspec(d, d_e), wspec(d, d_e), wspec(d_e, d)],
        out_specs=pl.BlockSpec((te * SUBLANES, LANES), lambda t, te_r, nv_r, src_r: (t, 0)),
        scratch_shapes=[pltpu.VMEM((2, te * SUBLANES, LANES), F32), pltpu.SemaphoreType.DMA((2,))],
    )
    return pl.pallas_call(
        _expert_kernel,
        grid_spec=grid_spec,
        out_shape=jax.ShapeDtypeStruct((n_tiles * te * SUBLANES, LANES), F32),
        compiler_params=_cparams("arbitrary"),
        name="moe_experts",
    )(tile_expert, n_valid, src_tok, h2, w_gate, w_up, w_down)


def _combine_kernel(d0_ref, d1_ref, x_ref, rw_ref, gf_ref, ys_hbm, *rest, n_ptiles):
    final = n_ptiles is not None
    if final:
        yp_ref, ysm_ref, buf, sem = rest
    else:
        xo_ref, buf, sem = rest
    t = pl.program_id(0)
    nt = pl.num_programs(0)
    tm = x_ref.shape[0]

    def issue(tile, slot):
        def body(r, carry):
            _row_copy(ys_hbm, d0_ref[tile * tm + r], buf, slot, r, sem).start(priority=0)
            _row_copy(ys_hbm, d1_ref[tile * tm + r], buf, slot, tm + r, sem).start(priority=1)
            return carry
        lax.fori_loop(0, tm, body, 0, unroll=8)

    def wait_all(slot):
        pltpu.make_async_copy(ys_hbm.at[pl.ds(0, 2 * tm * SUBLANES)], buf.at[slot], sem.at[slot]).wait()

    @pl.when(t == 0)
    def _():
        issue(0, 0)

    @pl.when(t + 1 < nt)
    def _():
        issue(t + 1, (t + 1) % 2)

    slot = t % 2
    wait_all(slot)
    rw = rw_ref[...]
    xo = (x_ref[...] + rw[:, 0:1] * _load_token_tiles(buf.at[slot], 0, tm)
          + rw[:, 1:2] * _load_token_tiles(buf.at[slot], tm * SUBLANES, tm))
    if not final:
        xo_ref[...] = xo
    else:
        y = _rmsnorm(xo, gf_ref[...])

        @pl.when(t < n_ptiles)
        def _():
            yp_ref[...] = y

        @pl.when(t >= n_ptiles)
        def _():
            ysm_ref[...] = y


def _combine(d0, d1, x, rw, gf, ys, n_prompt=None):
    n, d = x.shape
    tm = COMBINE_TILE
    row = lambda w: pl.BlockSpec((tm, w), lambda t, a, b: (t, 0))
    if n_prompt is None:
        n_ptiles = None
        out_specs = [row(d)]
        out_shape = [jax.ShapeDtypeStruct((n, d), F32)]
    else:
        n_ptiles = n_prompt // tm
        out_specs = [pl.BlockSpec((tm, d), lambda t, a, b: (jnp.minimum(t, n_ptiles - 1), 0)),
                     pl.BlockSpec((tm, d), lambda t, a, b: (jnp.maximum(t - n_ptiles, 0), 0))]
        out_shape = [jax.ShapeDtypeStruct((n_prompt, d), F32), jax.ShapeDtypeStruct((n - n_prompt, d), F32)]
    grid_spec = pltpu.PrefetchScalarGridSpec(
        num_scalar_prefetch=2,
        grid=(n // tm,),
        in_specs=[row(d), row(LANES), pl.BlockSpec((1, d), lambda t, a, b: (0, 0)), pl.BlockSpec(memory_space=pl.ANY)],
        out_specs=out_specs,
        scratch_shapes=[pltpu.VMEM((2, 2 * tm * SUBLANES, LANES), F32), pltpu.SemaphoreType.DMA((2,))],
    )
    return pl.pallas_call(
        functools.partial(_combine_kernel, n_ptiles=n_ptiles),
        grid_spec=grid_spec,
        out_shape=out_shape,
        compiler_params=_cparams("arbitrary"),
        name="moe_combine" if n_prompt is None else "moe_combine_final",
    )(d0, d1, x, rw, gf, ys)


def _route_plan(ri, cnt, n_tiles):
    te = EXPERT_TILE
    n = ri.shape[1]
    counts = cnt[0, :N_EXPERTS].astype(jnp.int32)
    padded = ((counts + te - 1) // te) * te
    ends = jnp.cumsum(padded)
    offs = ends - padded
    d0 = offs[ri[0]] + ri[2]
    d1 = offs[ri[1]] + ri[3]
    tok = jnp.arange(n, dtype=jnp.int32)
    src = jnp.zeros((n_tiles * te,), jnp.int32).at[jnp.concatenate([d0, d1])].set(jnp.concatenate([tok, tok]))
    tile_start = jnp.arange(n_tiles, dtype=jnp.int32) * te
    tile_expert = jnp.minimum(jnp.sum((ends[None, :] <= tile_start[:, None]).astype(jnp.int32), axis=1),
                              N_EXPERTS - 1)
    n_valid = (ends[-1:] // te).astype(jnp.int32)
    return d0, d1, src, tile_expert, n_valid


def _rope_tables(pos):
    half = HEAD_DIM // 2
    inv = ROPE_THETA ** (-jnp.arange(half, dtype=F32) / half)
    ang = pos.astype(F32)[:, None] * inv[None, :]
    cos = jnp.cos(ang)
    sin = jnp.sin(ang)
    cos_t = jnp.concatenate([cos, cos] * (LANES // HEAD_DIM), axis=1)
    sin_t = jnp.concatenate([-sin, sin] * (LANES // HEAD_DIM), axis=1)
    return cos_t, sin_t


def _block_diag(w):
    nb, bw, _ = w.shape
    out = jnp.zeros((nb * bw, nb * bw), w.dtype)
    for i in range(nb):
        out = out.at[i * bw:(i + 1) * bw, i * bw:(i + 1) * bw].set(w[i])
    return out


def kernel(x_prompt, x_sample, cache_kv_cmp, cache_kv_sel, cache_kv_win, state_lru_h, state_lru_conv, state_sconv, page_table, norm1_g, w_in, lru_conv_w, lru_conv_b, lru_wa, lru_ba, lru_wx, lru_bx, lru_lambda, nsa_cmp_wk, nsa_cmp_wv, sc_conv_w, w_out, norm2_g, router_group_w, router_group_b, router_exp_w, router_exp_b, exp_w_gate, exp_w_up, exp_w_down, norm_f_g):
    bsz, seq, d = x_prompt.shape
    dbs, t_len, _ = x_sample.shape
    depth = w_in.shape[0]
    d_a = lru_conv_w.shape[2]
    d_c = sc_conv_w.shape[2]
    kv_w = 2 * N_KV * HEAD_DIM
    d_b = N_HEADS * HEAD_DIM
    n_gate = 3 * N_HEADS
    n_p = bsz * seq
    n_s = dbs * t_len
    n = n_p + n_s
    assert n_s == TOKEN_TILE and n_p % TOKEN_TILE == 0 and seq % TOKEN_TILE == 0
    page = cache_kv_cmp.shape[2]
    past = page_table.shape[1] * page
    n_buf = cache_kv_win.shape[2]

    pos = jnp.concatenate([jnp.arange(seq, dtype=jnp.int32),
                           jnp.tile(past + jnp.arange(t_len, dtype=jnp.int32), dbs)])
    cos_t, sin_t = _rope_tables(pos)

    pool_c = cache_kv_cmp.reshape(depth, -1, page, kv_w)
    pool_s = cache_kv_sel.reshape(depth, -1, page, kv_w)
    win = cache_kv_win.reshape(depth, dbs, n_buf, kv_w)

    g_off = 2 * d_a + d_b + 3 * kv_w
    x = (x_prompt.reshape(n_p, d), x_sample.reshape(n_s, d))
    h0_p = jnp.zeros((bsz, 1, d_a), F32)
    lb0_p = jnp.zeros((bsz, lru_conv_w.shape[1] - 1, d_a), F32)
    sb0_p = jnp.zeros((bsz, sc_conv_w.shape[1] - 1, d_c), F32)

    n_tiles = (2 * n) // EXPERT_TILE + N_EXPERTS
    proj_parts = lambda l: 2 if l < depth - 1 else 1
    states_p, states_s = [], []
    y = None
    for l in range(depth):
        w_l = w_in[l]
        w_r = _split_weight(jnp.concatenate([w_l[:, :g_off], w_l[:, g_off + n_gate:], w_l[:, g_off:g_off + n_gate],
                                             jnp.zeros((d, LANES - n_gate), F32)], axis=1), proj_parts(l))
        lru_in, sc_in, gate, q, kvc, kvs, kvw, kvs_b, kvw_b = _in_proj(
            x, norm1_g[l][None], w_r, cos_t, sin_t, n_p, seq, d_a=d_a, d_b=d_b, kv_w=kv_w, d_c=d_c)

        seq_w = (lru_conv_w[l], lru_conv_b[l][None], _block_diag(lru_wa[l]).astype(MXU_DTYPE), lru_ba[l][None],
                 _block_diag(lru_wx[l]).astype(MXU_DTYPE), lru_bx[l][None], lru_lambda[l][None], sc_conv_w[l])
        out_a, out_c, hn_p, lbn_p, sbn_p = _seq_mix_prompt(lru_in, sc_in, h0_p, lb0_p, sb0_p, seq_w, bsz, seq,
                                                           d_a=d_a, d_c=d_c)
        tmaj = lambda a: jnp.swapaxes(a.reshape(dbs, t_len, -1), 0, 1)
        oa_s, oc_s, hn_s, lbn_s, sbn_s = _seq_mix_sample(
            tmaj(lru_in[n_p:]), tmaj(sc_in[n_p:]), state_lru_h[l], jnp.swapaxes(state_lru_conv[l], 0, 1),
            jnp.swapaxes(state_sconv[l], 0, 1), seq_w, d_a=d_a, d_c=d_c)
        oa_s = jnp.swapaxes(oa_s, 0, 1).reshape(n_s, d_a)
        oc_s = jnp.swapaxes(oc_s, 0, 1).reshape(n_s, d_c)

        w_cmp = jnp.concatenate([jnp.broadcast_to(nsa_cmp_wk[l][:, None], (CMP_BLOCK, kv_w // 2)),
                                 jnp.broadcast_to(nsa_cmp_wv[l][:, None], (CMP_BLOCK, kv_w // 2))], axis=1)
        kcv = _compress_prompt(kvc, w_cmp, bsz, seq)
        out_b = _nsa_prompt(q, gate, kcv, kvs_b, kvw_b, bsz, seq)
        s3 = lambda a: a[n_p:].reshape(dbs, t_len, -1)
        ob_s, nwin_s = _nsa_sample(page_table, s3(q), s3(gate), s3(kvs), s3(kvw), w_cmp, pool_c, pool_s, win, l)
        ob_s = ob_s.reshape(n_s, d_b)

        w_route = _split_weight(jnp.concatenate([router_group_w[l], router_exp_w[l],
                                                 jnp.zeros((d, LANES - N_GROUPS - N_EXPERTS), F32)], axis=1), 2)
        b_route = jnp.concatenate([router_group_b[l], router_exp_b[l],
                                   jnp.zeros((LANES - N_GROUPS - N_EXPERTS,), F32)])[None]
        xn, h2, ri, rw, cnt = _out_proj_router(x, (out_a, out_b, out_c), (oa_s, ob_s, oc_s),
                                               _split_weight(w_out[l], proj_parts(l)), norm2_g[l][None],
                                               w_route, b_route)
        d0, d1, src, tile_expert, n_valid = _route_plan(ri, cnt, n_tiles)
        ys = _experts(tile_expert, n_valid, src, h2, exp_w_gate, exp_w_up, exp_w_down, l)
        if l < depth - 1:
            x, = _combine(d0, d1, xn, rw, norm_f_g[None], ys)
        else:
            y = _combine(d0, d1, xn, rw, norm_f_g[None], ys, n_p)

        kv6 = lambda a, lead: a.reshape(lead + (2, N_KV, HEAD_DIM))
        states_p.append((kv6(kvc[:n_p], (bsz, seq)), kv6(kvs[:n_p], (bsz, seq)),
                         kv6(kvw[:n_p].reshape(bsz, seq, kv_w)[:, seq - min(WINDOW, seq):], (bsz, min(WINDOW, seq))),
                         hn_p[:, 0], lbn_p, sbn_p))
        states_s.append((kv6(kvc[n_p:], (dbs, t_len)), kv6(kvs[n_p:], (dbs, t_len)), kv6(nwin_s, (dbs, n_buf)),
                         hn_s, jnp.swapaxes(lbn_s, 0, 1), jnp.swapaxes(sbn_s, 0, 1)))

    stack = lambda sts, i: jnp.stack([s[i] for s in sts])
    res = [y[0].reshape(bsz, seq, d), y[1].reshape(dbs, t_len, d)]
    for i in range(6):
        res += [stack(states_p, i), stack(states_s, i)]
    return tuple(res)
```
